```python
import math
import jax, jax.numpy as jnp
from jax import lax
import numpy as np

D_MODEL = 1024
BATCH = 32
SEQ = 256
DEPTH = 2
DEC_BATCH = 8
DEC_SEQ = 4096
PAST_LEN = 512

GRID_W = 64
CHUNK = 64
N_BRANCH = 4
MIX_W = 512
RET_H = 4
RET_DK = 64
RET_DV = 128
SSD_H = 8
SSD_P = 64
SSD_N = 64
SSD_G = 2
SSD_CONV_DIM = MIX_W + 2 * SSD_G * SSD_N
RWKV_H = 8
RWKV_D = 64
RWKV_W_LORA = 64
RWKV_A_LORA = 64
RWKV_G_LORA = 128
ML_H = 4
ML_DK = 64
ML_DV = 128
FFN_DIM = 2816
CONV_W = 3
ROPE_BASE = 10000.0
EPS = 1e-6
GN_EPS = 1e-5
RWKV_GN_EPS = 64e-5

IN_SPLITS = (
    RET_H * RET_DK, RET_H * RET_DK, MIX_W, MIX_W,
    MIX_W, SSD_CONV_DIM, 2 * SSD_H,
    3 * MIX_W, RWKV_W_LORA, RWKV_A_LORA, RWKV_G_LORA,
    ML_H * ML_DK, ML_H * ML_DK, MIX_W, MIX_W, 4 * ML_H,
    N_BRANCH * D_MODEL,
)
IN_DIM = sum(IN_SPLITS)

kernel_name = 'bidir_hybrid_ret_ssd_rwkv7_mlstm_diffusion_step'


def _split(a, sizes):
    out, o = [], 0
    for s in sizes:
        out.append(a[..., o:o + s])
        o += s
    return out


def _rev(a):
    return jnp.flip(a, axis=1)


def _rms(x, w):
    xf = x.astype(jnp.float32)
    y = xf * lax.rsqrt(jnp.mean(xf * xf, axis=-1, keepdims=True) + EPS)
    return (y * w).astype(x.dtype)


def _head_norm(y, eps):
    yf = y.astype(jnp.float32)
    mu = jnp.mean(yf, axis=-1, keepdims=True)
    var = jnp.var(yf, axis=-1, keepdims=True)
    return (yf - mu) * lax.rsqrt(var + eps)


def _dwconv3(x, w, b=None):
    xp = jnp.pad(x, ((0, 0), (1, 1), (0, 0)))
    y = xp[:, :-2] * w[0] + xp[:, 1:-1] * w[1] + xp[:, 2:] * w[2]
    return y if b is None else y + b


def _grid_rope(n_tok):
    rows = n_tok // GRID_W
    rr, cc = jnp.meshgrid(jnp.arange(rows), jnp.arange(GRID_W), indexing='ij')
    nf = RET_DK // 4
    inv = ROPE_BASE ** (-jnp.arange(nf, dtype=jnp.float32) / nf)
    ang = jnp.concatenate([rr.reshape(-1, 1) * inv, cc.reshape(-1, 1) * inv], axis=-1)
    return jnp.cos(ang), jnp.sin(ang)


def _apply_rope(x, cos, sin):
    half = x.shape[-1] // 2
    x1, x2 = x[..., :half], x[..., half:]
    c, s = cos[None, :, None, :], sin[None, :, None, :]
    return jnp.concatenate([x1 * c - x2 * s, x1 * s + x2 * c], axis=-1)


def _chunks(a):
    b, t = a.shape[:2]
    return jnp.moveaxis(a.reshape(b, t // CHUNK, CHUNK, *a.shape[2:]), 1, 0)


def _unchunks(a):
    n, b = a.shape[:2]
    return jnp.moveaxis(a, 0, 1).reshape(b, n * CHUNK, *a.shape[3:])


def _decay_scan(q, k, v, logd, s0):
    f32 = jnp.float32
    causal = jnp.tril(jnp.ones((CHUNK, CHUNK), bool))[None, :, :, None]

    def step(s, blk):
        qb, kb, vb, lb = blk
        cs = jnp.cumsum(lb, axis=1)
        dmat = jnp.exp(jnp.where(causal, cs[:, :, None] - cs[:, None], -jnp.inf))
        att = jnp.einsum('blhk,bshk->blsh', qb, kb) * dmat
        y = (jnp.einsum('blsh,bshv->blhv', att, vb)
             + jnp.einsum('blhk,bhkv->blhv', qb * jnp.exp(cs)[..., None], s))
        tail = jnp.exp(cs[:, -1:] - cs)[..., None]
        s = jnp.exp(cs[:, -1])[..., None, None] * s + jnp.einsum('blhk,blhv->bhkv', kb * tail, vb)
        return s, y

    s, ys = lax.scan(step, s0.astype(f32), tuple(_chunks(a.astype(f32)) for a in (q, k, v, logd)))
    return _unchunks(ys), s


def _mlstm_scan(q, k, v, ig, lf, state):
    f32 = jnp.float32
    causal = jnp.tril(jnp.ones((CHUNK, CHUNK), bool))[None, :, :, None]

    def step(carry, blk):
        c, n, m = carry
        qb, kb, vb, ib, fb = blk
        b = jnp.cumsum(fb, axis=1)
        logw = jnp.where(causal, b[:, :, None] - b[:, None] + ib[:, None], -jnp.inf)
        inter = b + m[:, None]
        mt = jnp.maximum(inter, jnp.max(logw, axis=2))
        att = jnp.einsum('blhk,bshk->blsh', qb, kb) * jnp.exp(logw - mt[:, :, None])
        a_in = jnp.exp(inter - mt)
        num = (jnp.einsum('blsh,bshv->blhv', att, vb)
               + a_in[..., None] * jnp.einsum('blhk,bhkv->blhv', qb, c))
        den = jnp.sum(att, axis=2) + a_in * jnp.einsum('blhk,bhk->blh', qb, n)
        y = num / jnp.maximum(jnp.abs(den), jnp.exp(-mt))[..., None]
        b_end = b[:, -1]
        logw_end = b_end[:, None] - b + ib
        m_new = jnp.maximum(b_end + m, jnp.max(logw_end, axis=1))
        w_end = jnp.exp(logw_end - m_new[:, None])[..., None]
        scale = jnp.exp(b_end + m - m_new)
        c = scale[..., None, None] * c + jnp.einsum('blhk,blhv->bhkv', kb * w_end, vb)
        n = scale[..., None] * n + jnp.sum(kb * w_end, axis=1)
        return (c, n, m_new), y

    init = tuple(a.astype(f32) for a in state)
    carry, ys = lax.scan(step, init, tuple(_chunks(a.astype(f32)) for a in (q, k, v, ig, lf)))
    return _unchunks(ys), carry


def _rwkv_scan(r, w, k, v, a, b, s0):
    f32 = jnp.float32

    def step(s, tok):
        rt, wt, kt, vt, at, bt = tok
        sa = jnp.einsum('bhvk,bhk->bhv', s, at)
        s = s * wt[:, :, None] + sa[..., None] * bt[:, :, None] + vt[..., None] * kt[:, :, None]
        return s, jnp.einsum('bhvk,bhk->bhv', s, rt)

    seqs = tuple(jnp.moveaxis(x.astype(f32), 1, 0) for x in (r, w, k, v, a, b))
    s, ys = lax.scan(step, s0.astype(f32), seqs)
    return jnp.moveaxis(ys, 0, 1), s


def _run_dir(scan, seqs, state, backward):
    if backward:
        y, state = scan(*(_rev(a) for a in seqs), state)
        return _rev(y), state
    return scan(*seqs, state)


def _mixer(h, lp, state, rope):
    B, T, _ = h.shape
    f32 = jnp.float32
    (rq, rk, rv, rg, sz, sxbc, sdt, wrkv, wwd, wad, wgd,
     mq, mk, mv, mo, mif, mg) = _split(h @ lp['w_in'], IN_SPLITS)
    st_ret, st_ssd, st_rwkv, st_c, st_n, st_m = state

    q = rq.reshape(B, T, RET_H, RET_DK)
    k = rk.reshape(B, T, RET_H, RET_DK)
    v = rv.reshape(B, T, RET_H, RET_DV)
    if rope is not None:
        q, k = _apply_rope(q, *rope), _apply_rope(k, *rope)
    q = q * RET_DK ** -0.5
    log_gamma = -jnp.exp(lp['ret_log_rate'].astype(f32))
    ys, ss = [], []
    for d in range(2):
        logd = jnp.broadcast_to(log_gamma[d], (B, T, RET_H))
        y, s = _run_dir(_decay_scan, (q, k, v, logd), st_ret[:, d], d == 1)
        ys.append(y)
        ss.append(s)
    y_ret = jax.nn.silu(rg) * _head_norm(ys[0] + ys[1], GN_EPS).reshape(B, T, MIX_W)
    new_ret = jnp.stack(ss, axis=1)

    xbc = jax.nn.silu(_dwconv3(sxbc, lp['ssd_conv_w'], lp['ssd_conv_b']))
    xs, bm, cm = _split(xbc, (MIX_W, SSD_G * SSD_N, SSD_G * SSD_N))
    xs = xs.reshape(B, T, SSD_H, SSD_P).astype(f32)
    bm = jnp.repeat(bm.reshape(B, T, SSD_G, SSD_N), SSD_H // SSD_G, axis=2)
    cm = jnp.repeat(cm.reshape(B, T, SSD_G, SSD_N), SSD_H // SSD_G, axis=2)
    dt = jax.nn.softplus(sdt.reshape(B, T, 2, SSD_H).astype(f32) + lp['ssd_dt_bias'])
    a_neg = -jnp.exp(lp['ssd_A_log'].astype(f32))
    ys, ss = [], []
    for d in range(2):
        seqs = (cm, bm, xs * dt[:, :, d, :, None], dt[:, :, d] * a_neg[d])
        y, s = _run_dir(_decay_scan, seqs, st_ssd[:, d], d == 1)
        ys.append(y)
        ss.append(s)
    y = ys[0] + ys[1] + (lp['ssd_D'][0] + lp['ssd_D'][1])[:, None] * xs
    y_ssd = _rms(y.reshape(B, T, MIX_W) * jax.nn.silu(sz), lp['ssd_norm'])
    new_ssd = jnp.stack(ss, axis=1)

    hd = (RWKV_H, RWKV_D)
    r_, k_, v_ = (a.reshape(B, T, *hd).astype(f32)
                  for a in _split(_dwconv3(wrkv, lp['rwkv_conv_w']), (MIX_W,) * 3))
    g = jax.nn.sigmoid(wgd) @ lp['rwkv_g_up']
    kk = k_ * lp['rwkv_k_k'].reshape(hd)
    kk = kk / jnp.maximum(jnp.linalg.norm(kk, axis=-1, keepdims=True), 1e-12)
    w_low = jnp.tanh(wwd)
    r_bonus = lp['rwkv_r_k'].reshape(hd)
    ys, ss, bonus = [], [], []
    for d in range(2):
        w_log = -jax.nn.softplus(-(lp['rwkv_w0'][d] + w_low @ lp['rwkv_w_up'][d])) - 0.5
        decay = jnp.exp(-jnp.exp(w_log.astype(f32))).reshape(B, T, *hd)
        iclr = jax.nn.sigmoid(lp['rwkv_a0'][d] + wad @ lp['rwkv_a_up'][d]).astype(f32).reshape(B, T, *hd)
        kd = k_ * (1.0 + (iclr - 1.0) * lp['rwkv_k_a'].reshape(hd))
        y, s = _run_dir(_rwkv_scan, (r_, decay, kd, v_, -kk, kk * iclr), st_rwkv[:, d], d == 1)
        ys.append(y)
        ss.append(s)
        bonus.append(jnp.sum(r_ * kd * r_bonus, axis=-1, keepdims=True) * v_)
    y_rw = (_head_norm(ys[0] + ys[1], RWKV_GN_EPS).reshape(B, T, MIX_W) * lp['rwkv_ln_w']
            + lp['rwkv_ln_b'] + (bonus[0] + bonus[1]).reshape(B, T, MIX_W)) * g
    new_rwkv = jnp.stack(ss, axis=1)

    q = mq.reshape(B, T, ML_H, ML_DK) * ML_DK ** -0.5
    k = mk.reshape(B, T, ML_H, ML_DK)
    v = mv.reshape(B, T, ML_H, ML_DV)
    gif = mif.reshape(B, T, 2, 2, ML_H).astype(f32)
    i_pre = gif[:, :, :, 0] + lp['ml_i_bias']
    log_f = jax.nn.log_sigmoid(gif[:, :, :, 1] + lp['ml_f_bias'])
    ys, ss = [], []
    for d in range(2):
        y, s = _run_dir(_mlstm_scan, (q, k, v, i_pre[:, :, d], log_f[:, :, d]),
                        (st_c[:, d], st_n[:, d], st_m[:, d]), d == 1)
        ys.append(y)
        ss.append(s)
    y_ml = jax.nn.sigmoid(mo) * (_head_norm(ys[0] + ys[1], GN_EPS).reshape(B, T, MIX_W) * lp['ml_norm'])
    new_c = jnp.stack([s[0] for s in ss], axis=1)
    new_n = jnp.stack([s[1] for s in ss], axis=1)
    new_m = jnp.stack([s[2] for s in ss], axis=1)

    gates = jax.nn.sigmoid(mg.reshape(B, T, N_BRANCH, D_MODEL))
    merged = None
    for i, br in enumerate((y_ret, y_ssd, y_rw, y_ml)):
        term = gates[:, :, i] * (br.astype(h.dtype) @ lp['w_branch'][i])
        merged = term if merged is None else merged + term
    out = (merged @ lp['w_out']).astype(h.dtype)
    return out, (new_ret, new_ssd, new_rwkv, new_c, new_n, new_m)


def _conv_ffn(h, lp):
    u = _dwconv3(h @ lp['ffn_up'], lp['ffn_conv_w'], lp['ffn_conv_b'])
    val, gate = jnp.split(u, 2, axis=-1)
    return (val * jax.nn.silu(gate)) @ lp['ffn_down']


def _layer(x, mod, lp, state, rope):
    sh1, sc1, g1, sh2, sc2, g2 = (mod[:, i, None, :] for i in range(6))
    h = _rms(x, lp['norm1']) * (1.0 + sc1) + sh1
    mix, state = _mixer(h, lp, state, rope)
    x = x + g1 * mix
    h = _rms(x, lp['norm2']) * (1.0 + sc2) + sh2
    x = x + g2 * _conv_ffn(h, lp)
    return x, state


def _zero_states(b):
    def z(*s):
        return jnp.zeros((b, 2) + s, jnp.float32)
    return (z(RET_H, RET_DK, RET_DV), z(SSD_H, SSD_N, SSD_P), z(RWKV_H, RWKV_D, RWKV_D),
            z(ML_H, ML_DK, ML_DV), z(ML_H, ML_DK), z(ML_H))


def setup_inputs(seed: int = 0) -> dict:
    key = jax.random.key(seed)
    keys = iter(jax.random.split(key, 64))
    f32 = jnp.float32

    def nrm(shape, scale=1.0):
        return scale * jax.random.normal(next(keys), shape, f32)

    def unif(shape, lo, hi):
        return jax.random.uniform(next(keys), shape, f32, lo, hi)

    L, D, F, sb = DEPTH, D_MODEL, FFN_DIM, DEC_BATCH
    dt0 = jnp.exp(unif((L, 2, SSD_H), math.log(1e-3), math.log(1e-1)))
    return {
        'x_prompt': nrm((BATCH, SEQ, D)),
        'x_sample': nrm((DEC_BATCH, DEC_SEQ, D)),
        'state_ret': nrm((sb, L, 2, RET_H, RET_DK, RET_DV), 0.5),
        'state_ssd': nrm((sb, L, 2, SSD_H, SSD_N, SSD_P), 0.5),
        'state_rwkv': nrm((sb, L, 2, RWKV_H, RWKV_D, RWKV_D), 0.5),
        'state_mlstm_c': nrm((sb, L, 2, ML_H, ML_DK, ML_DV), 0.5),
        'state_mlstm_n': nrm((sb, L, 2, ML_H, ML_DK), 0.5),
        'state_mlstm_m': nrm((sb, L, 2, ML_H)),
        'c': nrm((sb, D)),
        'c_ctx': nrm((D,)),
        'ada_w': nrm((L, D, 6 * D), 0.5 * D ** -0.5),
        'ada_b': nrm((L, 6 * D), 0.02),
        'norm1': 1.0 + nrm((L, D), 0.02),
        'norm2': 1.0 + nrm((L, D), 0.02),
        'w_in': nrm((L, D, IN_DIM), D ** -0.5),
        'ret_log_rate': -(5.0 + jnp.arange(RET_H, dtype=f32)) * math.log(2.0) + nrm((L, 2, RET_H), 0.1),
        'ssd_conv_w': nrm((L, CONV_W, SSD_CONV_DIM), CONV_W ** -0.5),
        'ssd_conv_b': nrm((L, SSD_CONV_DIM), 0.02),
        'ssd_A_log': jnp.log(unif((L, 2, SSD_H), 1.0, 16.0)),
        'ssd_dt_bias': dt0 + jnp.log(-jnp.expm1(-dt0)),
        'ssd_D': 1.0 + nrm((L, 2, SSD_H), 0.1),
        'ssd_norm': 1.0 + nrm((L, MIX_W), 0.02),
        'rwkv_conv_w': nrm((L, CONV_W, 3 * MIX_W), CONV_W ** -0.5),
        'rwkv_w0': jnp.linspace(-6.0, -1.0, MIX_W, dtype=f32) + nrm((L, 2, MIX_W), 0.1),
        'rwkv_w_up': nrm((L, 2, RWKV_W_LORA, MIX_W), 0.1),
        'rwkv_a0': nrm((L, 2, MIX_W), 0.1),
        'rwkv_a_up': nrm((L, 2, RWKV_A_LORA, MIX_W), 0.5 * RWKV_A_LORA ** -0.5),
        'rwkv_g_up': nrm((L, RWKV_G_LORA, MIX_W), RWKV_G_LORA ** -0.5),
        'rwkv_k_k': 0.85 + nrm((L, MIX_W), 0.02),
        'rwkv_k_a': 1.0 + nrm((L, MIX_W), 0.02),
        'rwkv_r_k': nrm((L, MIX_W), 0.1),
        'rwkv_ln_w': 1.0 + nrm((L, MIX_W), 0.02),
        'rwkv_ln_b': nrm((L, MIX_W), 0.02),
        'ml_i_bias': nrm((L, 2, ML_H), 0.1) - 1.0,
        'ml_f_bias': jnp.linspace(3.0, 6.0, ML_H, dtype=f32) + nrm((L, 2, ML_H), 0.1),
        'ml_norm': 1.0 + nrm((L, MIX_W), 0.02),
        'w_branch': nrm((L, N_BRANCH, MIX_W, D), MIX_W ** -0.5),
        'w_out': nrm((L, D, D), D ** -0.5),
        'ffn_up': nrm((L, D, 2 * F), D ** -0.5),
        'ffn_conv_w': nrm((L, CONV_W, 2 * F), CONV_W ** -0.5),
        'ffn_conv_b': nrm((L, 2 * F), 0.02),
        'ffn_down': nrm((L, F, D), F ** -0.5),
        'final_norm': 1.0 + nrm((D,), 0.02),
    }


def reference(x_prompt, x_sample, state_ret, state_ssd, state_rwkv, state_mlstm_c, state_mlstm_n,
              state_mlstm_m, c, c_ctx, ada_w, ada_b, norm1, norm2, w_in, ret_log_rate, ssd_conv_w,
              ssd_conv_b, ssd_A_log, ssd_dt_bias, ssd_D, ssd_norm, rwkv_conv_w, rwkv_w0, rwkv_w_up,
              rwkv_a0, rwkv_a_up, rwkv_g_up, rwkv_k_k, rwkv_k_a, rwkv_r_k, rwkv_ln_w, rwkv_ln_b,
              ml_i_bias, ml_f_bias, ml_norm, w_branch, w_out, ffn_up, ffn_conv_w, ffn_conv_b,
              ffn_down, final_norm):
    rope = _grid_rope(x_sample.shape[1])
    xp, xs = x_prompt, x_sample
    ctx_states = []
    for l in range(DEPTH):
        lp = dict(norm1=norm1[l], norm2=norm2[l], w_in=w_in[l], ret_log_rate=ret_log_rate[l],
                  ssd_conv_w=ssd_conv_w[l], ssd_conv_b=ssd_conv_b[l], ssd_A_log=ssd_A_log[l],
                  ssd_dt_bias=ssd_dt_bias[l], ssd_D=ssd_D[l], ssd_norm=ssd_norm[l],
                  rwkv_conv_w=rwkv_conv_w[l], rwkv_w0=rwkv_w0[l], rwkv_w_up=rwkv_w_up[l],
                  rwkv_a0=rwkv_a0[l], rwkv_a_up=rwkv_a_up[l], rwkv_g_up=rwkv_g_up[l],
                  rwkv_k_k=rwkv_k_k[l], rwkv_k_a=rwkv_k_a[l], rwkv_r_k=rwkv_r_k[l],
                  rwkv_ln_w=rwkv_ln_w[l], rwkv_ln_b=rwkv_ln_b[l], ml_i_bias=ml_i_bias[l],
                  ml_f_bias=ml_f_bias[l], ml_norm=ml_norm[l], w_branch=w_branch[l], w_out=w_out[l],
                  ffn_up=ffn_up[l], ffn_conv_w=ffn_conv_w[l], ffn_conv_b=ffn_conv_b[l],
                  ffn_down=ffn_down[l])
        mod_ctx = (jax.nn.silu(c_ctx)[None] @ ada_w[l] + ada_b[l]).reshape(1, 6, D_MODEL)
        mod_lat = (jax.nn.silu(c) @ ada_w[l] + ada_b[l]).reshape(c.shape[0], 6, D_MODEL)
        xp, st = _layer(xp, mod_ctx, lp, _zero_states(xp.shape[0]), None)
        ctx_states.append(st)
        lat_init = (state_ret[:, l], state_ssd[:, l], state_rwkv[:, l],
                    state_mlstm_c[:, l], state_mlstm_n[:, l], state_mlstm_m[:, l])
        xs, _ = _layer(xs, mod_lat, lp, lat_init, rope)
    y_prompt = _rms(xp, final_norm)
    y_sample = _rms(xs, final_norm)
    new_ret = jnp.stack([s[0] for s in ctx_states], axis=1)
    new_ssd = jnp.stack([s[1] for s in ctx_states], axis=1)
    new_rwkv = jnp.stack([s[2] for s in ctx_states], axis=1)
    new_c = jnp.stack([s[3] for s in ctx_states], axis=1)
    new_n = jnp.stack([s[4] for s in ctx_states], axis=1)
    new_m = jnp.stack([s[5] for s in ctx_states], axis=1)
    return (y_prompt, y_sample, new_ret, new_ssd, new_rwkv, new_c, new_n, new_m)
```

```python
import functools
import math

import jax
import jax.numpy as jnp
import numpy as np
from jax import lax
from jax.experimental import pallas as pl
from jax.experimental.pallas import tpu as pltpu

F32 = jnp.float32
BF16 = jnp.bfloat16
HI = lax.Precision.HIGHEST

D_MODEL = 1024
DEPTH = 2
GRID_W = 64
CHUNK = 64
N_BRANCH = 4
MIX_W = 512
RET_H, RET_DK, RET_DV = 4, 64, 128
SSD_H, SSD_P, SSD_N, SSD_G = 8, 64, 64, 2
RWKV_H, RWKV_D = 8, 64
RWKV_W_LORA, RWKV_A_LORA, RWKV_G_LORA = 64, 64, 128
ML_H, ML_DK, ML_DV = 4, 64, 128
FFN_DIM = 2816
ROPE_BASE = 10000.0
EPS = 1e-6
GN_EPS = 1e-5
RWKV_GN_EPS = 64e-5

LANES = 128
VMEM_LIMIT = 56 * 1024 * 1024

_O_RQ, _O_RK, _O_RV, _O_RG = 0, 256, 512, 1024
_O_SZ, _O_SX, _O_SB, _O_SC, _O_SDT = 1536, 2048, 2560, 2688, 2816
_O_WRKV, _O_WWD, _O_WAD, _O_WGD = 2832, 4368, 4432, 4496
_O_MQ, _O_MK, _O_MV, _O_MO, _O_MIF, _O_MG = 4624, 4880, 5136, 5648, 6160, 6176

M_GATE, M_RQ, M_RK, M_RV, M_RG = 0, 4096, 4352, 4608, 5120
M_SZ, M_RWKV = 5632, 6144
M_MQ, M_MK, M_MV, M_MO = 7680, 7936, 8192, 8704
M_SX, M_SB, M_SC = 9216, 9728, 9984
N_MAIN = 10240
S_WGD, S_WWD, S_WAD, S_DT, S_MIF = 0, 128, 192, 256, 272
N_SMALL = 384


def _main_perm():
    idx = np.full((N_MAIN,), -1, np.int64)

    def put(dst, src, n):
        idx[dst:dst + n] = np.arange(src, src + n)

    put(M_GATE, _O_MG, 4096)
    put(M_RQ, _O_RQ, 256); put(M_RK, _O_RK, 256); put(M_RV, _O_RV, 512); put(M_RG, _O_RG, 512)
    put(M_SZ, _O_SZ, 512); put(M_RWKV, _O_WRKV, 1536)
    put(M_MQ, _O_MQ, 256); put(M_MK, _O_MK, 256); put(M_MV, _O_MV, 512); put(M_MO, _O_MO, 512)
    put(M_SX, _O_SX, 512)
    for g in range(SSD_G):
        put(M_SB + g * LANES, _O_SB + g * SSD_N, SSD_N)
        put(M_SC + g * LANES, _O_SC + g * SSD_N, SSD_N)
    return idx


def _small_perm():
    idx = np.full((N_SMALL,), -1, np.int64)
    idx[S_WGD:S_WGD + 128] = np.arange(_O_WGD, _O_WGD + 128)
    idx[S_WWD:S_WWD + 64] = np.arange(_O_WWD, _O_WWD + 64)
    idx[S_WAD:S_WAD + 64] = np.arange(_O_WAD, _O_WAD + 64)
    idx[S_DT:S_DT + 16] = np.arange(_O_SDT, _O_SDT + 16)
    idx[S_MIF:S_MIF + 16] = np.arange(_O_MIF, _O_MIF + 16)
    return idx


def _ssd_conv_perm():
    idx = np.full((1024,), -1, np.int64)
    idx[0:512] = np.arange(0, 512)
    for g in range(SSD_G):
        idx[512 + g * LANES:512 + g * LANES + SSD_N] = np.arange(512 + g * SSD_N, 512 + (g + 1) * SSD_N)
        idx[768 + g * LANES:768 + g * LANES + SSD_N] = np.arange(640 + g * SSD_N, 640 + (g + 1) * SSD_N)
    return idx


def _take_cols(a, idx):
    safe = np.where(idx < 0, 0, idx)
    out = jnp.take(a, jnp.asarray(safe, jnp.int32), axis=-1)
    return jnp.where(jnp.asarray(idx >= 0), out, 0).astype(a.dtype)


def _params(sem, vmem=VMEM_LIMIT):
    return pltpu.CompilerParams(dimension_semantics=sem, vmem_limit_bytes=vmem)


def _tdot(a, b, **kw):
    return lax.dot_general(a, b, (((0,), (0,)), ((), ())), preferred_element_type=F32, **kw)


def _ntdot(a, b, **kw):
    return lax.dot_general(a, b, (((1,), (1,)), ((), ())), preferred_element_type=F32, **kw)


def _dot(a, b, **kw):
    return jnp.dot(a, b, preferred_element_type=F32, **kw)


def _silu(x):
    return x * jax.nn.sigmoid(x)


def _softplus(x):
    return jnp.maximum(x, 0.0) + jnp.log1p(jnp.exp(-jnp.abs(x)))


def _iota(shape, dim):
    return lax.broadcasted_iota(jnp.int32, shape, dim)


def _mod_body(c_ref, w_ref, b_ref, o_ref):
    c = c_ref[...]
    o_ref[...] = _dot(_silu(c).astype(BF16), w_ref[...].astype(BF16)) + b_ref[...]


def _modulation(cvec, ada_w, ada_b):
    L = ada_w.shape[0]
    tn = 512
    return pl.pallas_call(
        _mod_body,
        grid=(L, 6 * D_MODEL // tn),
        in_specs=[pl.BlockSpec((16, D_MODEL), lambda l, j: (0, 0)),
                  pl.BlockSpec((None, D_MODEL, tn), lambda l, j: (l, 0, j)),
                  pl.BlockSpec((None, 1, tn), lambda l, j: (l, 0, j))],
        out_specs=pl.BlockSpec((None, 16, tn), lambda l, j: (l, 0, j)),
        out_shape=jax.ShapeDtypeStruct((L, 16, 6 * D_MODEL), F32),
        compiler_params=_params(("parallel", "parallel")),
        name="modulation",
    )(cvec, ada_w, ada_b.reshape(L, 1, 6 * D_MODEL))


def _rms_mod(x, nw, shift, scale):
    ms = jnp.mean(x * x, axis=-1, keepdims=True)
    return (x * lax.rsqrt(ms + EPS) * nw) * (1.0 + scale) + shift


def _in_proj_body(x_ref, mod_ref, nw_ref, w_ref, o_ref, h_ref):
    @pl.when(pl.program_id(1) == 0)
    def _():
        h = _rms_mod(x_ref[...], nw_ref[...], mod_ref[0:1, :], mod_ref[1:2, :])
        h_ref[...] = h.astype(BF16)

    o_ref[...] = _dot(h_ref[...], w_ref[...]).astype(o_ref.dtype)


def _mod_index(n_mod, tm, seq):
    if n_mod == 1:
        return lambda i: 0
    return lambda i: (i * tm) // seq


def _in_proj(x, mod, nw, w, out_dtype, tm, tn, seq, name):
    n_tok, n = x.shape[0], w.shape[1]
    mi = _mod_index(mod.shape[0], tm, seq)
    return pl.pallas_call(
        _in_proj_body,
        grid=(n_tok // tm, n // tn),
        in_specs=[pl.BlockSpec((tm, D_MODEL), lambda i, j: (i, 0)),
                  pl.BlockSpec((None, 8, D_MODEL), lambda i, j: (mi(i), 0, 0)),
                  pl.BlockSpec((1, D_MODEL), lambda i, j: (0, 0)),
                  pl.BlockSpec((D_MODEL, tn), lambda i, j: (0, j))],
        out_specs=pl.BlockSpec((tm, tn), lambda i, j: (i, j)),
        out_shape=jax.ShapeDtypeStruct((n_tok, n), out_dtype),
        scratch_shapes=[pltpu.VMEM((tm, D_MODEL), BF16)],
        compiler_params=_params(("parallel", "arbitrary")),
        name=name,
    )(x, mod, nw, w)


def _pair_masks():
    lane = _iota((1, LANES), 1)
    return lane < 64, lane


def _head_norm(y, eps):
    mu = jnp.mean(y, axis=-1, keepdims=True)
    d = y - mu
    var = jnp.mean(d * d, axis=-1, keepdims=True)
    return d * lax.rsqrt(var + eps)


def _blockdiag_rows(v, width):
    col = _iota((1, 2 * width), 1)
    left = col < width
    return jnp.concatenate([jnp.where(left, v, 0), jnp.where(left, 0, v)], axis=0)


def _tri_masks():
    ii = _iota((CHUNK, LANES), 0)
    jj = _iota((CHUNK, LANES), 1) % CHUNK
    return jj <= ii, jj >= ii, ii - jj


def _cumsum_mats():
    r = _iota((2 * CHUNK, CHUNK), 0)
    t = _iota((2 * CHUNK, CHUNK), 1)
    pre = (r < CHUNK) & (t <= r)
    suf = (r >= CHUNK) & (t >= r - CHUNK)
    return jnp.where(pre | suf, 1.0, 0.0).astype(F32)


def _lane_expand(x, lane0):
    l = _iota((LANES, LANES), 0)
    c = _iota((LANES, LANES), 1)
    sel = jnp.where(l == lane0 + jnp.where(c >= 64, 1, 0), 1.0, 0.0).astype(F32)
    return _dot(x, sel, precision=HI)


def _lane_rows(x, lane0):
    lane = _iota((1, LANES), 1)
    even = (lane % 2) == 0
    stack = jnp.concatenate([jnp.where(even, x, 0.0), jnp.where(even, 0.0, x)], axis=0)
    off = _iota((8, LANES), 1) - lane0
    pick = jnp.where((off == 0) | (off == 1), 1.0, 0.0).astype(F32)
    return _ntdot(pick, stack, precision=HI)[0:1, :]


def _ret_body(*refs, rope, has_s0, want_state, seq):
    it = iter(refs)
    lg_ref, q_ref, k_ref, v_ref, g_ref = (next(it) for _ in range(5))
    cos_ref = sin_ref = s0_ref = sn_ref = None
    if rope:
        cos_ref, sin_ref = next(it), next(it)
    if has_s0:
        s0_ref = next(it)
    y_ref = next(it)
    if want_state:
        sn_ref = next(it)
    qs, ks, sb_all = next(it), next(it), next(it)

    C = CHUNK
    n = seq // C
    hp = pl.program_id(1)
    m0, lane = _pair_masks()
    lgf0, lgf1 = lg_ref[0, 2 * hp], lg_ref[0, 2 * hp + 1]
    lgb0, lgb1 = lg_ref[1, 2 * hp], lg_ref[1, 2 * hp + 1]
    lgf = jnp.where(m0, lgf0, lgf1)
    lgb = jnp.where(m0, lgb0, lgb1)
    pos = _iota((C, 1), 0).astype(F32)
    e_qf = jnp.exp(lgf * (pos + 1.0))
    e_qb = jnp.exp(lgb * (C - pos))
    e_kf = jnp.exp(lgf * (C - 1.0 - pos))
    e_kb = jnp.exp(lgb * pos)
    row0 = _iota((LANES, 1), 0) < 64
    dec_f = jnp.exp(jnp.where(row0, lgf0, lgf1) * C)
    dec_b = jnp.exp(jnp.where(row0, lgb0, lgb1) * C)
    bd = (_iota((LANES, 2 * LANES), 0) < 64) == (_iota((LANES, 2 * LANES), 1) < LANES)
    _, _, diff = _tri_masks()
    dif = diff.astype(F32)
    dcomb = jnp.where(diff > 0, jnp.exp(lgf * jnp.maximum(dif, 0.0)),
                      jnp.where(diff < 0, jnp.exp(lgb * jnp.maximum(-dif, 0.0)), 2.0))

    rb = min(seq, 512)
    m32 = (lane % 64) < 32

    def swap(x):
        return jnp.where(m32, pltpu.roll(x, 96, 1), pltpu.roll(x, 32, 1))

    def prep(i, carry):
        r0 = pl.multiple_of(i * rb, rb)
        q = q_ref[pl.ds(r0, rb), :].astype(F32)
        k = k_ref[pl.ds(r0, rb), :].astype(F32)
        if rope:
            cs = cos_ref[pl.ds(r0, rb), :]
            sn = sin_ref[pl.ds(r0, rb), :]
            q = q * cs + swap(q) * sn
            k = k * cs + swap(k) * sn
        qs[pl.ds(r0, rb), :] = (q * RET_DK ** -0.5).astype(BF16)
        ks[pl.ds(r0, rb), :] = k.astype(BF16)
        return carry

    lax.fori_loop(0, seq // rb, prep, 0)

    def bd_state(d):
        if not has_s0:
            return jnp.zeros((LANES, 2 * LANES), F32)
        z = jnp.zeros((RET_DK, RET_DV), F32)
        return jnp.concatenate([jnp.concatenate([s0_ref[d, 0], z], axis=1),
                                jnp.concatenate([z, s0_ref[d, 1]], axis=1)], axis=0)

    def bstep(i, sb):
        c = n - 1 - i
        r0 = pl.multiple_of(c * C, C)
        sb_all[c] = sb.astype(BF16)
        kb = (ks[pl.ds(r0, C), :].astype(F32) * e_kb).astype(BF16)
        kv = _tdot(kb, v_ref[pl.ds(r0, C), :])
        return dec_b * sb + jnp.where(bd, kv, 0.0)

    sb_fin = lax.fori_loop(0, n, bstep, bd_state(1))

    def fstep(c, sf):
        r0 = pl.multiple_of(c * C, C)
        qc = qs[pl.ds(r0, C), :]
        kc = ks[pl.ds(r0, C), :]
        vc = v_ref[pl.ds(r0, C), :]
        qf = qc.astype(F32)
        k2 = jnp.concatenate([jnp.where(m0, kc, 0), jnp.where(m0, 0, kc)], axis=0)
        att = _ntdot(qc, k2) * dcomb
        lhs = jnp.concatenate([att.astype(BF16), (qf * e_qf).astype(BF16), (qf * e_qb).astype(BF16)], axis=1)
        rhs = jnp.concatenate([_blockdiag_rows(vc, LANES), sf.astype(BF16), sb_all[c]], axis=0)
        y = _dot(lhs, rhs)
        y = jnp.concatenate([_head_norm(y[:, :LANES], GN_EPS), _head_norm(y[:, LANES:], GN_EPS)], axis=1)
        y_ref[pl.ds(r0, C), :] = (_silu(g_ref[pl.ds(r0, C), :].astype(F32)) * y).astype(y_ref.dtype)
        kf = (kc.astype(F32) * e_kf).astype(BF16)
        return dec_f * sf + jnp.where(bd, _tdot(kf, vc), 0.0)

    sf_fin = lax.fori_loop(0, n, fstep, bd_state(0))

    if want_state:
        for d, s in ((0, sf_fin), (1, sb_fin)):
            sn_ref[d, 0] = s[:RET_DK, :RET_DV]
            sn_ref[d, 1] = s[RET_DK:, RET_DV:]


def _retention(lg, p16, g_rows, cos, sin, s0, n_batch, seq, want_state):
    rope = cos is not None
    has_s0 = s0 is not None
    n_tok = n_batch * seq
    in_specs = [pl.BlockSpec(memory_space=pltpu.SMEM),
                pl.BlockSpec((seq, LANES), lambda b, p: (b, M_RQ // LANES + p)),
                pl.BlockSpec((seq, LANES), lambda b, p: (b, M_RK // LANES + p)),
                pl.BlockSpec((seq, 2 * LANES), lambda b, p: (b, M_RV // 256 + p)),
                pl.BlockSpec((seq, 2 * LANES), lambda b, p: (b, M_RG // 256 + p))]
    args = [lg, p16, p16, p16, p16]
    if rope:
        in_specs += [pl.BlockSpec((seq, LANES), lambda b, p: (0, 0))] * 2
        args += [cos, sin]
    st_spec = pl.BlockSpec((None, 2, 2, RET_DK, RET_DV), lambda b, p: (b, 0, p, 0, 0))
    if has_s0:
        in_specs.append(st_spec)
        args.append(s0)
    out_specs = [pl.BlockSpec((seq, 2 * LANES), lambda b, p: (b, p))]
    out_shape = [jax.ShapeDtypeStruct((n_tok, MIX_W), BF16)]
    if want_state:
        out_specs.append(st_spec)
        out_shape.append(jax.ShapeDtypeStruct((n_batch, 2, RET_H, RET_DK, RET_DV), F32))
    res = pl.pallas_call(
        functools.partial(_ret_body, rope=rope, has_s0=has_s0, want_state=want_state, seq=seq),
        grid=(n_batch, RET_H // 2),
        in_specs=in_specs, out_specs=out_specs, out_shape=out_shape,
        scratch_shapes=[pltpu.VMEM((seq, LANES), BF16), pltpu.VMEM((seq, LANES), BF16),
                        pltpu.VMEM((seq // CHUNK, LANES, 2 * LANES), BF16)],
        compiler_params=_params(("parallel", "parallel")),
        name="retention",
    )(*args)
    return res if want_state else (res[0], None)


def _ml_gate_terms(gc, cum, lane_i, lane_f, reverse):
    lf = -_softplus(-gc)
    cs = _dot(cum, lf, precision=HI)
    b = cs[CHUNK:] if reverse else cs[:CHUNK]
    bcol = _lane_expand(b, lane_f)
    icol = _lane_expand(gc, lane_i)
    brow = _lane_rows(b, lane_f)
    irow = _lane_rows(gc, lane_i)
    return bcol, icol, brow, irow


def _ml_update(kc, vc, bcol, icol, cst, n_x, m_x, reverse, bd, row0):
    C = CHUNK
    b_end = bcol[0:1] if reverse else bcol[C - 1:C]
    lwe = b_end - bcol + icol
    m_new = jnp.maximum(b_end + m_x, jnp.max(lwe, axis=0, keepdims=True))
    w_end = jnp.exp(lwe - m_new)
    scale = jnp.exp(b_end + m_x - m_new)
    kw = kc.astype(F32) * w_end
    n_new = scale * n_x + jnp.sum(kw, axis=0, keepdims=True)
    scale_rows = jnp.where(row0, scale[:, 0:1], scale[:, 64:65])
    c_new = scale_rows * cst + jnp.where(bd, _tdot(kw.astype(BF16), vc), 0.0)
    return c_new, n_new, m_new


def _ml_dir_lhs(s2, qf, bcol, brow, irow, n_x, m_x, mask, m0):
    logw = jnp.where(mask, bcol - brow + irow, -jnp.inf)
    inter = bcol + m_x
    mx = jnp.where(m0, jnp.max(jnp.where(m0, logw, -jnp.inf), axis=1, keepdims=True),
                   jnp.max(jnp.where(m0, -jnp.inf, logw), axis=1, keepdims=True))
    mt = jnp.maximum(inter, mx)
    att = s2 * jnp.exp(logw - mt)
    a_in = jnp.exp(inter - mt)
    qn = qf * n_x
    rs = jnp.where(m0, jnp.sum(jnp.where(m0, att, 0.0), axis=1, keepdims=True),
                   jnp.sum(jnp.where(m0, 0.0, att), axis=1, keepdims=True))
    qs = jnp.where(m0, jnp.sum(jnp.where(m0, qn, 0.0), axis=1, keepdims=True),
                   jnp.sum(jnp.where(m0, 0.0, qn), axis=1, keepdims=True))
    den = rs + a_in * qs
    sc = 1.0 / jnp.maximum(jnp.abs(den), jnp.exp(-mt))
    return (att * sc).astype(BF16), (qf * (a_in * sc)).astype(BF16)


def _ml_body(*refs, has_s0, want_state, seq):
    it = iter(refs)
    q_ref, k_ref, v_ref, o_ref, gt_ref, gb_ref, nw_ref = (next(it) for _ in range(7))
    s0_ref = nm0_ref = sn_ref = nmn_ref = None
    if has_s0:
        s0_ref, nm0_ref = next(it), next(it)
    y_ref = next(it)
    if want_state:
        sn_ref, nmn_ref = next(it), next(it)
    cb_all, nmb_all = next(it), next(it)

    C = CHUNK
    n = seq // C
    hp = pl.program_id(1)
    m0, _ = _pair_masks()
    row0 = _iota((LANES, 1), 0) < 64
    bd = (_iota((LANES, 2 * LANES), 0) < 64) == (_iota((LANES, 2 * LANES), 1) < LANES)
    tril, triu, _ = _tri_masks()
    cum = _cumsum_mats()
    gbias = gb_ref[...]
    lane_i = [16 + d * 8 + 2 * hp for d in range(2)]
    lane_f = [20 + d * 8 + 2 * hp for d in range(2)]

    def init(d):
        if not has_s0:
            return (jnp.zeros((LANES, 2 * LANES), F32), jnp.zeros((1, LANES), F32), jnp.zeros((1, LANES), F32))
        z = jnp.zeros((ML_DK, ML_DV), F32)
        cst = jnp.concatenate([jnp.concatenate([s0_ref[d, 0], z], axis=1),
                               jnp.concatenate([z, s0_ref[d, 1]], axis=1)], axis=0)
        return cst, nm0_ref[d:d + 1, :], nm0_ref[2 + d:3 + d, :]

    def bstep(i, carry):
        cst, n_x, m_x = carry
        c = n - 1 - i
        r0 = pl.multiple_of(c * C, C)
        cb_all[c] = cst.astype(BF16)
        nmb_all[c, 0:1, :] = n_x
        nmb_all[c, 1:2, :] = m_x
        gc = gt_ref[pl.ds(r0, C), :] + gbias
        bcol, icol, _, _ = _ml_gate_terms(gc, cum, lane_i[1], lane_f[1], True)
        return _ml_update(k_ref[pl.ds(r0, C), :], v_ref[pl.ds(r0, C), :], bcol, icol, cst, n_x, m_x, True, bd, row0)

    cb_fin = lax.fori_loop(0, n, bstep, init(1))

    def fstep(c, carry):
        cst, n_x, m_x = carry
        r0 = pl.multiple_of(c * C, C)
        qc = q_ref[pl.ds(r0, C), :]
        kc = k_ref[pl.ds(r0, C), :]
        vc = v_ref[pl.ds(r0, C), :]
        qf = qc.astype(F32) * ML_DK ** -0.5
        k2 = jnp.concatenate([jnp.where(m0, kc, 0), jnp.where(m0, 0, kc)], axis=0)
        s2 = _ntdot(qf.astype(BF16), k2)
        gc = gt_ref[pl.ds(r0, C), :] + gbias
        bcf, icf, brf, irf = _ml_gate_terms(gc, cum, lane_i[0], lane_f[0], False)
        bcb, _, brb, irb = _ml_gate_terms(gc, cum, lane_i[1], lane_f[1], True)
        nb_x = nmb_all[c, 0:1, :]
        mb_x = nmb_all[c, 1:2, :]
        af, qaf = _ml_dir_lhs(s2, qf, bcf, brf, irf, n_x, m_x, tril, m0)
        ab, qab = _ml_dir_lhs(s2, qf, bcb, brb, irb, nb_x, mb_x, triu, m0)
        vbd = _blockdiag_rows(vc, LANES)
        lhs = jnp.concatenate([af, qaf, ab, qab], axis=1)
        rhs = jnp.concatenate([vbd, cst.astype(BF16), vbd, cb_all[c]], axis=0)
        y = _dot(lhs, rhs)
        y = jnp.concatenate([_head_norm(y[:, :LANES], GN_EPS), _head_norm(y[:, LANES:], GN_EPS)], axis=1)
        y = jax.nn.sigmoid(o_ref[pl.ds(r0, C), :].astype(F32)) * (y * nw_ref[...])
        y_ref[pl.ds(r0, C), :] = y.astype(y_ref.dtype)
        return _ml_update(kc, vc, bcf, icf, cst, n_x, m_x, False, bd, row0)

    cf_fin = lax.fori_loop(0, n, fstep, init(0))

    if want_state:
        nmn_ref[...] = jnp.zeros((8, LANES), F32)
        for d, (cst, n_x, m_x) in ((0, cf_fin), (1, cb_fin)):
            sn_ref[d, 0] = cst[:ML_DK, :ML_DV]
            sn_ref[d, 1] = cst[ML_DK:, ML_DV:]
            nmn_ref[d:d + 1, :] = n_x
            nmn_ref[2 + d:3 + d, :] = m_x


def _mlstm(p16, p32, gbias, nw, s0, nm0, n_batch, seq, want_state):
    has_s0 = s0 is not None
    n_tok = n_batch * seq
    in_specs = [pl.BlockSpec((seq, LANES), lambda b, p: (b, M_MQ // LANES + p)),
                pl.BlockSpec((seq, LANES), lambda b, p: (b, M_MK // LANES + p)),
                pl.BlockSpec((seq, 2 * LANES), lambda b, p: (b, M_MV // 256 + p)),
                pl.BlockSpec((seq, 2 * LANES), lambda b, p: (b, M_MO // 256 + p)),
                pl.BlockSpec((seq, LANES), lambda b, p: (b, S_DT // LANES)),
                pl.BlockSpec((1, LANES), lambda b, p: (0, 0)),
                pl.BlockSpec((1, 2 * LANES), lambda b, p: (0, p))]
    args = [p16, p16, p16, p16, p32, gbias, nw]
    st_spec = pl.BlockSpec((None, 2, 2, ML_DK, ML_DV), lambda b, p: (b, 0, p, 0, 0))
    nm_spec = pl.BlockSpec((None, None, 8, LANES), lambda b, p: (b, p, 0, 0))
    if has_s0:
        in_specs += [st_spec, nm_spec]
        args += [s0, nm0]
    out_specs = [pl.BlockSpec((seq, 2 * LANES), lambda b, p: (b, p))]
    out_shape = [jax.ShapeDtypeStruct((n_tok, MIX_W), BF16)]
    if want_state:
        out_specs += [st_spec, nm_spec]
        out_shape += [jax.ShapeDtypeStruct((n_batch, 2, ML_H, ML_DK, ML_DV), F32),
                      jax.ShapeDtypeStruct((n_batch, ML_H // 2, 8, LANES), F32)]
    res = pl.pallas_call(
        functools.partial(_ml_body, has_s0=has_s0, want_state=want_state, seq=seq),
        grid=(n_batch, ML_H // 2),
        in_specs=in_specs, out_specs=out_specs, out_shape=out_shape,
        scratch_shapes=[pltpu.VMEM((seq // CHUNK, LANES, 2 * LANES), BF16),
                        pltpu.VMEM((seq // CHUNK, 8, LANES), F32)],
        compiler_params=_params(("parallel", "parallel")),
        name="mlstm",
    )(*args)
    return res if want_state else (res[0], None, None)


def _ml_pack_nm(st_n, st_m):
    B = st_n.shape[0]
    n = st_n.reshape(B, 2, ML_H // 2, 2 * ML_DK).transpose(0, 2, 1, 3)
    m = jnp.repeat(st_m.reshape(B, 2, ML_H // 2, 2), ML_DK, axis=-1).transpose(0, 2, 1, 3)
    return jnp.concatenate([n, m, jnp.zeros((B, ML_H // 2, 4, LANES), F32)], axis=2)


def _ml_unpack_nm(nm):
    B = nm.shape[0]
    n = nm[:, :, 0:2, :].transpose(0, 2, 1, 3).reshape(B, 2, ML_H, ML_DK)
    m = nm[:, :, 2:4, :].transpose(0, 2, 1, 3).reshape(B, 2, ML_H, ML_DK)[..., 0]
    return n, m


def _conv3_rows(ref, r0, rb, seq, w, bias):
    x = ref[pl.ds(r0, rb), :].astype(F32)
    prev = ref[pl.ds(pl.multiple_of(jnp.maximum(r0 - 16, 0), 16), 16), :].astype(F32)[15:16]
    nxt = ref[pl.ds(pl.multiple_of(jnp.minimum(r0 + rb, seq - 16), 16), 16), :].astype(F32)[0:1]
    prev = jnp.where(r0 == 0, 0.0, prev)
    nxt = jnp.where(r0 + rb == seq, 0.0, nxt)
    row = _iota((rb, 1), 0)
    x_dn = jnp.where(row == 0, prev, pltpu.roll(x, 1, 0))
    x_up = jnp.where(row == rb - 1, nxt, pltpu.roll(x, rb - 1, 0))
    y = w[0:1] * x_dn + w[1:2] * x + w[2:3] * x_up
    return y if bias is None else y + bias


def _ssd_body(*refs, has_s0, want_state, seq):
    it = iter(refs)
    (x_ref, b_ref, c_ref, z_ref, dt_ref, wx_ref, wb_ref, wc_ref, bx_ref, bb_ref, bc_ref,
     alog_ref, dtb_ref, dd_ref) = (next(it) for _ in range(14))
    s0_ref = sn_ref = None
    if has_s0:
        s0_ref = next(it)
    y_ref = next(it)
    if want_state:
        sn_ref = next(it)
    xs, bs, cs_, dts, sb_all = (next(it) for _ in range(5))

    C = CHUNK
    n = seq // C
    p = pl.program_id(1)
    tril, triu, _ = _tri_masks()
    cum = _cumsum_mats()
    a_lane = -jnp.exp(alog_ref[...])
    dsum = dd_ref[0:1, :] + dd_ref[1:2, :]
    lane_d = [2 * p, 8 + 2 * p]

    rb = min(seq, 256)

    def prep(i, carry):
        r0 = pl.multiple_of(i * rb, rb)
        xs[pl.ds(r0, rb), :] = _silu(_conv3_rows(x_ref, r0, rb, seq, wx_ref[...], bx_ref[...]))
        bs[pl.ds(r0, rb), :] = _silu(_conv3_rows(b_ref, r0, rb, seq, wb_ref[...], bb_ref[...])).astype(BF16)
        cs_[pl.ds(r0, rb), :] = _silu(_conv3_rows(c_ref, r0, rb, seq, wc_ref[...], bc_ref[...])).astype(BF16)
        dts[pl.ds(r0, rb), :] = _softplus(dt_ref[pl.ds(r0, rb), :] + dtb_ref[...])
        return carry

    lax.fori_loop(0, seq // rb, prep, 0)

    def dir_terms(dtc, d, rows):
        cs2 = _dot(cum, dtc * a_lane, precision=HI)
        cs = cs2[C:] if d == 1 else cs2[:C]
        ccol = _lane_expand(cs, lane_d[d])
        dcol = _lane_expand(dtc, lane_d[d])
        if not rows:
            return ccol, dcol, None, None
        return ccol, dcol, _lane_rows(cs, lane_d[d]), _lane_rows(dtc, lane_d[d])

    def update(s, bc, xc, ccol, dcol, d):
        last = ccol[0:1] if d == 1 else ccol[C - 1:C]
        vdt = (xc * dcol * jnp.exp(last - ccol)).astype(BF16)
        return jnp.exp(last) * s + _tdot(bc, vdt)

    def state0(d):
        if not has_s0:
            return jnp.zeros((LANES, LANES), F32)
        return jnp.concatenate([s0_ref[d], jnp.zeros((SSD_N, LANES), F32)], axis=0)

    def bstep(i, sb):
        c = n - 1 - i
        r0 = pl.multiple_of(c * C, C)
        sb_all[c] = sb[:SSD_N].astype(BF16)
        ccol, dcol, _, _ = dir_terms(dts[pl.ds(r0, C), :], 1, False)
        return update(sb, bs[pl.ds(r0, C), :], xs[pl.ds(r0, C), :], ccol, dcol, 1)

    sb_fin = lax.fori_loop(0, n, bstep, state0(1))

    def fstep(c, sf):
        r0 = pl.multiple_of(c * C, C)
        xc = xs[pl.ds(r0, C), :]
        bc = bs[pl.ds(r0, C), :]
        cc = cs_[pl.ds(r0, C), :]
        dtc = dts[pl.ds(r0, C), :]
        ccf, dcf, crf, drf = dir_terms(dtc, 0, True)
        ccb, dcb, crb, drb = dir_terms(dtc, 1, True)
        m = (jnp.where(tril, jnp.exp(jnp.minimum(ccf - crf, 0.0)) * drf, 0.0)
             + jnp.where(triu, jnp.exp(jnp.minimum(ccb - crb, 0.0)) * drb, 0.0))
        s2 = _ntdot(cc, jnp.concatenate([bc, bc], axis=0))
        att = (s2 * m).astype(BF16)
        sbc = jnp.concatenate([sb_all[c], jnp.zeros((SSD_N, LANES), BF16)], axis=0)
        y = (_dot(att, _blockdiag_rows(xc, 64).astype(BF16))
             + _dot(cc, sf.astype(BF16)) * jnp.exp(ccf)
             + _dot(cc, sbc) * jnp.exp(ccb)
             + dsum * xc)
        y_ref[pl.ds(r0, C), :] = (y * _silu(z_ref[pl.ds(r0, C), :].astype(F32))).astype(y_ref.dtype)
        return update(sf, bc, xc, ccf, dcf, 0)

    sf_fin = lax.fori_loop(0, n, fstep, state0(0))
    if want_state:
        sn_ref[0] = sf_fin[:SSD_N]
        sn_ref[1] = sb_fin[:SSD_N]


def _ssd(p16, p32, conv_w, conv_b, alog, dtb, dd, s0, n_batch, seq, want_state):
    has_s0 = s0 is not None
    n_tok = n_batch * seq
    col = lambda off: (lambda b, p: (b, off // LANES + p))
    grp = lambda off: (lambda b, p: (b, off // LANES + p // 2))
    wcol = lambda off: (lambda b, p: (0, off + p))
    wgrp = lambda off: (lambda b, p: (0, off + p // 2))
    in_specs = [pl.BlockSpec((seq, LANES), col(M_SX)), pl.BlockSpec((seq, LANES), grp(M_SB)),
                pl.BlockSpec((seq, LANES), grp(M_SC)), pl.BlockSpec((seq, LANES), col(M_SZ)),
                pl.BlockSpec((seq, LANES), lambda b, p: (b, S_DT // LANES)),
                pl.BlockSpec((3, LANES), wcol(0)), pl.BlockSpec((3, LANES), wgrp(4)), pl.BlockSpec((3, LANES), wgrp(6)),
                pl.BlockSpec((1, LANES), wcol(0)), pl.BlockSpec((1, LANES), wgrp(4)), pl.BlockSpec((1, LANES), wgrp(6)),
                pl.BlockSpec((1, LANES), lambda b, p: (0, 0)), pl.BlockSpec((1, LANES), lambda b, p: (0, 0)),
                pl.BlockSpec((2, LANES), wcol(0))]
    args = [p16, p16, p16, p16, p32, conv_w, conv_w, conv_w, conv_b, conv_b, conv_b, alog, dtb, dd]
    st_spec = pl.BlockSpec((None, 2, None, SSD_N, LANES), lambda b, p: (b, 0, p, 0, 0))
    if has_s0:
        in_specs.append(st_spec)
        args.append(s0)
    out_specs = [pl.BlockSpec((seq, LANES), lambda b, p: (b, p))]
    out_shape = [jax.ShapeDtypeStruct((n_tok, MIX_W), BF16)]
    if want_state:
        out_specs.append(st_spec)
        out_shape.append(jax.ShapeDtypeStruct((n_batch, 2, SSD_H // 2, SSD_N, LANES), F32))
    res = pl.pallas_call(
        functools.partial(_ssd_body, has_s0=has_s0, want_state=want_state, seq=seq),
        grid=(n_batch, SSD_H // 2),
        in_specs=in_specs, out_specs=out_specs, out_shape=out_shape,
        scratch_shapes=[pltpu.VMEM((seq, LANES), F32), pltpu.VMEM((seq, LANES), BF16), pltpu.VMEM((seq, LANES), BF16),
                        pltpu.VMEM((seq, LANES), F32), pltpu.VMEM((seq // CHUNK, SSD_N, LANES), BF16)],
        compiler_params=_params(("parallel", "parallel")),
        name="ssd",
    )(*args)
    return res if want_state else (res[0], None)


def _ssd_pack_state(st):
    B = st.shape[0]
    return st.reshape(B, 2, SSD_H // 2, 2, SSD_N, SSD_P).transpose(0, 1, 2, 4, 3, 5).reshape(B, 2, SSD_H // 2, SSD_N, 2 * SSD_P)


def _ssd_unpack_state(st):
    B = st.shape[0]
    return st.reshape(B, 2, SSD_H // 2, SSD_N, 2, SSD_P).transpose(0, 1, 2, 4, 3, 5).reshape(B, 2, SSD_H, SSD_N, SSD_P)


def _seg_ones():
    r = _iota((MIX_W, MIX_W), 0) // RWKV_D
    c = _iota((MIX_W, MIX_W), 1) // RWKV_D
    return jnp.where(r == c, 1.0, 0.0).astype(BF16)


def _seg_sum(x, ones):
    hi = x.astype(BF16)
    lo = (x - hi.astype(F32)).astype(BF16)
    return _dot(hi, ones) + _dot(lo, ones)


def _rwkv_prep_body(x_ref, xp_ref, xn_ref, sm_ref, cw_ref, w0_ref, wup_ref, a0_ref, aup_ref,
                    kk_ref, ka_ref, rk_ref, r_o, v_o, a_o, w_o, kd_o, b_o, bon_o, *, rb):
    j = pl.program_id(1)
    nj = pl.num_programs(1)
    prev = jnp.where(j == 0, 0.0, xp_ref[...].astype(F32)[15:16])
    nxt = jnp.where(j == nj - 1, 0.0, xn_ref[...].astype(F32)[0:1])
    row = _iota((rb, 1), 0)
    cw = cw_ref[...]

    def conv(c0):
        x = x_ref[:, c0:c0 + MIX_W].astype(F32)
        x_dn = jnp.where(row == 0, prev[:, c0:c0 + MIX_W], pltpu.roll(x, 1, 0))
        x_up = jnp.where(row == rb - 1, nxt[:, c0:c0 + MIX_W], pltpu.roll(x, rb - 1, 0))
        w = cw[:, c0:c0 + MIX_W]
        return w[0:1] * x_dn + w[1:2] * x + w[2:3] * x_up

    r_ = conv(0)
    k_ = conv(MIX_W)
    v_ = conv(2 * MIX_W)
    ones = _seg_ones()
    kk = k_ * kk_ref[...]
    nrm = jnp.sqrt(_seg_sum(kk * kk, ones))
    kk = kk / jnp.maximum(nrm, 1e-12)
    lora = sm_ref[...]
    w_low = jnp.tanh(lora).astype(BF16)
    a_low = lora.astype(BF16)
    kd_sum = jnp.zeros_like(k_)
    for d in range(2):
        w_log = -_softplus(-(w0_ref[d:d + 1, :] + _dot(w_low, wup_ref[d]))) - 0.5
        w_o[d] = jnp.exp(-jnp.exp(w_log))
        iclr = jax.nn.sigmoid(a0_ref[d:d + 1, :] + _dot(a_low, aup_ref[d]))
        kd = k_ * (1.0 + (iclr - 1.0) * ka_ref[...])
        kd_o[d] = kd
        b_o[d] = kk * iclr
        kd_sum = kd_sum + kd
    r_o[...] = r_
    v_o[...] = v_
    a_o[...] = -kk
    bon_o[...] = _seg_sum(r_ * kd_sum * rk_ref[...], ones) * v_


def _rwkv_prep(p16, p32, conv_w, w0, wup, a0, aup, k_k, k_a, r_k, n_batch, seq):
    n_tok = n_batch * seq
    rb = 256
    nj = seq // rb
    nb16 = seq // 16
    one = lambda b, j: (0, 0)
    tok = jax.ShapeDtypeStruct((n_tok, MIX_W), F32)
    tok2 = jax.ShapeDtypeStruct((2, n_tok, MIX_W), F32)
    o1 = pl.BlockSpec((rb, MIX_W), lambda b, j: (b * nj + j, 0))
    o2 = pl.BlockSpec((2, rb, MIX_W), lambda b, j: (0, b * nj + j, 0))
    return pl.pallas_call(
        functools.partial(_rwkv_prep_body, rb=rb),
        grid=(n_batch, nj),
        in_specs=[pl.BlockSpec((rb, 3 * MIX_W), lambda b, j: (b * nj + j, M_RWKV // (3 * MIX_W))),
                  pl.BlockSpec((16, 3 * MIX_W),
                               lambda b, j: (b * nb16 + jnp.maximum(j * (rb // 16) - 1, 0), M_RWKV // (3 * MIX_W))),
                  pl.BlockSpec((16, 3 * MIX_W),
                               lambda b, j: (b * nb16 + jnp.minimum((j + 1) * (rb // 16), nb16 - 1), M_RWKV // (3 * MIX_W))),
                  pl.BlockSpec((rb, LANES), lambda b, j: (b * nj + j, S_WWD // LANES)),
                  pl.BlockSpec((3, 3 * MIX_W), one),
                  pl.BlockSpec((2, MIX_W), one), pl.BlockSpec((2, LANES, MIX_W), lambda b, j: (0, 0, 0)),
                  pl.BlockSpec((2, MIX_W), one), pl.BlockSpec((2, LANES, MIX_W), lambda b, j: (0, 0, 0)),
                  pl.BlockSpec((1, MIX_W), one), pl.BlockSpec((1, MIX_W), one), pl.BlockSpec((1, MIX_W), one)],
        out_specs=[o1, o1, o1, o2, o2, o2, o1],
        out_shape=[tok, tok, tok, tok2, tok2, tok2, tok],
        compiler_params=_params(("parallel", "parallel")),
        name="rwkv_prep",
    )(p16, p16, p16, p32, conv_w, w0, wup, a0, aup, k_k, k_a, r_k)


def _rwkv_scan_body(*refs, has_s0, want_state, tb):
    it = iter(refs)
    r_ref, w_ref, k_ref, v_ref, a_ref, b_ref = (next(it) for _ in range(6))
    s0_ref = sn_ref = None
    if has_s0:
        s0_ref = next(it)
    y_ref = next(it)
    if want_state:
        sn_ref = next(it)
    st = next(it)
    i = pl.program_id(1)

    @pl.when(i == 0)
    def _():
        st[...] = s0_ref[...] if has_s0 else jnp.zeros(st.shape, F32)

    def step(t, carry):
        sa = jnp.zeros((RWKV_D, LANES), F32)
        for kx in range(RWKV_D):
            sa = sa + st[kx] * a_ref[t, pl.ds(kx, 1), :]
        vt = v_ref[t]
        y = jnp.zeros((RWKV_D, LANES), F32)
        for kx in range(RWKV_D):
            row = pl.ds(kx, 1)
            sn = st[kx] * w_ref[t, row, :] + sa * b_ref[t, row, :] + vt * k_ref[t, row, :]
            st[kx] = sn
            y = y + sn * r_ref[t, row, :]
        y_ref[t] = y
        return carry

    lax.fori_loop(0, tb, step, 0)

    if want_state:
        @pl.when(i == pl.num_programs(1) - 1)
        def _():
            sn_ref[...] = st[...]


def _rwkv_scan(r, w, k, v, a, b, s0, want_state):
    seq, _, n_lane = r.shape
    tb = 32
    has_s0 = s0 is not None
    blk = pl.BlockSpec((tb, RWKV_D, LANES), lambda g, i: (i, 0, g))
    sblk = pl.BlockSpec((RWKV_D, RWKV_D, LANES), lambda g, i: (0, 0, g))
    in_specs = [blk] * 6
    args = [r, w, k, v, a, b]
    if has_s0:
        in_specs.append(sblk)
        args.append(s0)
    out_specs = [blk]
    out_shape = [jax.ShapeDtypeStruct((seq, RWKV_D, n_lane), F32)]
    if want_state:
        out_specs.append(sblk)
        out_shape.append(jax.ShapeDtypeStruct((RWKV_D, RWKV_D, n_lane), F32))
    res = pl.pallas_call(
        functools.partial(_rwkv_scan_body, has_s0=has_s0, want_state=want_state, tb=tb),
        grid=(n_lane // LANES, seq // tb),
        in_specs=in_specs, out_specs=out_specs, out_shape=out_shape,
        scratch_shapes=[pltpu.VMEM((RWKV_D, RWKV_D, LANES), F32)],
        compiler_params=_params(("parallel", "arbitrary")),
        name="rwkv_scan",
    )(*args)
    return res if want_state else (res[0], None)


def _rwkv_post_body(yf_ref, yb_ref, bon_ref, g_ref, gup_ref, lw_ref, lb_ref, o_ref):
    ones = _seg_ones()
    y = yf_ref[...] + yb_ref[...]
    mu = _seg_sum(y, ones) * (1.0 / RWKV_D)
    d = y - mu
    var = _seg_sum(d * d, ones) * (1.0 / RWKV_D)
    out = d * lax.rsqrt(var + RWKV_GN_EPS) * lw_ref[...] + lb_ref[...] + bon_ref[...]
    g = _dot(jax.nn.sigmoid(g_ref[...]).astype(BF16), gup_ref[...])
    o_ref[...] = (out * g).astype(o_ref.dtype)


def _rwkv_post(yf, yb, bonus, p32, g_up, ln_w, ln_b, tm):
    n_tok = yf.shape[0]
    tokb = pl.BlockSpec((tm, MIX_W), lambda i: (i, 0))
    one = lambda i: (0, 0)
    return pl.pallas_call(
        _rwkv_post_body,
        grid=(n_tok // tm,),
        in_specs=[tokb, tokb, tokb, pl.BlockSpec((tm, LANES), lambda i: (i, S_WGD // LANES)),
                  pl.BlockSpec((LANES, MIX_W), one), pl.BlockSpec((1, MIX_W), one), pl.BlockSpec((1, MIX_W), one)],
        out_specs=tokb,
        out_shape=jax.ShapeDtypeStruct((n_tok, MIX_W), BF16),
        compiler_params=_params(("parallel",)),
        name="rwkv_post",
    )(yf, yb, bonus, p32, g_up, ln_w, ln_b)


def _to_scan_layout(x, n_batch, seq):
    return x.reshape(n_batch, seq, RWKV_H, RWKV_D).transpose(1, 3, 0, 2).reshape(seq, RWKV_D, n_batch * RWKV_H)


def _from_scan_layout(y, n_batch, seq):
    return y.reshape(seq, RWKV_D, n_batch, RWKV_H).transpose(2, 0, 3, 1).reshape(n_batch * seq, MIX_W)


def _rwkv(p16, p32, lw, s0, n_batch, seq, want_state, tm):
    r_, v_, a_, w_, kd_, b_, bonus = _rwkv_prep(p16, p32, lw["conv_w"], lw["w0"], lw["w_up"], lw["a0"], lw["a_up"],
                                                lw["k_k"], lw["k_a"], lw["r_k"], n_batch, seq)
    tl = lambda x: _to_scan_layout(x, n_batch, seq)
    both = lambda x: jnp.concatenate([tl(x), tl(x)[::-1]], axis=-1)
    per_dir = lambda x: jnp.concatenate([tl(x[0]), tl(x[1])[::-1]], axis=-1)
    s0t = None
    if s0 is not None:
        s0t = s0.transpose(4, 3, 1, 0, 2).reshape(RWKV_D, RWKV_D, 2 * n_batch * RWKV_H)
    y, sn = _rwkv_scan(both(r_), per_dir(w_), per_dir(kd_), both(v_), both(a_), per_dir(b_), s0t, want_state)
    nl = n_batch * RWKV_H
    yf = _from_scan_layout(y[:, :, :nl], n_batch, seq)
    yb = _from_scan_layout(y[::-1, :, nl:], n_batch, seq)
    out = _rwkv_post(yf, yb, bonus, p32, lw["g_up"], lw["ln_w"], lw["ln_b"], tm)
    if want_state:
        sn = sn.reshape(RWKV_D, RWKV_D, 2, n_batch, RWKV_H).transpose(3, 2, 4, 1, 0)
    return out, sn


def _merge_body(yr_ref, ys_ref, yw_ref, ym_ref, g_ref, wb_ref, wo_ref, x_ref, mod_ref, sn_ref, o_ref):
    ys = ys_ref[...].astype(F32)
    ys = ys * lax.rsqrt(jnp.mean(ys * ys, axis=-1, keepdims=True) + EPS) * sn_ref[...]
    branches = (yr_ref[...], ys.astype(BF16), yw_ref[...], ym_ref[...])
    merged = None
    for i, br in enumerate(branches):
        gate = jax.nn.sigmoid(g_ref[:, i * D_MODEL:(i + 1) * D_MODEL].astype(F32))
        term = gate * _dot(br, wb_ref[i])
        merged = term if merged is None else merged + term
    out = _dot(merged.astype(BF16), wo_ref[...])
    o_ref[...] = x_ref[...] + mod_ref[2:3, :] * out


def _merge(y_ret, y_ssd, y_rw, y_ml, p16, w_branch, w_out, x, mod, ssd_norm, tm, seq):
    n_tok = x.shape[0]
    mi = _mod_index(mod.shape[0], tm, seq)
    yb = pl.BlockSpec((tm, MIX_W), lambda i: (i, 0))
    xb = pl.BlockSpec((tm, D_MODEL), lambda i: (i, 0))
    return pl.pallas_call(
        _merge_body,
        grid=(n_tok // tm,),
        in_specs=[yb, yb, yb, yb,
                  pl.BlockSpec((tm, N_BRANCH * D_MODEL), lambda i: (i, M_GATE)),
                  pl.BlockSpec((N_BRANCH, MIX_W, D_MODEL), lambda i: (0, 0, 0)),
                  pl.BlockSpec((D_MODEL, D_MODEL), lambda i: (0, 0)),
                  xb,
                  pl.BlockSpec((None, 8, D_MODEL), lambda i: (mi(i), 0, 0)),
                  pl.BlockSpec((1, MIX_W), lambda i: (0, 0))],
        out_specs=xb,
        out_shape=jax.ShapeDtypeStruct((n_tok, D_MODEL), F32),
        compiler_params=_params(("parallel",)),
        name="merge",
    )(y_ret, y_ssd, y_rw, y_ml, p16, w_branch, w_out, x, mod, ssd_norm)


def _ffn_body(x_ref, xp_ref, xn_ref, mod_ref, nw_ref, uv_ref, ug_ref, cwv_ref, cwg_ref, cbv_ref, cbg_ref,
              dn_ref, fw_ref, o_ref, h_ref, *, tm, seq, final):
    i = pl.program_id(0)
    f = pl.program_id(1)
    shift, scale, gate = mod_ref[3:4, :], mod_ref[4:5, :], mod_ref[5:6, :]

    @pl.when(f == 0)
    def _():
        nw = nw_ref[...]
        h_ref[0:tm, :] = _rms_mod(x_ref[...], nw, shift, scale).astype(BF16)
        h_ref[tm:tm + 8, :] = _rms_mod(xp_ref[...], nw, shift, scale).astype(BF16)
        h_ref[tm + 8:tm + 16, :] = _rms_mod(xn_ref[...], nw, shift, scale).astype(BF16)
        o_ref[...] = jnp.zeros(o_ref.shape, F32)

    row = _iota((tm, 1), 0)
    pos = (i * tm + row) % seq
    first, last = pos == 0, pos == seq - 1
    h = h_ref[...]

    def conv(u_ref, cw_ref, cb_ref):
        u = _dot(h, u_ref[...])
        um = u[0:tm]
        u_dn = jnp.where(row == 0, u[tm + 7:tm + 8], pltpu.roll(um, 1, 0))
        u_up = jnp.where(row == tm - 1, u[tm + 8:tm + 9], pltpu.roll(um, tm - 1, 0))
        u_dn = jnp.where(first, 0.0, u_dn)
        u_up = jnp.where(last, 0.0, u_up)
        cw = cw_ref[...]
        return cw[0:1] * u_dn + cw[1:2] * um + cw[2:3] * u_up + cb_ref[...]

    act = conv(uv_ref, cwv_ref, cbv_ref) * _silu(conv(ug_ref, cwg_ref, cbg_ref))
    o_ref[...] += _dot(act.astype(BF16), dn_ref[...])

    @pl.when(f == pl.num_programs(1) - 1)
    def _():
        xn = x_ref[...] + gate * o_ref[...]
        if final:
            xn = xn * lax.rsqrt(jnp.mean(xn * xn, axis=-1, keepdims=True) + EPS) * fw_ref[...]
        o_ref[...] = xn


def _ffn(x, mod, nw, up, conv_w, conv_b, down, final_w, tm, seq, final):
    n_tok = x.shape[0]
    fc = FFN_DIM // 2
    nf = FFN_DIM // fc
    mi = _mod_index(mod.shape[0], tm, seq)
    n8 = n_tok // 8
    xb = pl.BlockSpec((tm, D_MODEL), lambda i, f: (i, 0))
    one = lambda i, f: (0, 0)
    return pl.pallas_call(
        functools.partial(_ffn_body, tm=tm, seq=seq, final=final),
        grid=(n_tok // tm, nf),
        in_specs=[xb,
                  pl.BlockSpec((8, D_MODEL), lambda i, f: (jnp.maximum(i * (tm // 8) - 1, 0), 0)),
                  pl.BlockSpec((8, D_MODEL), lambda i, f: (jnp.minimum((i + 1) * (tm // 8), n8 - 1), 0)),
                  pl.BlockSpec((None, 8, D_MODEL), lambda i, f: (mi(i), 0, 0)),
                  pl.BlockSpec((1, D_MODEL), one),
                  pl.BlockSpec((D_MODEL, fc), lambda i, f: (0, f)),
                  pl.BlockSpec((D_MODEL, fc), lambda i, f: (0, nf + f)),
                  pl.BlockSpec((3, fc), lambda i, f: (0, f)),
                  pl.BlockSpec((3, fc), lambda i, f: (0, nf + f)),
                  pl.BlockSpec((1, fc), lambda i, f: (0, f)),
                  pl.BlockSpec((1, fc), lambda i, f: (0, nf + f)),
                  pl.BlockSpec((fc, D_MODEL), lambda i, f: (f, 0)),
                  pl.BlockSpec((1, D_MODEL), one)],
        out_specs=xb,
        out_shape=jax.ShapeDtypeStruct((n_tok, D_MODEL), F32),
        scratch_shapes=[pltpu.VMEM((tm + 16, D_MODEL), BF16)],
        compiler_params=_params(("parallel", "arbitrary")),
        name="conv_ffn",
    )(x, x, x, mod, nw, up, up, conv_w, conv_w, conv_b, conv_b, down, final_w)


def _rope_tables(seq):
    rows = seq // GRID_W
    rr, cc = jnp.meshgrid(jnp.arange(rows), jnp.arange(GRID_W), indexing='ij')
    nf = RET_DK // 4
    inv = ROPE_BASE ** (-jnp.arange(nf, dtype=F32) / nf)
    ang = jnp.concatenate([rr.reshape(-1, 1) * inv, cc.reshape(-1, 1) * inv], axis=-1)
    cos, sin = jnp.cos(ang), jnp.sin(ang)
    return (jnp.tile(jnp.concatenate([cos, cos], axis=-1), (1, 2)),
            jnp.tile(jnp.concatenate([-sin, sin], axis=-1), (1, 2)))


def _lanes16(a):
    return jnp.zeros((1, LANES), F32).at[0, :16].set(a.reshape(-1).astype(F32))


def _pad_rows(w, lo):
    return jnp.zeros((2, LANES, MIX_W), F32).at[:, lo:lo + w.shape[1]].set(w).astype(BF16)


def _layer(x, mod, lw, states, rope, n_batch, seq, want_state, final, final_w):
    tm_p = 1024
    tm = 512
    p16 = _in_proj(x, mod, lw["norm1"], lw["w_main"], BF16, tm_p, 2048, seq, "in_proj_main")
    p32 = _in_proj(x, mod, lw["norm1"], lw["w_small"], F32, tm_p, N_SMALL, seq, "in_proj_small")
    st_ret, st_ssd, st_rwkv, st_c, st_nm = states
    cos, sin = rope if rope is not None else (None, None)
    y_ret, n_ret = _retention(lw["ret_lg"], p16, None, cos, sin, st_ret, n_batch, seq, want_state)
    y_ssd, n_ssd = _ssd(p16, p32, lw["ssd_conv_w"], lw["ssd_conv_b"], lw["ssd_alog"], lw["ssd_dtb"], lw["ssd_dd"],
                        st_ssd, n_batch, seq, want_state)
    y_rw, n_rwkv = _rwkv(p16, p32, lw["rwkv"], st_rwkv, n_batch, seq, want_state, tm)
    y_ml, n_c, n_nm = _mlstm(p16, p32, lw["ml_gbias"], lw["ml_norm"], st_c, st_nm, n_batch, seq, want_state)
    x = _merge(y_ret, y_ssd, y_rw, y_ml, p16, lw["w_branch"], lw["w_out"], x, mod, lw["ssd_norm"], tm, seq)
    x = _ffn(x, mod, lw["norm2"], lw["ffn_up"], lw["ffn_conv_w"], lw["ffn_conv_b"], lw["ffn_down"], final_w,
             tm, seq, final)
    return x, (n_ret, n_ssd, n_rwkv, n_c, n_nm)


def kernel(x_prompt, x_sample, state_ret, state_ssd, state_rwkv, state_mlstm_c, state_mlstm_n, state_mlstm_m, c, c_ctx, ada_w, ada_b, norm1, norm2, w_in, ret_log_rate, ssd_conv_w, ssd_conv_b, ssd_A_log, ssd_dt_bias, ssd_D, ssd_norm, rwkv_conv_w, rwkv_w0, rwkv_w_up, rwkv_a0, rwkv_a_up, rwkv_g_up, rwkv_k_k, rwkv_k_a, rwkv_r_k, rwkv_ln_w, rwkv_ln_b, ml_i_bias, ml_f_bias, ml_norm, w_branch, w_out, ffn_up, ffn_conv_w, ffn_conv_b, ffn_down, final_norm):
    nb_c, seq_c, _ = x_prompt.shape
    nb_l, seq_l, _ = x_sample.shape
    assert seq_c % 256 == 0 and seq_l % 256 == 0 and nb_l * RWKV_H * 2 % LANES == 0

    cvec = jnp.zeros((16, D_MODEL), F32).at[0].set(c_ctx).at[1:1 + nb_l].set(c)
    mod_all = _modulation(cvec, ada_w, ada_b).reshape(DEPTH, 16, 6, D_MODEL)
    mod_all = jnp.pad(mod_all, ((0, 0), (0, 0), (0, 2), (0, 0)))
    rope = _rope_tables(seq_l)
    main_perm, small_perm, conv_perm = _main_perm(), _small_perm(), _ssd_conv_perm()
    row = lambda a: a.reshape(1, -1).astype(F32)
    final_w = row(final_norm)

    xp = x_prompt.reshape(nb_c * seq_c, D_MODEL)
    xs = x_sample.reshape(nb_l * seq_l, D_MODEL)
    new_states = []
    for l in range(DEPTH):
        lw = dict(
            norm1=row(norm1[l]), norm2=row(norm2[l]),
            w_main=_take_cols(w_in[l], main_perm).astype(BF16),
            w_small=_take_cols(w_in[l], small_perm).astype(BF16),
            ret_lg=-jnp.exp(ret_log_rate[l].astype(F32)),
            ssd_conv_w=_take_cols(ssd_conv_w[l], conv_perm), ssd_conv_b=_take_cols(row(ssd_conv_b[l]), conv_perm),
            ssd_alog=_lanes16(ssd_A_log[l]), ssd_dtb=_lanes16(ssd_dt_bias[l]),
            ssd_dd=jnp.repeat(ssd_D[l].astype(F32), SSD_P, axis=1), ssd_norm=row(ssd_norm[l]),
            rwkv=dict(conv_w=rwkv_conv_w[l], w0=rwkv_w0[l], w_up=_pad_rows(rwkv_w_up[l], 0),
                      a0=rwkv_a0[l], a_up=_pad_rows(rwkv_a_up[l], RWKV_W_LORA),
                      k_k=row(rwkv_k_k[l]), k_a=row(rwkv_k_a[l]), r_k=row(rwkv_r_k[l]),
                      g_up=rwkv_g_up[l].astype(BF16), ln_w=row(rwkv_ln_w[l]), ln_b=row(rwkv_ln_b[l])),
            ml_gbias=jnp.zeros((1, LANES), F32).at[0, 16:32].set(
                jnp.stack([ml_i_bias[l], ml_f_bias[l]], axis=1).reshape(-1)),
            ml_norm=row(ml_norm[l]),
            w_branch=w_branch[l].astype(BF16), w_out=w_out[l].astype(BF16),
            ffn_up=ffn_up[l].astype(BF16), ffn_conv_w=ffn_conv_w[l], ffn_conv_b=row(ffn_conv_b[l]),
            ffn_down=ffn_down[l].astype(BF16),
        )
        final = l == DEPTH - 1
        mod_c = mod_all[l, 0:1]
        mod_l = mod_all[l, 1:1 + nb_l]
        xp, st = _layer(xp, mod_c, lw, (None,) * 5, None, nb_c, seq_c, True, final, final_w)
        new_states.append(st)
        lat_states = (state_ret[:, l], _ssd_pack_state(state_ssd[:, l]), state_rwkv[:, l], state_mlstm_c[:, l],
                      _ml_pack_nm(state_mlstm_n[:, l], state_mlstm_m[:, l]))
        xs, _ = _layer(xs, mod_l, lw, lat_states, rope, nb_l, seq_l, False, final, final_w)

    new_ret = jnp.stack([s[0] for s in new_states], axis=1)
    new_ssd = jnp.stack([_ssd_unpack_state(s[1]) for s in new_states], axis=1)
    new_rwkv = jnp.stack([s[2] for s in new_states], axis=1)
    new_c = jnp.stack([s[3] for s in new_states], axis=1)
    nm = [_ml_unpack_nm(s[4]) for s in new_states]
    new_n = jnp.stack([a for a, _ in nm], axis=1)
    new_m = jnp.stack([b for _, b in nm], axis=1)
    return (xp.reshape(nb_c, seq_c, D_MODEL), xs.reshape(nb_l, seq_l, D_MODEL),
            new_ret, new_ssd, new_rwkv, new_c, new_n, new_m)
```

```python
import functools
import math

import jax
import jax.numpy as jnp
import numpy as np
from jax import lax
from jax.experimental import pallas as pl
from jax.experimental.pallas import tpu as pltpu

F32 = jnp.float32
BF16 = jnp.bfloat16
HI = lax.Precision.HIGHEST

D_MODEL = 1024
DEPTH = 2
GRID_W = 64
CHUNK = 64
N_BRANCH = 4
MIX_W = 512
RET_H, RET_DK, RET_DV = 4, 64, 128
SSD_H, SSD_P, SSD_N, SSD_G = 8, 64, 64, 2
RWKV_H, RWKV_D = 8, 64
RWKV_W_LORA, RWKV_A_LORA, RWKV_G_LORA = 64, 64, 128
ML_H, ML_DK, ML_DV = 4, 64, 128
FFN_DIM = 2816
ROPE_BASE = 10000.0
EPS = 1e-6
GN_EPS = 1e-5
RWKV_GN_EPS = 64e-5

LANES = 128
VMEM_LIMIT = 56 * 1024 * 1024
SWEEP_UNROLL = 1
RWKV_CHUNKS_PER_STEP = 4

_O_RQ, _O_RK, _O_RV, _O_RG = 0, 256, 512, 1024
_O_SZ, _O_SX, _O_SB, _O_SC, _O_SDT = 1536, 2048, 2560, 2688, 2816
_O_WRKV, _O_WWD, _O_WAD, _O_WGD = 2832, 4368, 4432, 4496
_O_MQ, _O_MK, _O_MV, _O_MO, _O_MIF, _O_MG = 4624, 4880, 5136, 5648, 6160, 6176

M_GATE, M_RQ, M_RK, M_RV, M_RG = 0, 4096, 4352, 4608, 5120
M_SZ, M_RWKV = 5632, 6144
M_MQ, M_MK, M_MV, M_MO = 7680, 7936, 8192, 8704
M_SX, M_SB, M_SC = 9216, 9728, 9984
N_MAIN = 10240
S_WGD, S_WWD, S_WAD, S_DT, S_MIF = 0, 128, 192, 256, 272
N_SMALL = 384


def _main_perm():
    idx = np.full((N_MAIN,), -1, np.int64)

    def put(dst, src, n):
        idx[dst:dst + n] = np.arange(src, src + n)

    put(M_GATE, _O_MG, 4096)
    put(M_RQ, _O_RQ, 256); put(M_RK, _O_RK, 256); put(M_RV, _O_RV, 512); put(M_RG, _O_RG, 512)
    put(M_SZ, _O_SZ, 512); put(M_RWKV, _O_WRKV, 1536)
    put(M_MQ, _O_MQ, 256); put(M_MK, _O_MK, 256); put(M_MV, _O_MV, 512); put(M_MO, _O_MO, 512)
    put(M_SX, _O_SX, 512)
    for g in range(SSD_G):
        put(M_SB + g * LANES, _O_SB + g * SSD_N, SSD_N)
        put(M_SC + g * LANES, _O_SC + g * SSD_N, SSD_N)
    return idx


def _small_perm():
    idx = np.full((N_SMALL,), -1, np.int64)
    idx[S_WGD:S_WGD + 128] = np.arange(_O_WGD, _O_WGD + 128)
    idx[S_WWD:S_WWD + 64] = np.arange(_O_WWD, _O_WWD + 64)
    idx[S_WAD:S_WAD + 64] = np.arange(_O_WAD, _O_WAD + 64)
    idx[S_DT:S_DT + 16] = np.arange(_O_SDT, _O_SDT + 16)
    idx[S_MIF:S_MIF + 16] = np.arange(_O_MIF, _O_MIF + 16)
    return idx


def _ssd_conv_perm():
    idx = np.full((1024,), -1, np.int64)
    idx[0:512] = np.arange(0, 512)
    for g in range(SSD_G):
        idx[512 + g * LANES:512 + g * LANES + SSD_N] = np.arange(512 + g * SSD_N, 512 + (g + 1) * SSD_N)
        idx[768 + g * LANES:768 + g * LANES + SSD_N] = np.arange(640 + g * SSD_N, 640 + (g + 1) * SSD_N)
    return idx


def _take_cols(a, idx):
    safe = np.where(idx < 0, 0, idx)
    out = jnp.take(a, jnp.asarray(safe, jnp.int32), axis=-1)
    return jnp.where(jnp.asarray(idx >= 0), out, 0).astype(a.dtype)


def _params(sem, vmem=VMEM_LIMIT):
    return pltpu.CompilerParams(dimension_semantics=sem, vmem_limit_bytes=vmem)


def _tdot(a, b, **kw):
    return lax.dot_general(a, b, (((0,), (0,)), ((), ())), preferred_element_type=F32, **kw)


def _ntdot(a, b, **kw):
    return lax.dot_general(a, b, (((1,), (1,)), ((), ())), preferred_element_type=F32, **kw)


def _dot(a, b, **kw):
    return jnp.dot(a, b, preferred_element_type=F32, **kw)


def _silu(x):
    return x * jax.nn.sigmoid(x)


def _softplus(x):
    return jnp.maximum(x, 0.0) + jnp.log1p(jnp.exp(-jnp.abs(x)))


def _iota(shape, dim):
    return lax.broadcasted_iota(jnp.int32, shape, dim)


def _mod_body(c_ref, w_ref, b_ref, o_ref):
    c = c_ref[...]
    o_ref[...] = _dot(_silu(c).astype(BF16), w_ref[...].astype(BF16)) + b_ref[...]


def _modulation(cvec, ada_w, ada_b):
    L = ada_w.shape[0]
    tn = 512
    return pl.pallas_call(
        _mod_body,
        grid=(L, 6 * D_MODEL // tn),
        in_specs=[pl.BlockSpec((16, D_MODEL), lambda l, j: (0, 0)),
                  pl.BlockSpec((None, D_MODEL, tn), lambda l, j: (l, 0, j)),
                  pl.BlockSpec((None, 1, tn), lambda l, j: (l, 0, j))],
        out_specs=pl.BlockSpec((None, 16, tn), lambda l, j: (l, 0, j)),
        out_shape=jax.ShapeDtypeStruct((L, 16, 6 * D_MODEL), F32),
        compiler_params=_params(("parallel", "parallel")),
        name="modulation",
    )(cvec, ada_w, ada_b.reshape(L, 1, 6 * D_MODEL))


def _rms_mod(x, nw, shift, scale):
    ms = jnp.mean(x * x, axis=-1, keepdims=True)
    return (x * lax.rsqrt(ms + EPS) * nw) * (1.0 + scale) + shift


def _in_proj_body(x_ref, mod_ref, nw_ref, w_ref, o_ref, h_ref):
    @pl.when(pl.program_id(1) == 0)
    def _():
        h = _rms_mod(x_ref[...], nw_ref[...], mod_ref[0:1, :], mod_ref[1:2, :])
        h_ref[...] = h.astype(BF16)

    o_ref[...] = _dot(h_ref[...], w_ref[...]).astype(o_ref.dtype)


def _mod_index(n_mod, tm, seq):
    if n_mod == 1:
        return lambda i: 0
    return lambda i: (i * tm) // seq


def _in_proj(x, mod, nw, w, out_dtype, tm, tn, seq, name):
    n_tok, n = x.shape[0], w.shape[1]
    mi = _mod_index(mod.shape[0], tm, seq)
    return pl.pallas_call(
        _in_proj_body,
        grid=(n_tok // tm, n // tn),
        in_specs=[pl.BlockSpec((tm, D_MODEL), lambda i, j: (i, 0)),
                  pl.BlockSpec((None, 8, D_MODEL), lambda i, j: (mi(i), 0, 0)),
                  pl.BlockSpec((1, D_MODEL), lambda i, j: (0, 0)),
                  pl.BlockSpec((D_MODEL, tn), lambda i, j: (0, j))],
        out_specs=pl.BlockSpec((tm, tn), lambda i, j: (i, j)),
        out_shape=jax.ShapeDtypeStruct((n_tok, n), out_dtype),
        scratch_shapes=[pltpu.VMEM((tm, D_MODEL), BF16)],
        compiler_params=_params(("parallel", "arbitrary")),
        name=name,
    )(x, mod, nw, w)


def _pair_masks():
    lane = _iota((1, LANES), 1)
    return lane < 64, lane


def _head_norm(y, eps):
    mu = jnp.mean(y, axis=-1, keepdims=True)
    d = y - mu
    var = jnp.mean(d * d, axis=-1, keepdims=True)
    return d * lax.rsqrt(var + eps)


def _blockdiag_rows(v, width):
    col = _iota((1, 2 * width), 1)
    left = col < width
    return jnp.concatenate([jnp.where(left, v, 0), jnp.where(left, 0, v)], axis=0)


def _tri_masks():
    ii = _iota((CHUNK, LANES), 0)
    jj = _iota((CHUNK, LANES), 1) % CHUNK
    return jj <= ii, jj >= ii, ii - jj


def _cumsum_mats():
    r = _iota((2 * CHUNK, CHUNK), 0)
    t = _iota((2 * CHUNK, CHUNK), 1)
    pre = (r < CHUNK) & (t <= r)
    suf = (r >= CHUNK) & (t >= r - CHUNK)
    return jnp.where(pre | suf, 1.0, 0.0).astype(F32)


def _lane_expand(x, lane0):
    l = _iota((LANES, LANES), 0)
    c = _iota((LANES, LANES), 1)
    sel = jnp.where(l == lane0 + jnp.where(c >= 64, 1, 0), 1.0, 0.0).astype(F32)
    return _dot(x, sel, precision=HI)


def _lane_rows(x, lane0):
    lane = _iota((1, LANES), 1)
    even = (lane % 2) == 0
    stack = jnp.concatenate([jnp.where(even, x, 0.0), jnp.where(even, 0.0, x)], axis=0)
    off = _iota((8, LANES), 1) - lane0
    pick = jnp.where((off == 0) | (off == 1), 1.0, 0.0).astype(F32)
    return _ntdot(pick, stack, precision=HI)[0:1, :]


def _ret_body(*refs, rope, has_s0, want_state, seq):
    it = iter(refs)
    lg_ref, q_ref, k_ref, v_ref, g_ref = (next(it) for _ in range(5))
    cos_ref = sin_ref = s0_ref = sn_ref = None
    if rope:
        cos_ref, sin_ref = next(it), next(it)
    if has_s0:
        s0_ref = next(it)
    y_ref = next(it)
    if want_state:
        sn_ref = next(it)
    qs, ks, sb_all = next(it), next(it), next(it)

    C = CHUNK
    n = seq // C
    hp = pl.program_id(1)
    m0, lane = _pair_masks()
    lgf0, lgf1 = lg_ref[0, 2 * hp], lg_ref[0, 2 * hp + 1]
    lgb0, lgb1 = lg_ref[1, 2 * hp], lg_ref[1, 2 * hp + 1]
    lgf = jnp.where(m0, lgf0, lgf1)
    lgb = jnp.where(m0, lgb0, lgb1)
    pos = _iota((C, 1), 0).astype(F32)
    e_qf = jnp.exp(lgf * (pos + 1.0))
    e_qb = jnp.exp(lgb * (C - pos))
    e_kf = jnp.exp(lgf * (C - 1.0 - pos))
    e_kb = jnp.exp(lgb * pos)
    row0 = _iota((LANES, 1), 0) < 64
    dec_f = jnp.exp(jnp.where(row0, lgf0, lgf1) * C)
    dec_b = jnp.exp(jnp.where(row0, lgb0, lgb1) * C)
    bd = (_iota((LANES, 2 * LANES), 0) < 64) == (_iota((LANES, 2 * LANES), 1) < LANES)
    _, _, diff = _tri_masks()
    dif = diff.astype(F32)
    dcomb = jnp.where(diff > 0, jnp.exp(lgf * jnp.maximum(dif, 0.0)),
                      jnp.where(diff < 0, jnp.exp(lgb * jnp.maximum(-dif, 0.0)), 2.0))

    rb = min(seq, 512)
    m32 = (lane % 64) < 32

    def swap(x):
        return jnp.where(m32, pltpu.roll(x, 96, 1), pltpu.roll(x, 32, 1))

    def prep(i, carry):
        r0 = pl.multiple_of(i * rb, rb)
        q = q_ref[pl.ds(r0, rb), :].astype(F32)
        k = k_ref[pl.ds(r0, rb), :].astype(F32)
        if rope:
            cs = cos_ref[pl.ds(r0, rb), :]
            sn = sin_ref[pl.ds(r0, rb), :]
            q = q * cs + swap(q) * sn
            k = k * cs + swap(k) * sn
        qs[pl.ds(r0, rb), :] = (q * RET_DK ** -0.5).astype(BF16)
        ks[pl.ds(r0, rb), :] = k.astype(BF16)
        return carry

    lax.fori_loop(0, seq // rb, prep, 0)

    def bd_state(d):
        if not has_s0:
            return jnp.zeros((LANES, 2 * LANES), F32)
        z = jnp.zeros((RET_DK, RET_DV), F32)
        return jnp.concatenate([jnp.concatenate([s0_ref[d, 0], z], axis=1),
                                jnp.concatenate([z, s0_ref[d, 1]], axis=1)], axis=0)

    def bstep(i, sb):
        c = n - 1 - i
        r0 = pl.multiple_of(c * C, C)
        sb_all[c] = sb.astype(BF16)
        kb = (ks[pl.ds(r0, C), :].astype(F32) * e_kb).astype(BF16)
        kv = _tdot(kb, v_ref[pl.ds(r0, C), :])
        return dec_b * sb + jnp.where(bd, kv, 0.0)

    sb_fin = lax.fori_loop(0, n, bstep, bd_state(1), unroll=SWEEP_UNROLL)

    def fstep(c, sf):
        r0 = pl.multiple_of(c * C, C)
        qc = qs[pl.ds(r0, C), :]
        kc = ks[pl.ds(r0, C), :]
        vc = v_ref[pl.ds(r0, C), :]
        qf = qc.astype(F32)
        k2 = jnp.concatenate([jnp.where(m0, kc, 0), jnp.where(m0, 0, kc)], axis=0)
        att = _ntdot(qc, k2) * dcomb
        lhs = jnp.concatenate([att.astype(BF16), (qf * e_qf).astype(BF16), (qf * e_qb).astype(BF16)], axis=1)
        rhs = jnp.concatenate([_blockdiag_rows(vc, LANES), sf.astype(BF16), sb_all[c]], axis=0)
        y = _dot(lhs, rhs)
        y = jnp.concatenate([_head_norm(y[:, :LANES], GN_EPS), _head_norm(y[:, LANES:], GN_EPS)], axis=1)
        y_ref[pl.ds(r0, C), :] = (_silu(g_ref[pl.ds(r0, C), :].astype(F32)) * y).astype(y_ref.dtype)
        kf = (kc.astype(F32) * e_kf).astype(BF16)
        return dec_f * sf + jnp.where(bd, _tdot(kf, vc), 0.0)

    sf_fin = lax.fori_loop(0, n, fstep, bd_state(0), unroll=SWEEP_UNROLL)

    if want_state:
        for d, s in ((0, sf_fin), (1, sb_fin)):
            sn_ref[d, 0] = s[:RET_DK, :RET_DV]
            sn_ref[d, 1] = s[RET_DK:, RET_DV:]


def _retention(lg, p16, g_rows, cos, sin, s0, n_batch, seq, want_state):
    rope = cos is not None
    has_s0 = s0 is not None
    n_tok = n_batch * seq
    in_specs = [pl.BlockSpec(memory_space=pltpu.SMEM),
                pl.BlockSpec((seq, LANES), lambda b, p: (b, M_RQ // LANES + p)),
                pl.BlockSpec((seq, LANES), lambda b, p: (b, M_RK // LANES + p)),
                pl.BlockSpec((seq, 2 * LANES), lambda b, p: (b, M_RV // 256 + p)),
                pl.BlockSpec((seq, 2 * LANES), lambda b, p: (b, M_RG // 256 + p))]
    args = [lg, p16, p16, p16, p16]
    if rope:
        in_specs += [pl.BlockSpec((seq, LANES), lambda b, p: (0, 0))] * 2
        args += [cos, sin]
    st_spec = pl.BlockSpec((None, 2, 2, RET_DK, RET_DV), lambda b, p: (b, 0, p, 0, 0))
    if has_s0:
        in_specs.append(st_spec)
        args.append(s0)
    out_specs = [pl.BlockSpec((seq, 2 * LANES), lambda b, p: (b, p))]
    out_shape = [jax.ShapeDtypeStruct((n_tok, MIX_W), BF16)]
    if want_state:
        out_specs.append(st_spec)
        out_shape.append(jax.ShapeDtypeStruct((n_batch, 2, RET_H, RET_DK, RET_DV), F32))
    res = pl.pallas_call(
        functools.partial(_ret_body, rope=rope, has_s0=has_s0, want_state=want_state, seq=seq),
        grid=(n_batch, RET_H // 2),
        in_specs=in_specs, out_specs=out_specs, out_shape=out_shape,
        scratch_shapes=[pltpu.VMEM((seq, LANES), BF16), pltpu.VMEM((seq, LANES), BF16),
                        pltpu.VMEM((seq // CHUNK, LANES, 2 * LANES), BF16)],
        compiler_params=_params(("parallel", "parallel")),
        name="retention",
    )(*args)
    return res if want_state else (res[0], None)


def _ml_gate_terms(gc, cum, lane_i, lane_f, reverse):
    lf = -_softplus(-gc)
    cs = _dot(cum, lf, precision=HI)
    b = cs[CHUNK:] if reverse else cs[:CHUNK]
    bcol = _lane_expand(b, lane_f)
    icol = _lane_expand(gc, lane_i)
    brow = _lane_rows(b, lane_f)
    irow = _lane_rows(gc, lane_i)
    return bcol, icol, brow, irow


def _ml_update(kc, vc, bcol, icol, cst, n_x, m_x, reverse, bd, row0):
    C = CHUNK
    b_end = bcol[0:1] if reverse else bcol[C - 1:C]
    lwe = b_end - bcol + icol
    m_new = jnp.maximum(b_end + m_x, jnp.max(lwe, axis=0, keepdims=True))
    w_end = jnp.exp(lwe - m_new)
    scale = jnp.exp(b_end + m_x - m_new)
    kw = kc.astype(F32) * w_end
    n_new = scale * n_x + jnp.sum(kw, axis=0, keepdims=True)
    scale_rows = jnp.where(row0, scale[:, 0:1], scale[:, 64:65])
    c_new = scale_rows * cst + jnp.where(bd, _tdot(kw.astype(BF16), vc), 0.0)
    return c_new, n_new, m_new


def _ml_dir_lhs(s2, qf, bcol, brow, irow, n_x, m_x, mask, m0):
    logw = jnp.where(mask, bcol - brow + irow, -jnp.inf)
    inter = bcol + m_x
    mx = jnp.where(m0, jnp.max(jnp.where(m0, logw, -jnp.inf), axis=1, keepdims=True),
                   jnp.max(jnp.where(m0, -jnp.inf, logw), axis=1, keepdims=True))
    mt = jnp.maximum(inter, mx)
    att = s2 * jnp.exp(logw - mt)
    a_in = jnp.exp(inter - mt)
    qn = qf * n_x
    rs = jnp.where(m0, jnp.sum(jnp.where(m0, att, 0.0), axis=1, keepdims=True),
                   jnp.sum(jnp.where(m0, 0.0, att), axis=1, keepdims=True))
    qs = jnp.where(m0, jnp.sum(jnp.where(m0, qn, 0.0), axis=1, keepdims=True),
                   jnp.sum(jnp.where(m0, 0.0, qn), axis=1, keepdims=True))
    den = rs + a_in * qs
    sc = 1.0 / jnp.maximum(jnp.abs(den), jnp.exp(-mt))
    return (att * sc).astype(BF16), (qf * (a_in * sc)).astype(BF16)


def _ml_body(*refs, has_s0, want_state, seq):
    it = iter(refs)
    q_ref, k_ref, v_ref, o_ref, gt_ref, gb_ref, nw_ref = (next(it) for _ in range(7))
    s0_ref = nm0_ref = sn_ref = nmn_ref = None
    if has_s0:
        s0_ref, nm0_ref = next(it), next(it)
    y_ref = next(it)
    if want_state:
        sn_ref, nmn_ref = next(it), next(it)
    cb_all, nmb_all = next(it), next(it)

    C = CHUNK
    n = seq // C
    hp = pl.program_id(1)
    m0, _ = _pair_masks()
    row0 = _iota((LANES, 1), 0) < 64
    bd = (_iota((LANES, 2 * LANES), 0) < 64) == (_iota((LANES, 2 * LANES), 1) < LANES)
    tril, triu, _ = _tri_masks()
    cum = _cumsum_mats()
    gbias = gb_ref[...]
    lane_i = [16 + d * 8 + 2 * hp for d in range(2)]
    lane_f = [20 + d * 8 + 2 * hp for d in range(2)]

    def init(d):
        if not has_s0:
            return (jnp.zeros((LANES, 2 * LANES), F32), jnp.zeros((1, LANES), F32), jnp.zeros((1, LANES), F32))
        z = jnp.zeros((ML_DK, ML_DV), F32)
        cst = jnp.concatenate([jnp.concatenate([s0_ref[d, 0], z], axis=1),
                               jnp.concatenate([z, s0_ref[d, 1]], axis=1)], axis=0)
        return cst, nm0_ref[d:d + 1, :], nm0_ref[2 + d:3 + d, :]

    def bstep(i, carry):
        cst, n_x, m_x = carry
        c = n - 1 - i
        r0 = pl.multiple_of(c * C, C)
        cb_all[c] = cst.astype(BF16)
        nmb_all[c, 0:1, :] = n_x
        nmb_all[c, 1:2, :] = m_x
        gc = gt_ref[pl.ds(r0, C), :] + gbias
        bcol, icol, _, _ = _ml_gate_terms(gc, cum, lane_i[1], lane_f[1], True)
        return _ml_update(k_ref[pl.ds(r0, C), :], v_ref[pl.ds(r0, C), :], bcol, icol, cst, n_x, m_x, True, bd, row0)

    cb_fin = lax.fori_loop(0, n, bstep, init(1), unroll=SWEEP_UNROLL)

    def fstep(c, carry):
        cst, n_x, m_x = carry
        r0 = pl.multiple_of(c * C, C)
        qc = q_ref[pl.ds(r0, C), :]
        kc = k_ref[pl.ds(r0, C), :]
        vc = v_ref[pl.ds(r0, C), :]
        qf = qc.astype(F32) * ML_DK ** -0.5
        k2 = jnp.concatenate([jnp.where(m0, kc, 0), jnp.where(m0, 0, kc)], axis=0)
        s2 = _ntdot(qf.astype(BF16), k2)
        gc = gt_ref[pl.ds(r0, C), :] + gbias
        bcf, icf, brf, irf = _ml_gate_terms(gc, cum, lane_i[0], lane_f[0], False)
        bcb, _, brb, irb = _ml_gate_terms(gc, cum, lane_i[1], lane_f[1], True)
        nb_x = nmb_all[c, 0:1, :]
        mb_x = nmb_all[c, 1:2, :]
        af, qaf = _ml_dir_lhs(s2, qf, bcf, brf, irf, n_x, m_x, tril, m0)
        ab, qab = _ml_dir_lhs(s2, qf, bcb, brb, irb, nb_x, mb_x, triu, m0)
        vbd = _blockdiag_rows(vc, LANES)
        lhs = jnp.concatenate([af, qaf, ab, qab], axis=1)
        rhs = jnp.concatenate([vbd, cst.astype(BF16), vbd, cb_all[c]], axis=0)
        y = _dot(lhs, rhs)
        y = jnp.concatenate([_head_norm(y[:, :LANES], GN_EPS), _head_norm(y[:, LANES:], GN_EPS)], axis=1)
        y = jax.nn.sigmoid(o_ref[pl.ds(r0, C), :].astype(F32)) * (y * nw_ref[...])
        y_ref[pl.ds(r0, C), :] = y.astype(y_ref.dtype)
        return _ml_update(kc, vc, bcf, icf, cst, n_x, m_x, False, bd, row0)

    cf_fin = lax.fori_loop(0, n, fstep, init(0), unroll=SWEEP_UNROLL)

    if want_state:
        nmn_ref[...] = jnp.zeros((8, LANES), F32)
        for d, (cst, n_x, m_x) in ((0, cf_fin), (1, cb_fin)):
            sn_ref[d, 0] = cst[:ML_DK, :ML_DV]
            sn_ref[d, 1] = cst[ML_DK:, ML_DV:]
            nmn_ref[d:d + 1, :] = n_x
            nmn_ref[2 + d:3 + d, :] = m_x


def _mlstm(p16, p32, gbias, nw, s0, nm0, n_batch, seq, want_state):
    has_s0 = s0 is not None
    n_tok = n_batch * seq
    in_specs = [pl.BlockSpec((seq, LANES), lambda b, p: (b, M_MQ // LANES + p)),
                pl.BlockSpec((seq, LANES), lambda b, p: (b, M_MK // LANES + p)),
                pl.BlockSpec((seq, 2 * LANES), lambda b, p: (b, M_MV // 256 + p)),
                pl.BlockSpec((seq, 2 * LANES), lambda b, p: (b, M_MO // 256 + p)),
                pl.BlockSpec((seq, LANES), lambda b, p: (b, S_DT // LANES)),
                pl.BlockSpec((1, LANES), lambda b, p: (0, 0)),
                pl.BlockSpec((1, 2 * LANES), lambda b, p: (0, p))]
    args = [p16, p16, p16, p16, p32, gbias, nw]
    st_spec = pl.BlockSpec((None, 2, 2, ML_DK, ML_DV), lambda b, p: (b, 0, p, 0, 0))
    nm_spec = pl.BlockSpec((None, None, 8, LANES), lambda b, p: (b, p, 0, 0))
    if has_s0:
        in_specs += [st_spec, nm_spec]
        args += [s0, nm0]
    out_specs = [pl.BlockSpec((seq, 2 * LANES), lambda b, p: (b, p))]
    out_shape = [jax.ShapeDtypeStruct((n_tok, MIX_W), BF16)]
    if want_state:
        out_specs += [st_spec, nm_spec]
        out_shape += [jax.ShapeDtypeStruct((n_batch, 2, ML_H, ML_DK, ML_DV), F32),
                      jax.ShapeDtypeStruct((n_batch, ML_H // 2, 8, LANES), F32)]
    res = pl.pallas_call(
        functools.partial(_ml_body, has_s0=has_s0, want_state=want_state, seq=seq),
        grid=(n_batch, ML_H // 2),
        in_specs=in_specs, out_specs=out_specs, out_shape=out_shape,
        scratch_shapes=[pltpu.VMEM((seq // CHUNK, LANES, 2 * LANES), BF16),
                        pltpu.VMEM((seq // CHUNK, 8, LANES), F32)],
        compiler_params=_params(("parallel", "parallel")),
        name="mlstm",
    )(*args)
    return res if want_state else (res[0], None, None)


def _ml_pack_nm(st_n, st_m):
    B = st_n.shape[0]
    n = st_n.reshape(B, 2, ML_H // 2, 2 * ML_DK).transpose(0, 2, 1, 3)
    m = jnp.repeat(st_m.reshape(B, 2, ML_H // 2, 2), ML_DK, axis=-1).transpose(0, 2, 1, 3)
    return jnp.concatenate([n, m, jnp.zeros((B, ML_H // 2, 4, LANES), F32)], axis=2)


def _ml_unpack_nm(nm):
    B = nm.shape[0]
    n = nm[:, :, 0:2, :].transpose(0, 2, 1, 3).reshape(B, 2, ML_H, ML_DK)
    m = nm[:, :, 2:4, :].transpose(0, 2, 1, 3).reshape(B, 2, ML_H, ML_DK)[..., 0]
    return n, m


def _conv3_rows(ref, r0, rb, seq, w, bias):
    x = ref[pl.ds(r0, rb), :].astype(F32)
    prev = ref[pl.ds(pl.multiple_of(jnp.maximum(r0 - 16, 0), 16), 16), :].astype(F32)[15:16]
    nxt = ref[pl.ds(pl.multiple_of(jnp.minimum(r0 + rb, seq - 16), 16), 16), :].astype(F32)[0:1]
    prev = jnp.where(r0 == 0, 0.0, prev)
    nxt = jnp.where(r0 + rb == seq, 0.0, nxt)
    row = _iota((rb, 1), 0)
    x_dn = jnp.where(row == 0, prev, pltpu.roll(x, 1, 0))
    x_up = jnp.where(row == rb - 1, nxt, pltpu.roll(x, rb - 1, 0))
    y = w[0:1] * x_dn + w[1:2] * x + w[2:3] * x_up
    return y if bias is None else y + bias


def _ssd_body(*refs, has_s0, want_state, seq):
    it = iter(refs)
    (x_ref, b_ref, c_ref, z_ref, dt_ref, wx_ref, wb_ref, wc_ref, bx_ref, bb_ref, bc_ref,
     alog_ref, dtb_ref, dd_ref) = (next(it) for _ in range(14))
    s0_ref = sn_ref = None
    if has_s0:
        s0_ref = next(it)
    y_ref = next(it)
    if want_state:
        sn_ref = next(it)
    xs, bs, cs_, dts, sb_all = (next(it) for _ in range(5))

    C = CHUNK
    n = seq // C
    p = pl.program_id(1)
    tril, triu, _ = _tri_masks()
    cum = _cumsum_mats()
    a_lane = -jnp.exp(alog_ref[...])
    dsum = dd_ref[0:1, :] + dd_ref[1:2, :]
    lane_d = [2 * p, 8 + 2 * p]

    rb = min(seq, 256)

    def prep(i, carry):
        r0 = pl.multiple_of(i * rb, rb)
        xs[pl.ds(r0, rb), :] = _silu(_conv3_rows(x_ref, r0, rb, seq, wx_ref[...], bx_ref[...]))
        bs[pl.ds(r0, rb), :] = _silu(_conv3_rows(b_ref, r0, rb, seq, wb_ref[...], bb_ref[...])).astype(BF16)
        cs_[pl.ds(r0, rb), :] = _silu(_conv3_rows(c_ref, r0, rb, seq, wc_ref[...], bc_ref[...])).astype(BF16)
        dts[pl.ds(r0, rb), :] = _softplus(dt_ref[pl.ds(r0, rb), :] + dtb_ref[...])
        return carry

    lax.fori_loop(0, seq // rb, prep, 0)

    def dir_terms(dtc, d, rows):
        cs2 = _dot(cum, dtc * a_lane, precision=HI)
        cs = cs2[C:] if d == 1 else cs2[:C]
        ccol = _lane_expand(cs, lane_d[d])
        dcol = _lane_expand(dtc, lane_d[d])
        if not rows:
            return ccol, dcol, None, None
        return ccol, dcol, _lane_rows(cs, lane_d[d]), _lane_rows(dtc, lane_d[d])

    def update(s, bc, xc, ccol, dcol, d):
        last = ccol[0:1] if d == 1 else ccol[C - 1:C]
        vdt = (xc * dcol * jnp.exp(last - ccol)).astype(BF16)
        return jnp.exp(last) * s + _tdot(bc, vdt)

    def state0(d):
        if not has_s0:
            return jnp.zeros((LANES, LANES), F32)
        return jnp.concatenate([s0_ref[d], jnp.zeros((SSD_N, LANES), F32)], axis=0)

    def bstep(i, sb):
        c = n - 1 - i
        r0 = pl.multiple_of(c * C, C)
        sb_all[c] = sb[:SSD_N].astype(BF16)
        ccol, dcol, _, _ = dir_terms(dts[pl.ds(r0, C), :], 1, False)
        return update(sb, bs[pl.ds(r0, C), :], xs[pl.ds(r0, C), :], ccol, dcol, 1)

    sb_fin = lax.fori_loop(0, n, bstep, state0(1), unroll=SWEEP_UNROLL)

    def fstep(c, sf):
        r0 = pl.multiple_of(c * C, C)
        xc = xs[pl.ds(r0, C), :]
        bc = bs[pl.ds(r0, C), :]
        cc = cs_[pl.ds(r0, C), :]
        dtc = dts[pl.ds(r0, C), :]
        ccf, dcf, crf, drf = dir_terms(dtc, 0, True)
        ccb, dcb, crb, drb = dir_terms(dtc, 1, True)
        m = (jnp.where(tril, jnp.exp(jnp.minimum(ccf - crf, 0.0)) * drf, 0.0)
             + jnp.where(triu, jnp.exp(jnp.minimum(ccb - crb, 0.0)) * drb, 0.0))
        s2 = _ntdot(cc, jnp.concatenate([bc, bc], axis=0))
        att = (s2 * m).astype(BF16)
        sbc = jnp.concatenate([sb_all[c], jnp.zeros((SSD_N, LANES), BF16)], axis=0)
        y = (_dot(att, _blockdiag_rows(xc, 64).astype(BF16))
             + _dot(cc, sf.astype(BF16)) * jnp.exp(ccf)
             + _dot(cc, sbc) * jnp.exp(ccb)
             + dsum * xc)
        y_ref[pl.ds(r0, C), :] = (y * _silu(z_ref[pl.ds(r0, C), :].astype(F32))).astype(y_ref.dtype)
        return update(sf, bc, xc, ccf, dcf, 0)

    sf_fin = lax.fori_loop(0, n, fstep, state0(0), unroll=SWEEP_UNROLL)
    if want_state:
        sn_ref[0] = sf_fin[:SSD_N]
        sn_ref[1] = sb_fin[:SSD_N]


def _ssd(p16, p32, conv_w, conv_b, alog, dtb, dd, s0, n_batch, seq, want_state):
    has_s0 = s0 is not None
    n_tok = n_batch * seq
    col = lambda off: (lambda b, p: (b, off // LANES + p))
    grp = lambda off: (lambda b, p: (b, off // LANES + p // 2))
    wcol = lambda off: (lambda b, p: (0, off + p))
    wgrp = lambda off: (lambda b, p: (0, off + p // 2))
    in_specs = [pl.BlockSpec((seq, LANES), col(M_SX)), pl.BlockSpec((seq, LANES), grp(M_SB)),
                pl.BlockSpec((seq, LANES), grp(M_SC)), pl.BlockSpec((seq, LANES), col(M_SZ)),
                pl.BlockSpec((seq, LANES), lambda b, p: (b, S_DT // LANES)),
                pl.BlockSpec((3, LANES), wcol(0)), pl.BlockSpec((3, LANES), wgrp(4)), pl.BlockSpec((3, LANES), wgrp(6)),
                pl.BlockSpec((1, LANES), wcol(0)), pl.BlockSpec((1, LANES), wgrp(4)), pl.BlockSpec((1, LANES), wgrp(6)),
                pl.BlockSpec((1, LANES), lambda b, p: (0, 0)), pl.BlockSpec((1, LANES), lambda b, p: (0, 0)),
                pl.BlockSpec((2, LANES), wcol(0))]
    args = [p16, p16, p16, p16, p32, conv_w, conv_w, conv_w, conv_b, conv_b, conv_b, alog, dtb, dd]
    st_spec = pl.BlockSpec((None, 2, None, SSD_N, LANES), lambda b, p: (b, 0, p, 0, 0))
    if has_s0:
        in_specs.append(st_spec)
        args.append(s0)
    out_specs = [pl.BlockSpec((seq, LANES), lambda b, p: (b, p))]
    out_shape = [jax.ShapeDtypeStruct((n_tok, MIX_W), BF16)]
    if want_state:
        out_specs.append(st_spec)
        out_shape.append(jax.ShapeDtypeStruct((n_batch, 2, SSD_H // 2, SSD_N, LANES), F32))
    res = pl.pallas_call(
        functools.partial(_ssd_body, has_s0=has_s0, want_state=want_state, seq=seq),
        grid=(n_batch, SSD_H // 2),
        in_specs=in_specs, out_specs=out_specs, out_shape=out_shape,
        scratch_shapes=[pltpu.VMEM((seq, LANES), F32), pltpu.VMEM((seq, LANES), BF16), pltpu.VMEM((seq, LANES), BF16),
                        pltpu.VMEM((seq, LANES), F32), pltpu.VMEM((seq // CHUNK, SSD_N, LANES), BF16)],
        compiler_params=_params(("parallel", "parallel")),
        name="ssd",
    )(*args)
    return res if want_state else (res[0], None)


def _ssd_pack_state(st):
    B = st.shape[0]
    return st.reshape(B, 2, SSD_H // 2, 2, SSD_N, SSD_P).transpose(0, 1, 2, 4, 3, 5).reshape(B, 2, SSD_H // 2, SSD_N, 2 * SSD_P)


def _ssd_unpack_state(st):
    B = st.shape[0]
    return st.reshape(B, 2, SSD_H // 2, SSD_N, 2, SSD_P).transpose(0, 1, 2, 4, 3, 5).reshape(B, 2, SSD_H, SSD_N, SSD_P)


def _seg_ones():
    r = _iota((MIX_W, MIX_W), 0) // RWKV_D
    c = _iota((MIX_W, MIX_W), 1) // RWKV_D
    return jnp.where(r == c, 1.0, 0.0).astype(BF16)


def _seg_sum(x, ones):
    hi = x.astype(BF16)
    lo = (x - hi.astype(F32)).astype(BF16)
    return _dot(hi, ones) + _dot(lo, ones)


def _rwkv_prep_body(x_ref, xp_ref, xn_ref, sm_ref, cw_ref, w0_ref, wup_ref, a0_ref, aup_ref,
                    kk_ref, ka_ref, rk_ref, r_o, v_o, a_o, w_o, kd_o, b_o, bon_o, *, rb):
    j = pl.program_id(1)
    nj = pl.num_programs(1)
    prev = jnp.where(j == 0, 0.0, xp_ref[...].astype(F32)[15:16])
    nxt = jnp.where(j == nj - 1, 0.0, xn_ref[...].astype(F32)[0:1])
    row = _iota((rb, 1), 0)
    cw = cw_ref[...]

    def conv(c0):
        x = x_ref[:, c0:c0 + MIX_W].astype(F32)
        x_dn = jnp.where(row == 0, prev[:, c0:c0 + MIX_W], pltpu.roll(x, 1, 0))
        x_up = jnp.where(row == rb - 1, nxt[:, c0:c0 + MIX_W], pltpu.roll(x, rb - 1, 0))
        w = cw[:, c0:c0 + MIX_W]
        return w[0:1] * x_dn + w[1:2] * x + w[2:3] * x_up

    r_ = conv(0)
    k_ = conv(MIX_W)
    v_ = conv(2 * MIX_W)
    ones = _seg_ones()
    kk = k_ * kk_ref[...]
    nrm = jnp.sqrt(_seg_sum(kk * kk, ones))
    kk = kk / jnp.maximum(nrm, 1e-12)
    lora = sm_ref[...]
    w_low = jnp.tanh(lora).astype(BF16)
    a_low = lora.astype(BF16)
    kd_sum = jnp.zeros_like(k_)
    for d in range(2):
        w_log = -_softplus(-(w0_ref[d:d + 1, :] + _dot(w_low, wup_ref[d]))) - 0.5
        w_o[d] = jnp.exp(-jnp.exp(w_log))
        iclr = jax.nn.sigmoid(a0_ref[d:d + 1, :] + _dot(a_low, aup_ref[d]))
        kd = k_ * (1.0 + (iclr - 1.0) * ka_ref[...])
        kd_o[d] = kd
        b_o[d] = kk * iclr
        kd_sum = kd_sum + kd
    r_o[...] = r_
    v_o[...] = v_
    a_o[...] = -kk
    bon_o[...] = _seg_sum(r_ * kd_sum * rk_ref[...], ones) * v_


def _rwkv_prep(p16, p32, conv_w, w0, wup, a0, aup, k_k, k_a, r_k, n_batch, seq):
    n_tok = n_batch * seq
    rb = 256
    nj = seq // rb
    nb16 = seq // 16
    one = lambda b, j: (0, 0)
    tok = jax.ShapeDtypeStruct((n_tok, MIX_W), F32)
    tok2 = jax.ShapeDtypeStruct((2, n_tok, MIX_W), F32)
    o1 = pl.BlockSpec((rb, MIX_W), lambda b, j: (b * nj + j, 0))
    o2 = pl.BlockSpec((2, rb, MIX_W), lambda b, j: (0, b * nj + j, 0))
    return pl.pallas_call(
        functools.partial(_rwkv_prep_body, rb=rb),
        grid=(n_batch, nj),
        in_specs=[pl.BlockSpec((rb, 3 * MIX_W), lambda b, j: (b * nj + j, M_RWKV // (3 * MIX_W))),
                  pl.BlockSpec((16, 3 * MIX_W),
                               lambda b, j: (b * nb16 + jnp.maximum(j * (rb // 16) - 1, 0), M_RWKV // (3 * MIX_W))),
                  pl.BlockSpec((16, 3 * MIX_W),
                               lambda b, j: (b * nb16 + jnp.minimum((j + 1) * (rb // 16), nb16 - 1), M_RWKV // (3 * MIX_W))),
                  pl.BlockSpec((rb, LANES), lambda b, j: (b * nj + j, S_WWD // LANES)),
                  pl.BlockSpec((3, 3 * MIX_W), one),
                  pl.BlockSpec((2, MIX_W), one), pl.BlockSpec((2, LANES, MIX_W), lambda b, j: (0, 0, 0)),
                  pl.BlockSpec((2, MIX_W), one), pl.BlockSpec((2, LANES, MIX_W), lambda b, j: (0, 0, 0)),
                  pl.BlockSpec((1, MIX_W), one), pl.BlockSpec((1, MIX_W), one), pl.BlockSpec((1, MIX_W), one)],
        out_specs=[o1, o1, o1, o2, o2, o2, o1],
        out_shape=[tok, tok, tok, tok2, tok2, tok2, tok],
        compiler_params=_params(("parallel", "parallel")),
        name="rwkv_prep",
    )(p16, p16, p16, p32, conv_w, w0, wup, a0, aup, k_k, k_a, r_k)


def _rwkv_scan_body(*refs, has_s0, want_state, tb):
    it = iter(refs)
    r_ref, w_ref, k_ref, v_ref, a_ref, b_ref = (next(it) for _ in range(6))
    s0_ref = sn_ref = None
    if has_s0:
        s0_ref = next(it)
    y_ref = next(it)
    if want_state:
        sn_ref = next(it)
    st = next(it)
    i = pl.program_id(1)

    @pl.when(i == 0)
    def _():
        st[...] = s0_ref[...] if has_s0 else jnp.zeros(st.shape, F32)

    def step(t, carry):
        sa = jnp.zeros((RWKV_D, LANES), F32)
        for kx in range(RWKV_D):
            sa = sa + st[kx] * a_ref[t, pl.ds(kx, 1), :]
        vt = v_ref[t]
        y = jnp.zeros((RWKV_D, LANES), F32)
        for kx in range(RWKV_D):
            row = pl.ds(kx, 1)
            sn = st[kx] * w_ref[t, row, :] + sa * b_ref[t, row, :] + vt * k_ref[t, row, :]
            st[kx] = sn
            y = y + sn * r_ref[t, row, :]
        y_ref[t] = y
        return carry

    lax.fori_loop(0, tb, step, 0)

    if want_state:
        @pl.when(i == pl.num_programs(1) - 1)
        def _():
            sn_ref[...] = st[...]


def _rwkv_scan(r, w, k, v, a, b, s0, want_state):
    seq, _, n_lane = r.shape
    tb = 32
    has_s0 = s0 is not None
    blk = pl.BlockSpec((tb, RWKV_D, LANES), lambda g, i: (i, 0, g))
    sblk = pl.BlockSpec((RWKV_D, RWKV_D, LANES), lambda g, i: (0, 0, g))
    in_specs = [blk] * 6
    args = [r, w, k, v, a, b]
    if has_s0:
        in_specs.append(sblk)
        args.append(s0)
    out_specs = [blk]
    out_shape = [jax.ShapeDtypeStruct((seq, RWKV_D, n_lane), F32)]
    if want_state:
        out_specs.append(sblk)
        out_shape.append(jax.ShapeDtypeStruct((RWKV_D, RWKV_D, n_lane), F32))
    res = pl.pallas_call(
        functools.partial(_rwkv_scan_body, has_s0=has_s0, want_state=want_state, tb=tb),
        grid=(n_lane // LANES, seq // tb),
        in_specs=in_specs, out_specs=out_specs, out_shape=out_shape,
        scratch_shapes=[pltpu.VMEM((RWKV_D, RWKV_D, LANES), F32)],
        compiler_params=_params(("parallel", "arbitrary")),
        name="rwkv_scan",
    )(*args)
    return res if want_state else (res[0], None)


def _rwkv_post_body(yf_ref, yb_ref, bon_ref, g_ref, gup_ref, lw_ref, lb_ref, o_ref):
    ones = _seg_ones()
    y = yf_ref[...] + yb_ref[...]
    mu = _seg_sum(y, ones) * (1.0 / RWKV_D)
    d = y - mu
    var = _seg_sum(d * d, ones) * (1.0 / RWKV_D)
    out = d * lax.rsqrt(var + RWKV_GN_EPS) * lw_ref[...] + lb_ref[...] + bon_ref[...]
    g = _dot(jax.nn.sigmoid(g_ref[...]).astype(BF16), gup_ref[...])
    o_ref[...] = (out * g).astype(o_ref.dtype)


def _rwkv_post(yf, yb, bonus, p32, g_up, ln_w, ln_b, tm):
    n_tok = yf.shape[0]
    tokb = pl.BlockSpec((tm, MIX_W), lambda i: (i, 0))
    one = lambda i: (0, 0)
    return pl.pallas_call(
        _rwkv_post_body,
        grid=(n_tok // tm,),
        in_specs=[tokb, tokb, tokb, pl.BlockSpec((tm, LANES), lambda i: (i, S_WGD // LANES)),
                  pl.BlockSpec((LANES, MIX_W), one), pl.BlockSpec((1, MIX_W), one), pl.BlockSpec((1, MIX_W), one)],
        out_specs=tokb,
        out_shape=jax.ShapeDtypeStruct((n_tok, MIX_W), BF16),
        compiler_params=_params(("parallel",)),
        name="rwkv_post",
    )(yf, yb, bonus, p32, g_up, ln_w, ln_b)


def _to_scan_layout(x, n_batch, seq):
    return x.reshape(n_batch, seq, RWKV_H, RWKV_D).transpose(1, 3, 0, 2).reshape(seq, RWKV_D, n_batch * RWKV_H)


def _from_scan_layout(y, n_batch, seq):
    return y.reshape(seq, RWKV_D, n_batch, RWKV_H).transpose(2, 0, 3, 1).reshape(n_batch * seq, MIX_W)


def _rwkv(p16, p32, lw, s0, n_batch, seq, want_state, tm):
    r_, v_, a_, w_, kd_, b_, bonus = _rwkv_prep(p16, p32, lw["conv_w"], lw["w0"], lw["w_up"], lw["a0"], lw["a_up"],
                                                lw["k_k"], lw["k_a"], lw["r_k"], n_batch, seq)
    tl = lambda x: _to_scan_layout(x, n_batch, seq)
    both = lambda x: jnp.concatenate([tl(x), tl(x)[::-1]], axis=-1)
    per_dir = lambda x: jnp.concatenate([tl(x[0]), tl(x[1])[::-1]], axis=-1)
    s0t = None
    if s0 is not None:
        s0t = s0.transpose(4, 3, 1, 0, 2).reshape(RWKV_D, RWKV_D, 2 * n_batch * RWKV_H)
    y, sn = _rwkv_scan(both(r_), per_dir(w_), per_dir(kd_), both(v_), both(a_), per_dir(b_), s0t, want_state)
    nl = n_batch * RWKV_H
    yf = _from_scan_layout(y[:, :, :nl], n_batch, seq)
    yb = _from_scan_layout(y[::-1, :, nl:], n_batch, seq)
    out = _rwkv_post(yf, yb, bonus, p32, lw["g_up"], lw["ln_w"], lw["ln_b"], tm)
    if want_state:
        sn = sn.reshape(RWKV_D, RWKV_D, 2, n_batch, RWKV_H).transpose(3, 2, 4, 1, 0)
    return out, sn


def _seg_ones_pair():
    r = _iota((LANES, LANES), 0) // RWKV_D
    c = _iota((LANES, LANES), 1) // RWKV_D
    return jnp.where(r == c, 1.0, 0.0).astype(BF16)


def _b16(x):
    return x.astype(BF16)


def _rwkv_chunk_stage_a(job, cum, refs):
    d, r0, rev = job["d"], job["r0"], job["rev"]
    rs, vs, as_, lws, kds, bs = refs
    C = CHUNK
    lw = lws[d, pl.ds(r0, C), :]
    cs = _dot(cum, lw, precision=HI)
    g = cs[C:] if rev else cs[:C]
    g_end = g[0:1] if rev else g[C - 1:C]
    r, a = rs[pl.ds(r0, C), :], as_[pl.ds(r0, C), :]
    kd, b = kds[d, pl.ds(r0, C), :], bs[d, pl.ds(r0, C), :]
    e_g, e_ng, e_end = jnp.exp(g), jnp.exp(-g), jnp.exp(g_end - g)
    job.update(v=vs[pl.ds(r0, C), :], at=a * jnp.exp(g - lw), rt=r * e_g, bt=b * e_ng, kt=kd * e_ng,
               bh=b * e_end, kh=kd * e_end, dec=jnp.exp(g_end))


def _rwkv_chunk_stage_b(job, m0, strict_f, incl_f, strict_b, incl_b):
    C = CHUNK
    strict, incl = (strict_b, incl_b) if job["rev"] else (strict_f, incl_f)
    lhs = _b16(jnp.concatenate([job["at"], job["rt"]], axis=0))
    bt, kt = job["bt"], job["kt"]
    rhs = _b16(jnp.concatenate([jnp.where(m0, bt, 0.0), jnp.where(m0, 0.0, bt),
                                jnp.where(m0, kt, 0.0), jnp.where(m0, 0.0, kt)], axis=0))
    a4 = _ntdot(lhs, rhs)
    job.update(lab=jnp.where(strict, a4[:C, :LANES], 0.0), lak=jnp.where(strict, a4[:C, LANES:], 0.0),
               mrb=jnp.where(incl, a4[C:, :LANES], 0.0), mrk=jnp.where(incl, a4[C:, LANES:], 0.0))
    lbd = _blockdiag_rows(job["lab"], 64)
    eye = _iota((LANES, LANES), 0) == _iota((LANES, LANES), 1)
    job.update(tm=jnp.where(eye, 1.0, 0.0) + lbd, lbd=lbd)


def _rwkv_chunk_inverse(jobs):
    pws = [_dot(_b16(j["lbd"]), _b16(j["lbd"])) for j in jobs]
    levels = int(math.log2(CHUNK)) - 1
    for lvl in range(levels):
        for j, pw in zip(jobs, pws):
            j["tm"] = j["tm"] + _dot(_b16(j["tm"]), _b16(pw))
        if lvl < levels - 1:
            pws = [_dot(_b16(pw), _b16(pw)) for pw in pws]


def _rwkv_chunk_stage_c(job):
    C = CHUNK
    v = job["v"]
    at_rows = _blockdiag_rows(job["at"], 64)
    v_rows = _blockdiag_rows(v, 64)
    wv = _dot(_b16(job["lak"]), _b16(v_rows))
    job.update(at_rows=at_rows, v_rows=v_rows, wv_rows=_blockdiag_rows(wv, 64))


def _rwkv_chunk_stage_d(job):
    C = CHUNK
    tx = _dot(_b16(job["tm"]), _b16(jnp.concatenate([job["at_rows"], job["wv_rows"]], axis=1)))
    job.update(ah_rows=tx[:, :LANES], uv_rows=tx[:, LANES:],
               ah=tx[:C, :LANES] + tx[C:, :LANES], uv=tx[:C, LANES:] + tx[C:, LANES:])


def _rwkv_chunk_stage_e(job, bd128, eye):
    C = CHUNK
    z = jnp.zeros((2 * C, LANES), F32)
    rhs_m = jnp.concatenate([jnp.concatenate([job["ah_rows"], job["uv_rows"]], axis=1),
                             jnp.concatenate([z, job["v_rows"]], axis=1)], axis=0)
    my = _dot(_b16(jnp.concatenate([job["mrb"], job["mrk"]], axis=1)), _b16(rhs_m))
    zc = jnp.zeros((C, LANES), F32)
    rhs_g = jnp.concatenate([jnp.concatenate([job["ah"], job["uv"]], axis=1),
                             jnp.concatenate([zc, job["v"]], axis=1)], axis=0)
    gh = _tdot(_b16(jnp.concatenate([job["bh"], job["kh"]], axis=0)), _b16(rhs_g))
    job.update(rh=job["rt"] + my[:, :LANES], yv=my[:, LANES:],
               gm=jnp.where(bd128, gh[:, :LANES], 0.0) + jnp.where(eye, job["dec"], 0.0),
               hm=jnp.where(bd128, gh[:, LANES:], 0.0))


def _rwkv_chunk_stage_f(job, p):
    C = CHUNK
    out = _dot(_b16(jnp.concatenate([job["rh"], job["gm"]], axis=0)), _b16(p))
    return out[:C] + job["yv"], out[C:] + job["hm"]


def _rwkv_body(*refs, has_s0, want_state, seq):
    it = iter(refs)
    (r_ref, k_ref, v_ref, lora_ref, g_ref, cwr_ref, cwk_ref, cwv_ref, w0_ref, wup_ref, a0_ref, aup_ref,
     kkw_ref, kaw_ref, rkw_ref, gup_ref, lnw_ref, lnb_ref) = (next(it) for _ in range(18))
    s0_ref = sn_ref = None
    if has_s0:
        s0_ref = next(it)
    y_ref = next(it)
    if want_state:
        sn_ref = next(it)
    rs, vs, as_, lws, kds, bs, yf, yb = (next(it) for _ in range(8))

    C = CHUNK
    n = seq // C
    m0, _ = _pair_masks()
    ones = _seg_ones_pair()
    rb = min(seq, 256)

    def prep(i, carry):
        r0 = pl.multiple_of(i * rb, rb)
        r_ = _conv3_rows(r_ref, r0, rb, seq, cwr_ref[...], None)
        k_ = _conv3_rows(k_ref, r0, rb, seq, cwk_ref[...], None)
        rs[pl.ds(r0, rb), :] = r_
        vs[pl.ds(r0, rb), :] = _conv3_rows(v_ref, r0, rb, seq, cwv_ref[...], None)
        kk = k_ * kkw_ref[...]
        kk = kk / jnp.maximum(jnp.sqrt(_seg_sum(kk * kk, ones)), 1e-12)
        as_[pl.ds(r0, rb), :] = -kk
        lora = lora_ref[pl.ds(r0, rb), :]
        w_low, a_low = _b16(jnp.tanh(lora)), _b16(lora)
        for d in range(2):
            w_log = -_softplus(-(w0_ref[d:d + 1, :] + _dot(w_low, wup_ref[d]))) - 0.5
            lws[d, pl.ds(r0, rb), :] = -jnp.exp(w_log)
            iclr = jax.nn.sigmoid(a0_ref[d:d + 1, :] + _dot(a_low, aup_ref[d]))
            kds[d, pl.ds(r0, rb), :] = k_ * (1.0 + (iclr - 1.0) * kaw_ref[...])
            bs[d, pl.ds(r0, rb), :] = kk * iclr
        return carry

    lax.fori_loop(0, seq // rb, prep, 0)

    cum = _cumsum_mats()
    ii = _iota((C, LANES), 0)
    jj = _iota((C, LANES), 1) % C
    strict_f, incl_f, strict_b, incl_b = jj < ii, jj <= ii, jj > ii, jj >= ii
    bd128 = (_iota((LANES, LANES), 0) < 64) == (_iota((LANES, LANES), 1) < 64)
    eye = _iota((LANES, LANES), 0) == _iota((LANES, LANES), 1)
    scr = (rs, vs, as_, lws, kds, bs)

    nch = RWKV_CHUNKS_PER_STEP

    def sweep(i, carry):
        pf, pb = carry
        jobs = []
        for q in range(nch):
            jobs.append(dict(d=0, rev=False, r0=pl.multiple_of((i * nch + q) * C, C)))
            jobs.append(dict(d=1, rev=True, r0=pl.multiple_of((n - 1 - i * nch - q) * C, C)))
        for j in jobs:
            _rwkv_chunk_stage_a(j, cum, scr)
        for j in jobs:
            _rwkv_chunk_stage_b(j, m0, strict_f, incl_f, strict_b, incl_b)
        _rwkv_chunk_inverse(jobs)
        for j in jobs:
            _rwkv_chunk_stage_c(j)
        for j in jobs:
            _rwkv_chunk_stage_d(j)
        for j in jobs:
            _rwkv_chunk_stage_e(j, bd128, eye)
        for q in range(nch):
            y_f, pf = _rwkv_chunk_stage_f(jobs[2 * q], pf)
            y_b, pb = _rwkv_chunk_stage_f(jobs[2 * q + 1], pb)
            yf[pl.ds(jobs[2 * q]["r0"], C), :] = y_f
            yb[pl.ds(jobs[2 * q + 1]["r0"], C), :] = y_b
        return pf, pb

    p0 = (s0_ref[0], s0_ref[1]) if has_s0 else (jnp.zeros((LANES, LANES), F32),) * 2
    pf, pb = lax.fori_loop(0, n // nch, sweep, p0)
    if want_state:
        sn_ref[0] = pf
        sn_ref[1] = pb

    def post(i, carry):
        r0 = pl.multiple_of(i * rb, rb)
        rows = pl.ds(r0, rb)
        y = yf[rows, :] + yb[rows, :]
        mu = _seg_sum(y, ones) * (1.0 / RWKV_D)
        dv = y - mu
        var = _seg_sum(dv * dv, ones) * (1.0 / RWKV_D)
        bonus = _seg_sum(rs[rows, :] * (kds[0, rows, :] + kds[1, rows, :]) * rkw_ref[...], ones) * vs[rows, :]
        out = dv * lax.rsqrt(var + RWKV_GN_EPS) * lnw_ref[...] + lnb_ref[...] + bonus
        g = _dot(_b16(jax.nn.sigmoid(g_ref[rows, :])), gup_ref[...])
        y_ref[rows, :] = (out * g).astype(y_ref.dtype)
        return carry

    lax.fori_loop(0, seq // rb, post, 0)


def _rwkv_chunked(p16, p32, lw, s0, n_batch, seq, want_state):
    has_s0 = s0 is not None
    n_tok = n_batch * seq
    base = M_RWKV // LANES
    tok = lambda off: pl.BlockSpec((seq, LANES), lambda b, p: (b, off + p))
    wcol = lambda rows: pl.BlockSpec((rows, LANES), lambda b, p: (0, p))
    in_specs = [tok(base), tok(base + 4), tok(base + 8),
                pl.BlockSpec((seq, LANES), lambda b, p: (b, S_WWD // LANES)),
                pl.BlockSpec((seq, LANES), lambda b, p: (b, S_WGD // LANES)),
                pl.BlockSpec((3, LANES), lambda b, p: (0, p)),
                pl.BlockSpec((3, LANES), lambda b, p: (0, 4 + p)),
                pl.BlockSpec((3, LANES), lambda b, p: (0, 8 + p)),
                wcol(2), pl.BlockSpec((2, LANES, LANES), lambda b, p: (0, 0, p)),
                wcol(2), pl.BlockSpec((2, LANES, LANES), lambda b, p: (0, 0, p)),
                wcol(1), wcol(1), wcol(1),
                pl.BlockSpec((LANES, LANES), lambda b, p: (0, p)), wcol(1), wcol(1)]
    args = [p16, p16, p16, p32, p32, lw["conv_w"], lw["conv_w"], lw["conv_w"], lw["w0"], lw["w_up"], lw["a0"],
            lw["a_up"], lw["k_k"], lw["k_a"], lw["r_k"], lw["g_up"], lw["ln_w"], lw["ln_b"]]
    st_spec = pl.BlockSpec((None, 2, None, LANES, LANES), lambda b, p: (b, 0, p, 0, 0))
    if has_s0:
        in_specs.append(st_spec)
        args.append(s0)
    out_specs = [pl.BlockSpec((seq, LANES), lambda b, p: (b, p))]
    out_shape = [jax.ShapeDtypeStruct((n_tok, MIX_W), BF16)]
    if want_state:
        out_specs.append(st_spec)
        out_shape.append(jax.ShapeDtypeStruct((n_batch, 2, RWKV_H // 2, LANES, LANES), F32))
    tokf = pltpu.VMEM((seq, LANES), F32)
    tok2 = pltpu.VMEM((2, seq, LANES), F32)
    res = pl.pallas_call(
        functools.partial(_rwkv_body, has_s0=has_s0, want_state=want_state, seq=seq),
        grid=(n_batch, RWKV_H // 2),
        in_specs=in_specs, out_specs=out_specs, out_shape=out_shape,
        scratch_shapes=[tokf, tokf, tokf, tok2, tok2, tok2, tokf, tokf],
        compiler_params=_params(("parallel", "parallel")),
        name="rwkv",
    )(*args)
    return res if want_state else (res[0], None)


def _rwkv_pack_state(st):
    B = st.shape[0]
    p = jnp.swapaxes(st, -1, -2).reshape(B, 2, RWKV_H // 2, 2, RWKV_D, RWKV_D)
    z = jnp.zeros_like(p[:, :, :, 0])
    return jnp.concatenate([jnp.concatenate([p[:, :, :, 0], z], axis=-1),
                            jnp.concatenate([z, p[:, :, :, 1]], axis=-1)], axis=-2)


def _rwkv_unpack_state(pbd):
    h0 = pbd[:, :, :, :RWKV_D, :RWKV_D]
    h1 = pbd[:, :, :, RWKV_D:, RWKV_D:]
    p = jnp.stack([h0, h1], axis=3)
    B = p.shape[0]
    return jnp.swapaxes(p, -1, -2).reshape(B, 2, RWKV_H, RWKV_D, RWKV_D)


def _merge_body(yr_ref, ys_ref, yw_ref, ym_ref, g_ref, wb_ref, wo_ref, x_ref, mod_ref, sn_ref, o_ref):
    ys = ys_ref[...].astype(F32)
    ys = ys * lax.rsqrt(jnp.mean(ys * ys, axis=-1, keepdims=True) + EPS) * sn_ref[...]
    branches = (yr_ref[...], ys.astype(BF16), yw_ref[...], ym_ref[...])
    merged = None
    for i, br in enumerate(branches):
        gate = jax.nn.sigmoid(g_ref[:, i * D_MODEL:(i + 1) * D_MODEL].astype(F32))
        term = gate * _dot(br, wb_ref[i])
        merged = term if merged is None else merged + term
    out = _dot(merged.astype(BF16), wo_ref[...])
    o_ref[...] = x_ref[...] + mod_ref[2:3, :] * out


def _merge(y_ret, y_ssd, y_rw, y_ml, p16, w_branch, w_out, x, mod, ssd_norm, tm, seq):
    n_tok = x.shape[0]
    mi = _mod_index(mod.shape[0], tm, seq)
    yb = pl.BlockSpec((tm, MIX_W), lambda i: (i, 0))
    xb = pl.BlockSpec((tm, D_MODEL), lambda i: (i, 0))
    return pl.pallas_call(
        _merge_body,
        grid=(n_tok // tm,),
        in_specs=[yb, yb, yb, yb,
                  pl.BlockSpec((tm, N_BRANCH * D_MODEL), lambda i: (i, M_GATE)),
                  pl.BlockSpec((N_BRANCH, MIX_W, D_MODEL), lambda i: (0, 0, 0)),
                  pl.BlockSpec((D_MODEL, D_MODEL), lambda i: (0, 0)),
                  xb,
                  pl.BlockSpec((None, 8, D_MODEL), lambda i: (mi(i), 0, 0)),
                  pl.BlockSpec((1, MIX_W), lambda i: (0, 0))],
        out_specs=xb,
        out_shape=jax.ShapeDtypeStruct((n_tok, D_MODEL), F32),
        compiler_params=_params(("parallel",)),
        name="merge",
    )(y_ret, y_ssd, y_rw, y_ml, p16, w_branch, w_out, x, mod, ssd_norm)


def _ffn_body(x_ref, xp_ref, xn_ref, mod_ref, nw_ref, uv_ref, ug_ref, cwv_ref, cwg_ref, cbv_ref, cbg_ref,
              dn_ref, fw_ref, o_ref, h_ref, *, tm, seq, final):
    i = pl.program_id(0)
    f = pl.program_id(1)
    shift, scale, gate = mod_ref[3:4, :], mod_ref[4:5, :], mod_ref[5:6, :]

    @pl.when(f == 0)
    def _():
        nw = nw_ref[...]
        h_ref[0:tm, :] = _rms_mod(x_ref[...], nw, shift, scale).astype(BF16)
        h_ref[tm:tm + 8, :] = _rms_mod(xp_ref[...], nw, shift, scale).astype(BF16)
        h_ref[tm + 8:tm + 16, :] = _rms_mod(xn_ref[...], nw, shift, scale).astype(BF16)
        o_ref[...] = jnp.zeros(o_ref.shape, F32)

    row = _iota((tm, 1), 0)
    pos = (i * tm + row) % seq
    first, last = pos == 0, pos == seq - 1
    h = h_ref[...]

    def conv(u_ref, cw_ref, cb_ref):
        u = _dot(h, u_ref[...])
        um = u[0:tm]
        u_dn = jnp.where(row == 0, u[tm + 7:tm + 8], pltpu.roll(um, 1, 0))
        u_up = jnp.where(row == tm - 1, u[tm + 8:tm + 9], pltpu.roll(um, tm - 1, 0))
        u_dn = jnp.where(first, 0.0, u_dn)
        u_up = jnp.where(last, 0.0, u_up)
        cw = cw_ref[...]
        return cw[0:1] * u_dn + cw[1:2] * um + cw[2:3] * u_up + cb_ref[...]

    act = conv(uv_ref, cwv_ref, cbv_ref) * _silu(conv(ug_ref, cwg_ref, cbg_ref))
    o_ref[...] += _dot(act.astype(BF16), dn_ref[...])

    @pl.when(f == pl.num_programs(1) - 1)
    def _():
        xn = x_ref[...] + gate * o_ref[...]
        if final:
            xn = xn * lax.rsqrt(jnp.mean(xn * xn, axis=-1, keepdims=True) + EPS) * fw_ref[...]
        o_ref[...] = xn


def _ffn(x, mod, nw, up, conv_w, conv_b, down, final_w, tm, seq, final):
    n_tok = x.shape[0]
    fc = FFN_DIM // 2
    nf = FFN_DIM // fc
    mi = _mod_index(mod.shape[0], tm, seq)
    n8 = n_tok // 8
    xb = pl.BlockSpec((tm, D_MODEL), lambda i, f: (i, 0))
    one = lambda i, f: (0, 0)
    return pl.pallas_call(
        functools.partial(_ffn_body, tm=tm, seq=seq, final=final),
        grid=(n_tok // tm, nf),
        in_specs=[xb,
                  pl.BlockSpec((8, D_MODEL), lambda i, f: (jnp.maximum(i * (tm // 8) - 1, 0), 0)),
                  pl.BlockSpec((8, D_MODEL), lambda i, f: (jnp.minimum((i + 1) * (tm // 8), n8 - 1), 0)),
                  pl.BlockSpec((None, 8, D_MODEL), lambda i, f: (mi(i), 0, 0)),
                  pl.BlockSpec((1, D_MODEL), one),
                  pl.BlockSpec((D_MODEL, fc), lambda i, f: (0, f)),
                  pl.BlockSpec((D_MODEL, fc), lambda i, f: (0, nf + f)),
                  pl.BlockSpec((3, fc), lambda i, f: (0, f)),
                  pl.BlockSpec((3, fc), lambda i, f: (0, nf + f)),
                  pl.BlockSpec((1, fc), lambda i, f: (0, f)),
                  pl.BlockSpec((1, fc), lambda i, f: (0, nf + f)),
                  pl.BlockSpec((fc, D_MODEL), lambda i, f: (f, 0)),
                  pl.BlockSpec((1, D_MODEL), one)],
        out_specs=xb,
        out_shape=jax.ShapeDtypeStruct((n_tok, D_MODEL), F32),
        scratch_shapes=[pltpu.VMEM((tm + 16, D_MODEL), BF16)],
        compiler_params=_params(("parallel", "arbitrary")),
        name="conv_ffn",
    )(x, x, x, mod, nw, up, up, conv_w, conv_w, conv_b, conv_b, down, final_w)


def _rope_tables(seq):
    rows = seq // GRID_W
    rr, cc = jnp.meshgrid(jnp.arange(rows), jnp.arange(GRID_W), indexing='ij')
    nf = RET_DK // 4
    inv = ROPE_BASE ** (-jnp.arange(nf, dtype=F32) / nf)
    ang = jnp.concatenate([rr.reshape(-1, 1) * inv, cc.reshape(-1, 1) * inv], axis=-1)
    cos, sin = jnp.cos(ang), jnp.sin(ang)
    return (jnp.tile(jnp.concatenate([cos, cos], axis=-1), (1, 2)),
            jnp.tile(jnp.concatenate([-sin, sin], axis=-1), (1, 2)))


def _lanes16(a):
    return jnp.zeros((1, LANES), F32).at[0, :16].set(a.reshape(-1).astype(F32))


def _pad_rows(w, lo):
    return jnp.zeros((2, LANES, MIX_W), F32).at[:, lo:lo + w.shape[1]].set(w).astype(BF16)


def _layer(x, mod, lw, states, rope, n_batch, seq, want_state, final, final_w):
    tm_p = 1024
    tm = 512
    p16 = _in_proj(x, mod, lw["norm1"], lw["w_main"], BF16, tm_p, 2048, seq, "in_proj_main")
    p32 = _in_proj(x, mod, lw["norm1"], lw["w_small"], F32, tm_p, N_SMALL, seq, "in_proj_small")
    st_ret, st_ssd, st_rwkv, st_c, st_nm = states
    cos, sin = rope if rope is not None else (None, None)
    y_ret, n_ret = _retention(lw["ret_lg"], p16, None, cos, sin, st_ret, n_batch, seq, want_state)
    y_ssd, n_ssd = _ssd(p16, p32, lw["ssd_conv_w"], lw["ssd_conv_b"], lw["ssd_alog"], lw["ssd_dtb"], lw["ssd_dd"],
                        st_ssd, n_batch, seq, want_state)
    y_rw, n_rwkv = _rwkv_chunked(p16, p32, lw["rwkv"], st_rwkv, n_batch, seq, want_state)
    y_ml, n_c, n_nm = _mlstm(p16, p32, lw["ml_gbias"], lw["ml_norm"], st_c, st_nm, n_batch, seq, want_state)
    x = _merge(y_ret, y_ssd, y_rw, y_ml, p16, lw["w_branch"], lw["w_out"], x, mod, lw["ssd_norm"], tm, seq)
    x = _ffn(x, mod, lw["norm2"], lw["ffn_up"], lw["ffn_conv_w"], lw["ffn_conv_b"], lw["ffn_down"], final_w,
             tm, seq, final)
    return x, (n_ret, n_ssd, n_rwkv, n_c, n_nm)


def kernel(x_prompt, x_sample, state_ret, state_ssd, state_rwkv, state_mlstm_c, state_mlstm_n, state_mlstm_m, c, c_ctx, ada_w, ada_b, norm1, norm2, w_in, ret_log_rate, ssd_conv_w, ssd_conv_b, ssd_A_log, ssd_dt_bias, ssd_D, ssd_norm, rwkv_conv_w, rwkv_w0, rwkv_w_up, rwkv_a0, rwkv_a_up, rwkv_g_up, rwkv_k_k, rwkv_k_a, rwkv_r_k, rwkv_ln_w, rwkv_ln_b, ml_i_bias, ml_f_bias, ml_norm, w_branch, w_out, ffn_up, ffn_conv_w, ffn_conv_b, ffn_down, final_norm):
    nb_c, seq_c, _ = x_prompt.shape
    nb_l, seq_l, _ = x_sample.shape
    assert seq_c % 256 == 0 and seq_l % 256 == 0 and nb_l * RWKV_H * 2 % LANES == 0

    cvec = jnp.zeros((16, D_MODEL), F32).at[0].set(c_ctx).at[1:1 + nb_l].set(c)
    mod_all = _modulation(cvec, ada_w, ada_b).reshape(DEPTH, 16, 6, D_MODEL)
    mod_all = jnp.pad(mod_all, ((0, 0), (0, 0), (0, 2), (0, 0)))
    rope = _rope_tables(seq_l)
    main_perm, small_perm, conv_perm = _main_perm(), _small_perm(), _ssd_conv_perm()
    row = lambda a: a.reshape(1, -1).astype(F32)
    final_w = row(final_norm)

    xp = x_prompt.reshape(nb_c * seq_c, D_MODEL)
    xs = x_sample.reshape(nb_l * seq_l, D_MODEL)
    new_states = []
    for l in range(DEPTH):
        lw = dict(
            norm1=row(norm1[l]), norm2=row(norm2[l]),
            w_main=_take_cols(w_in[l], main_perm).astype(BF16),
            w_small=_take_cols(w_in[l], small_perm).astype(BF16),
            ret_lg=-jnp.exp(ret_log_rate[l].astype(F32)),
            ssd_conv_w=_take_cols(ssd_conv_w[l], conv_perm), ssd_conv_b=_take_cols(row(ssd_conv_b[l]), conv_perm),
            ssd_alog=_lanes16(ssd_A_log[l]), ssd_dtb=_lanes16(ssd_dt_bias[l]),
            ssd_dd=jnp.repeat(ssd_D[l].astype(F32), SSD_P, axis=1), ssd_norm=row(ssd_norm[l]),
            rwkv=dict(conv_w=rwkv_conv_w[l], w0=rwkv_w0[l], w_up=_pad_rows(rwkv_w_up[l], 0),
                      a0=rwkv_a0[l], a_up=_pad_rows(rwkv_a_up[l], RWKV_W_LORA),
                      k_k=row(rwkv_k_k[l]), k_a=row(rwkv_k_a[l]), r_k=row(rwkv_r_k[l]),
                      g_up=rwkv_g_up[l].astype(BF16), ln_w=row(rwkv_ln_w[l]), ln_b=row(rwkv_ln_b[l])),
            ml_gbias=jnp.zeros((1, LANES), F32).at[0, 16:32].set(
                jnp.stack([ml_i_bias[l], ml_f_bias[l]], axis=1).reshape(-1)),
            ml_norm=row(ml_norm[l]),
            w_branch=w_branch[l].astype(BF16), w_out=w_out[l].astype(BF16),
            ffn_up=ffn_up[l].astype(BF16), ffn_conv_w=ffn_conv_w[l], ffn_conv_b=row(ffn_conv_b[l]),
            ffn_down=ffn_down[l].astype(BF16),
        )
        final = l == DEPTH - 1
        mod_c = mod_all[l, 0:1]
        mod_l = mod_all[l, 1:1 + nb_l]
        xp, st = _layer(xp, mod_c, lw, (None,) * 5, None, nb_c, seq_c, True, final, final_w)
        new_states.append(st)
        lat_states = (state_ret[:, l], _ssd_pack_state(state_ssd[:, l]), _rwkv_pack_state(state_rwkv[:, l]),
                      state_mlstm_c[:, l],
                      _ml_pack_nm(state_mlstm_n[:, l], state_mlstm_m[:, l]))
        xs, _ = _layer(xs, mod_l, lw, lat_states, rope, nb_l, seq_l, False, final, final_w)

    new_ret = jnp.stack([s[0] for s in new_states], axis=1)
    new_ssd = jnp.stack([_ssd_unpack_state(s[1]) for s in new_states], axis=1)
    new_rwkv = jnp.stack([_rwkv_unpack_state(s[2]) for s in new_states], axis=1)
    new_c = jnp.stack([s[3] for s in new_states], axis=1)
    nm = [_ml_unpack_nm(s[4]) for s in new_states]
    new_n = jnp.stack([a for a, _ in nm], axis=1)
    new_m = jnp.stack([b for _, b in nm], axis=1)
    return (xp.reshape(nb_c, seq_c, D_MODEL), xs.reshape(nb_l, seq_l, D_MODEL),
            new_ret, new_ssd, new_rwkv, new_c, new_n, new_m)
```

```python
import functools
import math

import jax
import jax.numpy as jnp
import numpy as np
from jax import lax
from jax.experimental import pallas as pl
from jax.experimental.pallas import tpu as pltpu

F32 = jnp.float32
BF16 = jnp.bfloat16
HI = lax.Precision.HIGHEST

D_MODEL = 1024
DEPTH = 2
GRID_W = 64
CHUNK = 64
N_BRANCH = 4
MIX_W = 512
RET_H, RET_DK, RET_DV = 4, 64, 128
SSD_H, SSD_P, SSD_N, SSD_G = 8, 64, 64, 2
RWKV_H, RWKV_D = 8, 64
RWKV_W_LORA, RWKV_A_LORA, RWKV_G_LORA = 64, 64, 128
ML_H, ML_DK, ML_DV = 4, 64, 128
FFN_DIM = 2816
ROPE_BASE = 10000.0
EPS = 1e-6
GN_EPS = 1e-5
RWKV_GN_EPS = 64e-5

LANES = 128
VMEM_LIMIT = 56 * 1024 * 1024
SWEEP_UNROLL = 1
CHUNKS_PER_STEP = 4
RWKV_CHUNKS_PER_STEP = 8

_O_RQ, _O_RK, _O_RV, _O_RG = 0, 256, 512, 1024
_O_SZ, _O_SX, _O_SB, _O_SC, _O_SDT = 1536, 2048, 2560, 2688, 2816
_O_WRKV, _O_WWD, _O_WAD, _O_WGD = 2832, 4368, 4432, 4496
_O_MQ, _O_MK, _O_MV, _O_MO, _O_MIF, _O_MG = 4624, 4880, 5136, 5648, 6160, 6176

M_GATE, M_RQ, M_RK, M_RV, M_RG = 0, 4096, 4352, 4608, 5120
M_SZ, M_RWKV = 5632, 6144
M_MQ, M_MK, M_MV, M_MO = 7680, 7936, 8192, 8704
M_SX, M_SB, M_SC = 9216, 9728, 9984
N_MAIN = 10240
S_WGD, S_WWD, S_WAD, S_DT, S_MIF = 0, 128, 192, 256, 272
N_SMALL = 384


def _main_perm():
    idx = np.full((N_MAIN,), -1, np.int64)

    def put(dst, src, n):
        idx[dst:dst + n] = np.arange(src, src + n)

    put(M_GATE, _O_MG, 4096)
    put(M_RQ, _O_RQ, 256); put(M_RK, _O_RK, 256); put(M_RV, _O_RV, 512); put(M_RG, _O_RG, 512)
    put(M_SZ, _O_SZ, 512); put(M_RWKV, _O_WRKV, 1536)
    put(M_MQ, _O_MQ, 256); put(M_MK, _O_MK, 256); put(M_MV, _O_MV, 512); put(M_MO, _O_MO, 512)
    put(M_SX, _O_SX, 512)
    for g in range(SSD_G):
        put(M_SB + g * LANES, _O_SB + g * SSD_N, SSD_N)
        put(M_SC + g * LANES, _O_SC + g * SSD_N, SSD_N)
    return idx


def _small_perm():
    idx = np.full((N_SMALL,), -1, np.int64)
    idx[S_WGD:S_WGD + 128] = np.arange(_O_WGD, _O_WGD + 128)
    idx[S_WWD:S_WWD + 64] = np.arange(_O_WWD, _O_WWD + 64)
    idx[S_WAD:S_WAD + 64] = np.arange(_O_WAD, _O_WAD + 64)
    idx[S_DT:S_DT + 16] = np.arange(_O_SDT, _O_SDT + 16)
    idx[S_MIF:S_MIF + 16] = np.arange(_O_MIF, _O_MIF + 16)
    return idx


def _ssd_conv_perm():
    idx = np.full((1024,), -1, np.int64)
    idx[0:512] = np.arange(0, 512)
    for g in range(SSD_G):
        idx[512 + g * LANES:512 + g * LANES + SSD_N] = np.arange(512 + g * SSD_N, 512 + (g + 1) * SSD_N)
        idx[768 + g * LANES:768 + g * LANES + SSD_N] = np.arange(640 + g * SSD_N, 640 + (g + 1) * SSD_N)
    return idx


def _take_cols(a, idx):
    safe = np.where(idx < 0, 0, idx)
    out = jnp.take(a, jnp.asarray(safe, jnp.int32), axis=-1)
    return jnp.where(jnp.asarray(idx >= 0), out, 0).astype(a.dtype)


def _params(sem, vmem=VMEM_LIMIT):
    return pltpu.CompilerParams(dimension_semantics=sem, vmem_limit_bytes=vmem)


def _tdot(a, b, **kw):
    return lax.dot_general(a, b, (((0,), (0,)), ((), ())), preferred_element_type=F32, **kw)


def _ntdot(a, b, **kw):
    return lax.dot_general(a, b, (((1,), (1,)), ((), ())), preferred_element_type=F32, **kw)


def _dot(a, b, **kw):
    return jnp.dot(a, b, preferred_element_type=F32, **kw)


def _silu(x):
    return x * jax.nn.sigmoid(x)


def _softplus(x):
    return jnp.maximum(x, 0.0) + jnp.log1p(jnp.exp(-jnp.abs(x)))


def _iota(shape, dim):
    return lax.broadcasted_iota(jnp.int32, shape, dim)


def _mod_body(c_ref, w_ref, b_ref, o_ref):
    c = c_ref[...]
    o_ref[...] = _dot(_silu(c).astype(BF16), w_ref[...].astype(BF16)) + b_ref[...]


def _modulation(cvec, ada_w, ada_b):
    L = ada_w.shape[0]
    tn = 512
    return pl.pallas_call(
        _mod_body,
        grid=(L, 6 * D_MODEL // tn),
        in_specs=[pl.BlockSpec((16, D_MODEL), lambda l, j: (0, 0)),
                  pl.BlockSpec((None, D_MODEL, tn), lambda l, j: (l, 0, j)),
                  pl.BlockSpec((None, 1, tn), lambda l, j: (l, 0, j))],
        out_specs=pl.BlockSpec((None, 16, tn), lambda l, j: (l, 0, j)),
        out_shape=jax.ShapeDtypeStruct((L, 16, 6 * D_MODEL), F32),
        compiler_params=_params(("parallel", "parallel")),
        name="modulation",
    )(cvec, ada_w, ada_b.reshape(L, 1, 6 * D_MODEL))


def _rms_mod(x, nw, shift, scale):
    ms = jnp.mean(x * x, axis=-1, keepdims=True)
    return (x * lax.rsqrt(ms + EPS) * nw) * (1.0 + scale) + shift


def _in_proj_body(x_ref, mod_ref, nw_ref, w_ref, o_ref, h_ref):
    @pl.when(pl.program_id(1) == 0)
    def _():
        h = _rms_mod(x_ref[...], nw_ref[...], mod_ref[0:1, :], mod_ref[1:2, :])
        h_ref[...] = h.astype(BF16)

    o_ref[...] = _dot(h_ref[...], w_ref[...]).astype(o_ref.dtype)


def _mod_index(n_mod, tm, seq):
    if n_mod == 1:
        return lambda i: 0
    return lambda i: (i * tm) // seq


def _in_proj(x, mod, nw, w, out_dtype, tm, tn, seq, name):
    n_tok, n = x.shape[0], w.shape[1]
    mi = _mod_index(mod.shape[0], tm, seq)
    return pl.pallas_call(
        _in_proj_body,
        grid=(n_tok // tm, n // tn),
        in_specs=[pl.BlockSpec((tm, D_MODEL), lambda i, j: (i, 0)),
                  pl.BlockSpec((None, 8, D_MODEL), lambda i, j: (mi(i), 0, 0)),
                  pl.BlockSpec((1, D_MODEL), lambda i, j: (0, 0)),
                  pl.BlockSpec((D_MODEL, tn), lambda i, j: (0, j))],
        out_specs=pl.BlockSpec((tm, tn), lambda i, j: (i, j)),
        out_shape=jax.ShapeDtypeStruct((n_tok, n), out_dtype),
        scratch_shapes=[pltpu.VMEM((tm, D_MODEL), BF16)],
        compiler_params=_params(("parallel", "arbitrary")),
        name=name,
    )(x, mod, nw, w)


def _pair_masks():
    lane = _iota((1, LANES), 1)
    return lane < 64, lane


def _head_norm(y, eps):
    mu = jnp.mean(y, axis=-1, keepdims=True)
    d = y - mu
    var = jnp.mean(d * d, axis=-1, keepdims=True)
    return d * lax.rsqrt(var + eps)


def _blockdiag_rows(v, width):
    col = _iota((1, 2 * width), 1)
    left = col < width
    return jnp.concatenate([jnp.where(left, v, 0), jnp.where(left, 0, v)], axis=0)


def _tri_masks():
    ii = _iota((CHUNK, LANES), 0)
    jj = _iota((CHUNK, LANES), 1) % CHUNK
    return jj <= ii, jj >= ii, ii - jj


def _cumsum_mats():
    r = _iota((2 * CHUNK, CHUNK), 0)
    t = _iota((2 * CHUNK, CHUNK), 1)
    pre = (r < CHUNK) & (t <= r)
    suf = (r >= CHUNK) & (t >= r - CHUNK)
    return jnp.where(pre | suf, 1.0, 0.0).astype(F32)


def _split3(x):
    hi = x.astype(BF16)
    r1 = x - hi.astype(F32)
    mid = r1.astype(BF16)
    lo = (r1 - mid.astype(F32)).astype(BF16)
    return hi, mid, lo


def _sel_dot(sel, x):
    s = sel.astype(BF16)
    hi, mid, lo = _split3(x)
    return _dot(s, hi) + _dot(s, mid) + _dot(s, lo)


def _dot_sel(x, sel):
    s = sel.astype(BF16)
    hi, mid, lo = _split3(x)
    return _dot(hi, s) + _dot(mid, s) + _dot(lo, s)


def _sel_ntdot(sel, x):
    s = sel.astype(BF16)
    hi, mid, lo = _split3(x)
    return _ntdot(s, hi) + _ntdot(s, mid) + _ntdot(s, lo)


def _cumsum_mats_blk(rb):
    r = _iota((2 * rb, rb), 0)
    t = _iota((2 * rb, rb), 1)
    rr = jnp.where(r < rb, r, r - rb)
    same = jnp.where((rr // CHUNK) == (t // CHUNK), 1.0, 0.0)
    keep = jnp.where(r < rb, jnp.where(t <= rr, 1.0, 0.0), jnp.where(t >= rr, 1.0, 0.0))
    return (same * keep).astype(F32)


def _pair_select(lane0s):
    k = len(lane0s)
    l = _iota((LANES, k * LANES), 0)
    c = _iota((LANES, k * LANES), 1)
    tgt = jnp.where((c % LANES) >= 64, 1, 0)
    for q in range(k):
        tgt = tgt + jnp.where(c // LANES == q, lane0s[q], 0)
    return jnp.where(l == tgt, 1.0, 0.0).astype(F32)


def _pair_pick(lane0s):
    r = _iota((8, LANES), 0)
    base = jnp.full((8, LANES), -8, jnp.int32)
    for q in range(len(lane0s)):
        base = jnp.where(r == q, lane0s[q], base)
    off = _iota((8, LANES), 1) - base
    return jnp.where((off == 0) | (off == 1), 1.0, 0.0).astype(F32)


def _even_odd_stack(x):
    even = (_iota((1, LANES), 1) % 2) == 0
    return jnp.concatenate([jnp.where(even, x, 0.0), jnp.where(even, 0.0, x)], axis=0)


def _lane_expand(x, lane0):
    l = _iota((LANES, LANES), 0)
    c = _iota((LANES, LANES), 1)
    sel = jnp.where(l == lane0 + jnp.where(c >= 64, 1, 0), 1.0, 0.0).astype(F32)
    return _dot(x, sel, precision=HI)


def _lane_rows(x, lane0):
    lane = _iota((1, LANES), 1)
    even = (lane % 2) == 0
    stack = jnp.concatenate([jnp.where(even, x, 0.0), jnp.where(even, 0.0, x)], axis=0)
    off = _iota((8, LANES), 1) - lane0
    pick = jnp.where((off == 0) | (off == 1), 1.0, 0.0).astype(F32)
    return _ntdot(pick, stack, precision=HI)[0:1, :]


def _ret_body(*refs, rope, has_s0, want_state, seq):
    it = iter(refs)
    lg_ref, q_ref, k_ref, v_ref, g_ref = (next(it) for _ in range(5))
    cos_ref = sin_ref = s0_ref = sn_ref = None
    if rope:
        cos_ref, sin_ref = next(it), next(it)
    if has_s0:
        s0_ref = next(it)
    y_ref = next(it)
    if want_state:
        sn_ref = next(it)
    qs, ks, sb_all = next(it), next(it), next(it)

    C = CHUNK
    n = seq // C
    hp = pl.program_id(1)
    m0, lane = _pair_masks()
    lgf0, lgf1 = lg_ref[0, 2 * hp], lg_ref[0, 2 * hp + 1]
    lgb0, lgb1 = lg_ref[1, 2 * hp], lg_ref[1, 2 * hp + 1]
    lgf = jnp.where(m0, lgf0, lgf1)
    lgb = jnp.where(m0, lgb0, lgb1)
    pos = _iota((C, 1), 0).astype(F32)
    e_qf = jnp.exp(lgf * (pos + 1.0))
    e_qb = jnp.exp(lgb * (C - pos))
    e_kf = jnp.exp(lgf * (C - 1.0 - pos))
    e_kb = jnp.exp(lgb * pos)
    row0 = _iota((LANES, 1), 0) < 64
    dec_f = jnp.exp(jnp.where(row0, lgf0, lgf1) * C)
    dec_b = jnp.exp(jnp.where(row0, lgb0, lgb1) * C)
    bd = (_iota((LANES, 2 * LANES), 0) < 64) == (_iota((LANES, 2 * LANES), 1) < LANES)
    _, _, diff = _tri_masks()
    dif = diff.astype(F32)
    dcomb = jnp.where(diff > 0, jnp.exp(lgf * jnp.maximum(dif, 0.0)),
                      jnp.where(diff < 0, jnp.exp(lgb * jnp.maximum(-dif, 0.0)), 2.0))

    rb = min(seq, 512)
    m32 = (lane % 64) < 32

    def swap(x):
        return jnp.where(m32, pltpu.roll(x, 96, 1), pltpu.roll(x, 32, 1))

    def prep(i, carry):
        r0 = pl.multiple_of(i * rb, rb)
        q = q_ref[pl.ds(r0, rb), :].astype(F32)
        k = k_ref[pl.ds(r0, rb), :].astype(F32)
        if rope:
            cs = cos_ref[pl.ds(r0, rb), :]
            sn = sin_ref[pl.ds(r0, rb), :]
            q = q * cs + swap(q) * sn
            k = k * cs + swap(k) * sn
        qs[pl.ds(r0, rb), :] = (q * RET_DK ** -0.5).astype(BF16)
        ks[pl.ds(r0, rb), :] = k.astype(BF16)
        return carry

    lax.fori_loop(0, seq // rb, prep, 0)

    def bd_state(d):
        if not has_s0:
            return jnp.zeros((LANES, 2 * LANES), F32)
        z = jnp.zeros((RET_DK, RET_DV), F32)
        return jnp.concatenate([jnp.concatenate([s0_ref[d, 0], z], axis=1),
                                jnp.concatenate([z, s0_ref[d, 1]], axis=1)], axis=0)

    nch = CHUNKS_PER_STEP

    def bsweep(i, sb):
        kvs = []
        for q in range(nch):
            c = n - 1 - (i * nch + q)
            rows = pl.ds(pl.multiple_of(c * C, C), C)
            kb = (ks[rows, :].astype(F32) * e_kb).astype(BF16)
            kvs.append((c, _tdot(kb, v_ref[rows, :])))
        for c, kv in kvs:
            sb_all[c] = sb.astype(BF16)
            sb = dec_b * sb + jnp.where(bd, kv, 0.0)
        return sb

    sb_fin = lax.fori_loop(0, n // nch, bsweep, bd_state(1))

    def fsweep(i, sf):
        jobs = []
        for q in range(nch):
            c = i * nch + q
            rows = pl.ds(pl.multiple_of(c * C, C), C)
            qc, kc, vc = qs[rows, :], ks[rows, :], v_ref[rows, :]
            k2 = jnp.concatenate([jnp.where(m0, kc, 0), jnp.where(m0, 0, kc)], axis=0)
            kf = (kc.astype(F32) * e_kf).astype(BF16)
            jobs.append(dict(c=c, rows=rows, qc=qc, vc=vc, s2=_ntdot(qc, k2), kv=_tdot(kf, vc)))
        for j in jobs:
            rows, qf = j["rows"], j["qc"].astype(F32)
            lhs = jnp.concatenate([(j["s2"] * dcomb).astype(BF16), (qf * e_qf).astype(BF16),
                                   (qf * e_qb).astype(BF16)], axis=1)
            rhs = jnp.concatenate([_blockdiag_rows(j["vc"], LANES), sf.astype(BF16), sb_all[j["c"]]], axis=0)
            y = _dot(lhs, rhs)
            y = jnp.concatenate([_head_norm(y[:, :LANES], GN_EPS), _head_norm(y[:, LANES:], GN_EPS)], axis=1)
            y_ref[rows, :] = (_silu(g_ref[rows, :].astype(F32)) * y).astype(y_ref.dtype)
            sf = dec_f * sf + jnp.where(bd, j["kv"], 0.0)
        return sf

    sf_fin = lax.fori_loop(0, n // nch, fsweep, bd_state(0))

    if want_state:
        for d, s in ((0, sf_fin), (1, sb_fin)):
            sn_ref[d, 0] = s[:RET_DK, :RET_DV]
            sn_ref[d, 1] = s[RET_DK:, RET_DV:]


def _retention(lg, p16, g_rows, cos, sin, s0, n_batch, seq, want_state):
    rope = cos is not None
    has_s0 = s0 is not None
    n_tok = n_batch * seq
    in_specs = [pl.BlockSpec(memory_space=pltpu.SMEM),
                pl.BlockSpec((seq, LANES), lambda b, p: (b, M_RQ // LANES + p)),
                pl.BlockSpec((seq, LANES), lambda b, p: (b, M_RK // LANES + p)),
                pl.BlockSpec((seq, 2 * LANES), lambda b, p: (b, M_RV // 256 + p)),
                pl.BlockSpec((seq, 2 * LANES), lambda b, p: (b, M_RG // 256 + p))]
    args = [lg, p16, p16, p16, p16]
    if rope:
        in_specs += [pl.BlockSpec((seq, LANES), lambda b, p: (0, 0))] * 2
        args += [cos, sin]
    st_spec = pl.BlockSpec((None, 2, 2, RET_DK, RET_DV), lambda b, p: (b, 0, p, 0, 0))
    if has_s0:
        in_specs.append(st_spec)
        args.append(s0)
    out_specs = [pl.BlockSpec((seq, 2 * LANES), lambda b, p: (b, p))]
    out_shape = [jax.ShapeDtypeStruct((n_tok, MIX_W), BF16)]
    if want_state:
        out_specs.append(st_spec)
        out_shape.append(jax.ShapeDtypeStruct((n_batch, 2, RET_H, RET_DK, RET_DV), F32))
    res = pl.pallas_call(
        functools.partial(_ret_body, rope=rope, has_s0=has_s0, want_state=want_state, seq=seq),
        grid=(n_batch, RET_H // 2),
        in_specs=in_specs, out_specs=out_specs, out_shape=out_shape,
        scratch_shapes=[pltpu.VMEM((seq, LANES), BF16), pltpu.VMEM((seq, LANES), BF16),
                        pltpu.VMEM((seq // CHUNK, LANES, 2 * LANES), BF16)],
        compiler_params=_params(("parallel", "parallel")),
        name="retention",
    )(*args)
    return res if want_state else (res[0], None)


def _ml_chunk_gate(bcol, icol, kc, n_x, m_x, reverse):
    C = CHUNK
    b_end = bcol[0:1] if reverse else bcol[C - 1:C]
    lwe = b_end - bcol + icol
    m_new = jnp.maximum(b_end + m_x, jnp.max(lwe, axis=0, keepdims=True))
    scale = jnp.exp(b_end + m_x - m_new)
    kw = kc.astype(F32) * jnp.exp(lwe - m_new)
    n_new = scale * n_x + jnp.sum(kw, axis=0, keepdims=True)
    return kw.astype(BF16), scale, n_new, m_new


def _ml_dir_lhs(s2, qf, bcol, brow, irow, n_x, m_x, mask, m0):
    logw = jnp.where(mask, bcol - brow + irow, -jnp.inf)
    inter = bcol + m_x
    mx = jnp.where(m0, jnp.max(jnp.where(m0, logw, -jnp.inf), axis=1, keepdims=True),
                   jnp.max(jnp.where(m0, -jnp.inf, logw), axis=1, keepdims=True))
    mt = jnp.maximum(inter, mx)
    att = s2 * jnp.exp(logw - mt)
    a_in = jnp.exp(inter - mt)
    qn = qf * n_x
    rs = jnp.where(m0, jnp.sum(jnp.where(m0, att, 0.0), axis=1, keepdims=True),
                   jnp.sum(jnp.where(m0, 0.0, att), axis=1, keepdims=True))
    qs = jnp.where(m0, jnp.sum(jnp.where(m0, qn, 0.0), axis=1, keepdims=True),
                   jnp.sum(jnp.where(m0, 0.0, qn), axis=1, keepdims=True))
    den = rs + a_in * qs
    sc = 1.0 / jnp.maximum(jnp.abs(den), jnp.exp(-mt))
    return (att * sc).astype(BF16), (qf * (a_in * sc)).astype(BF16)


def _ml_body(*refs, has_s0, want_state, seq):
    it = iter(refs)
    q_ref, k_ref, v_ref, o_ref, gt_ref, gb_ref, nw_ref = (next(it) for _ in range(7))
    s0_ref = nm0_ref = sn_ref = nmn_ref = None
    if has_s0:
        s0_ref, nm0_ref = next(it), next(it)
    y_ref = next(it)
    if want_state:
        sn_ref, nmn_ref = next(it), next(it)
    bcf_s, bcb_s, icf_s, icb_s, rows_s, cb_all, nmb_all = (next(it) for _ in range(7))

    C = CHUNK
    n = seq // C
    nch = CHUNKS_PER_STEP
    hp = pl.program_id(1)
    m0, _ = _pair_masks()
    row0 = _iota((LANES, 1), 0) < 64
    bd = (_iota((LANES, 2 * LANES), 0) < 64) == (_iota((LANES, 2 * LANES), 1) < LANES)
    tril, triu, _ = _tri_masks()
    cum = _cumsum_mats()
    gbias = gb_ref[...]
    lanes = [20 + 2 * hp, 28 + 2 * hp, 16 + 2 * hp, 24 + 2 * hp]
    sel = _pair_select(lanes)
    pick = _pair_pick(lanes)
    rb = min(seq, 256)

    def prep(i, carry):
        r0 = pl.multiple_of(i * rb, rb)
        rows = pl.ds(r0, rb)
        g = gt_ref[rows, :] + gbias
        lf = -_softplus(-g)
        cs2 = [_sel_dot(cum, lf[q * C:(q + 1) * C]) for q in range(rb // C)]
        pre = jnp.concatenate([c2[:C] for c2 in cs2], axis=0)
        suf = jnp.concatenate([c2[C:] for c2 in cs2], axis=0)
        e = _dot_sel(jnp.concatenate([pre, suf, g], axis=0), sel)
        bcf_s[rows, :] = e[0:rb, 0:LANES]
        bcb_s[rows, :] = e[rb:2 * rb, LANES:2 * LANES]
        icf_s[rows, :] = e[2 * rb:, 2 * LANES:3 * LANES]
        icb_s[rows, :] = e[2 * rb:, 3 * LANES:]
        for q in range(rb // C):
            st = jnp.concatenate([_even_odd_stack(cs2[q][:C]), _even_odd_stack(cs2[q][C:]),
                                  _even_odd_stack(g[q * C:(q + 1) * C])], axis=0)
            o = _sel_ntdot(pick, st)
            rows_s[i * (rb // C) + q] = jnp.concatenate(
                [o[0:1, 0:LANES], o[1:2, LANES:2 * LANES], o[2:3, 2 * LANES:], o[3:4, 2 * LANES:],
                 jnp.zeros((4, LANES), F32)], axis=0)
        return carry

    lax.fori_loop(0, seq // rb, prep, 0)

    def row_scale(scale):
        return jnp.where(row0, scale[:, 0:1], scale[:, 64:65])

    def init(d):
        if not has_s0:
            return (jnp.zeros((LANES, 2 * LANES), F32), jnp.zeros((1, LANES), F32), jnp.zeros((1, LANES), F32))
        z = jnp.zeros((ML_DK, ML_DV), F32)
        cst = jnp.concatenate([jnp.concatenate([s0_ref[d, 0], z], axis=1),
                               jnp.concatenate([z, s0_ref[d, 1]], axis=1)], axis=0)
        return cst, nm0_ref[d:d + 1, :], nm0_ref[2 + d:3 + d, :]

    def bsweep(i, carry):
        cst, n_x, m_x = carry
        terms = []
        for q in range(nch):
            c = n - 1 - (i * nch + q)
            rows = pl.ds(pl.multiple_of(c * C, C), C)
            kw, scale, n_new, m_new = _ml_chunk_gate(bcb_s[rows, :], icb_s[rows, :], k_ref[rows, :], n_x, m_x, True)
            terms.append((c, n_x, m_x, scale, _tdot(kw, v_ref[rows, :])))
            n_x, m_x = n_new, m_new
        for c, n_in, m_in, scale, kv in terms:
            cb_all[c] = cst.astype(BF16)
            nmb_all[c, 0:1, :] = n_in
            nmb_all[c, 1:2, :] = m_in
            cst = row_scale(scale) * cst + jnp.where(bd, kv, 0.0)
        return cst, n_x, m_x

    cb_fin = lax.fori_loop(0, n // nch, bsweep, init(1))

    def fsweep(i, carry):
        cst, n_x, m_x = carry
        jobs = []
        for q in range(nch):
            c = i * nch + q
            rows = pl.ds(pl.multiple_of(c * C, C), C)
            kc = k_ref[rows, :]
            vc = v_ref[rows, :]
            qf = q_ref[rows, :].astype(F32) * ML_DK ** -0.5
            k2 = jnp.concatenate([jnp.where(m0, kc, 0), jnp.where(m0, 0, kc)], axis=0)
            bcf = bcf_s[rows, :]
            kw, scale, n_new, m_new = _ml_chunk_gate(bcf, icf_s[rows, :], kc, n_x, m_x, False)
            jobs.append(dict(c=c, rows=rows, qf=qf, vc=vc, bcf=bcf, n_in=n_x, m_in=m_x, scale=scale,
                             s2=_ntdot(qf.astype(BF16), k2), kv=_tdot(kw, vc)))
            n_x, m_x = n_new, m_new
        for j in jobs:
            c, rows, s2, qf = j["c"], j["rows"], j["s2"], j["qf"]
            rw = rows_s[c]
            af, qaf = _ml_dir_lhs(s2, qf, j["bcf"], rw[0:1], rw[2:3], j["n_in"], j["m_in"], tril, m0)
            ab, qab = _ml_dir_lhs(s2, qf, bcb_s[rows, :], rw[1:2], rw[3:4], nmb_all[c, 0:1, :], nmb_all[c, 1:2, :],
                                  triu, m0)
            vbd = _blockdiag_rows(j["vc"], LANES)
            lhs = jnp.concatenate([af, qaf, ab, qab], axis=1)
            rhs = jnp.concatenate([vbd, cst.astype(BF16), vbd, cb_all[c]], axis=0)
            y = _dot(lhs, rhs)
            y = jnp.concatenate([_head_norm(y[:, :LANES], GN_EPS), _head_norm(y[:, LANES:], GN_EPS)], axis=1)
            y = jax.nn.sigmoid(o_ref[rows, :].astype(F32)) * (y * nw_ref[...])
            y_ref[rows, :] = y.astype(y_ref.dtype)
            cst = row_scale(j["scale"]) * cst + jnp.where(bd, j["kv"], 0.0)
        return cst, n_x, m_x

    cf_fin = lax.fori_loop(0, n // nch, fsweep, init(0))

    if want_state:
        nmn_ref[...] = jnp.zeros((8, LANES), F32)
        for d, (cst, n_x, m_x) in ((0, cf_fin), (1, cb_fin)):
            sn_ref[d, 0] = cst[:ML_DK, :ML_DV]
            sn_ref[d, 1] = cst[ML_DK:, ML_DV:]
            nmn_ref[d:d + 1, :] = n_x
            nmn_ref[2 + d:3 + d, :] = m_x


def _mlstm(p16, p32, gbias, nw, s0, nm0, n_batch, seq, want_state):
    has_s0 = s0 is not None
    n_tok = n_batch * seq
    in_specs = [pl.BlockSpec((seq, LANES), lambda b, p: (b, M_MQ // LANES + p)),
                pl.BlockSpec((seq, LANES), lambda b, p: (b, M_MK // LANES + p)),
                pl.BlockSpec((seq, 2 * LANES), lambda b, p: (b, M_MV // 256 + p)),
                pl.BlockSpec((seq, 2 * LANES), lambda b, p: (b, M_MO // 256 + p)),
                pl.BlockSpec((seq, LANES), lambda b, p: (b, S_DT // LANES)),
                pl.BlockSpec((1, LANES), lambda b, p: (0, 0)),
                pl.BlockSpec((1, 2 * LANES), lambda b, p: (0, p))]
    args = [p16, p16, p16, p16, p32, gbias, nw]
    st_spec = pl.BlockSpec((None, 2, 2, ML_DK, ML_DV), lambda b, p: (b, 0, p, 0, 0))
    nm_spec = pl.BlockSpec((None, None, 8, LANES), lambda b, p: (b, p, 0, 0))
    if has_s0:
        in_specs += [st_spec, nm_spec]
        args += [s0, nm0]
    out_specs = [pl.BlockSpec((seq, 2 * LANES), lambda b, p: (b, p))]
    out_shape = [jax.ShapeDtypeStruct((n_tok, MIX_W), BF16)]
    if want_state:
        out_specs += [st_spec, nm_spec]
        out_shape += [jax.ShapeDtypeStruct((n_batch, 2, ML_H, ML_DK, ML_DV), F32),
                      jax.ShapeDtypeStruct((n_batch, ML_H // 2, 8, LANES), F32)]
    res = pl.pallas_call(
        functools.partial(_ml_body, has_s0=has_s0, want_state=want_state, seq=seq),
        grid=(n_batch, ML_H // 2),
        in_specs=in_specs, out_specs=out_specs, out_shape=out_shape,
        scratch_shapes=[pltpu.VMEM((seq, LANES), F32)] * 4
        + [pltpu.VMEM((seq // CHUNK, 8, LANES), F32), pltpu.VMEM((seq // CHUNK, LANES, 2 * LANES), BF16),
           pltpu.VMEM((seq // CHUNK, 8, LANES), F32)],
        compiler_params=_params(("parallel", "parallel")),
        name="mlstm",
    )(*args)
    return res if want_state else (res[0], None, None)


def _ml_pack_nm(st_n, st_m):
    B = st_n.shape[0]
    n = st_n.reshape(B, 2, ML_H // 2, 2 * ML_DK).transpose(0, 2, 1, 3)
    m = jnp.repeat(st_m.reshape(B, 2, ML_H // 2, 2), ML_DK, axis=-1).transpose(0, 2, 1, 3)
    return jnp.concatenate([n, m, jnp.zeros((B, ML_H // 2, 4, LANES), F32)], axis=2)


def _ml_unpack_nm(nm):
    B = nm.shape[0]
    n = nm[:, :, 0:2, :].transpose(0, 2, 1, 3).reshape(B, 2, ML_H, ML_DK)
    m = nm[:, :, 2:4, :].transpose(0, 2, 1, 3).reshape(B, 2, ML_H, ML_DK)[..., 0]
    return n, m


def _conv3_rows(ref, r0, rb, seq, w, bias):
    x = ref[pl.ds(r0, rb), :].astype(F32)
    prev = ref[pl.ds(pl.multiple_of(jnp.maximum(r0 - 16, 0), 16), 16), :].astype(F32)[15:16]
    nxt = ref[pl.ds(pl.multiple_of(jnp.minimum(r0 + rb, seq - 16), 16), 16), :].astype(F32)[0:1]
    prev = jnp.where(r0 == 0, 0.0, prev)
    nxt = jnp.where(r0 + rb == seq, 0.0, nxt)
    row = _iota((rb, 1), 0)
    x_dn = jnp.where(row == 0, prev, pltpu.roll(x, 1, 0))
    x_up = jnp.where(row == rb - 1, nxt, pltpu.roll(x, rb - 1, 0))
    y = w[0:1] * x_dn + w[1:2] * x + w[2:3] * x_up
    return y if bias is None else y + bias


def _ssd_body(*refs, has_s0, want_state, seq):
    it = iter(refs)
    (x_ref, b_ref, c_ref, z_ref, dt_ref, wx_ref, wb_ref, wc_ref, bx_ref, bb_ref, bc_ref,
     alog_ref, dtb_ref, dd_ref) = (next(it) for _ in range(14))
    s0_ref = sn_ref = None
    if has_s0:
        s0_ref = next(it)
    y_ref = next(it)
    if want_state:
        sn_ref = next(it)
    xs, bs, cs_, ccf_s, ccb_s, dcf_s, dcb_s, rows_s, sb_all = (next(it) for _ in range(9))

    C = CHUNK
    n = seq // C
    nch = CHUNKS_PER_STEP
    p = pl.program_id(1)
    tril, triu, _ = _tri_masks()
    a_lane = -jnp.exp(alog_ref[...])
    dsum = dd_ref[0:1, :] + dd_ref[1:2, :]
    rb = min(seq, 256)
    cum = _cumsum_mats()
    sel = _pair_select([2 * p, 8 + 2 * p])
    pick = _pair_pick([2 * p, 8 + 2 * p])

    def prep(i, carry):
        r0 = pl.multiple_of(i * rb, rb)
        rows = pl.ds(r0, rb)
        xs[rows, :] = _silu(_conv3_rows(x_ref, r0, rb, seq, wx_ref[...], bx_ref[...]))
        bs[rows, :] = _silu(_conv3_rows(b_ref, r0, rb, seq, wb_ref[...], bb_ref[...])).astype(BF16)
        cs_[rows, :] = _silu(_conv3_rows(c_ref, r0, rb, seq, wc_ref[...], bc_ref[...])).astype(BF16)
        dt = _softplus(dt_ref[rows, :] + dtb_ref[...])
        lw = dt * a_lane
        cs2 = [_sel_dot(cum, lw[q * C:(q + 1) * C]) for q in range(rb // C)]
        pre = jnp.concatenate([c2[:C] for c2 in cs2], axis=0)
        suf = jnp.concatenate([c2[C:] for c2 in cs2], axis=0)
        e = _dot_sel(jnp.concatenate([pre, suf, dt], axis=0), sel)
        ccf_s[rows, :] = e[0:rb, :LANES]
        ccb_s[rows, :] = e[rb:2 * rb, LANES:]
        dcf_s[rows, :] = e[2 * rb:, :LANES]
        dcb_s[rows, :] = e[2 * rb:, LANES:]
        for q in range(rb // C):
            st = jnp.concatenate([_even_odd_stack(cs2[q][:C]), _even_odd_stack(cs2[q][C:]),
                                  _even_odd_stack(dt[q * C:(q + 1) * C])], axis=0)
            o = _sel_ntdot(pick, st)
            rows_s[i * (rb // C) + q] = jnp.concatenate(
                [o[0:1, 0:LANES], o[1:2, LANES:2 * LANES], o[0:1, 2 * LANES:], o[1:2, 2 * LANES:],
                 jnp.zeros((4, LANES), F32)], axis=0)
        return carry

    lax.fori_loop(0, seq // rb, prep, 0)

    def state0(d):
        if not has_s0:
            return jnp.zeros((LANES, LANES), F32)
        return jnp.concatenate([s0_ref[d], jnp.zeros((SSD_N, LANES), F32)], axis=0)

    def state_update_terms(rows, ccol_s, dcol_s, rev):
        ccol = ccol_s[rows, :]
        last = ccol[0:1] if rev else ccol[C - 1:C]
        vdt = (xs[rows, :] * dcol_s[rows, :] * jnp.exp(last - ccol)).astype(BF16)
        return jnp.exp(last), _tdot(bs[rows, :], vdt)

    def bsweep(i, sb):
        terms = []
        for q in range(nch):
            c = n - 1 - (i * nch + q)
            rows = pl.ds(pl.multiple_of(c * C, C), C)
            terms.append((c,) + state_update_terms(rows, ccb_s, dcb_s, True))
        for c, dec, kv in terms:
            sb_all[c] = sb[:SSD_N].astype(BF16)
            sb = dec * sb + kv
        return sb

    sb_fin = lax.fori_loop(0, n // nch, bsweep, state0(1))

    def fsweep(i, sf):
        jobs = []
        for q in range(nch):
            c = i * nch + q
            rows = pl.ds(pl.multiple_of(c * C, C), C)
            j = dict(c=c, rows=rows, xc=xs[rows, :], bc=bs[rows, :], cc=cs_[rows, :],
                     ccf=ccf_s[rows, :], ccb=ccb_s[rows, :])
            rw = rows_s[c]
            j["m"] = (jnp.where(tril, jnp.exp(jnp.minimum(j["ccf"] - rw[0:1], 0.0)) * rw[2:3], 0.0)
                      + jnp.where(triu, jnp.exp(jnp.minimum(j["ccb"] - rw[1:2], 0.0)) * rw[3:4], 0.0))
            j["s2"] = _ntdot(j["cc"], jnp.concatenate([j["bc"], j["bc"]], axis=0))
            jobs.append(j)
        for j in jobs:
            att = (j["s2"] * j["m"]).astype(BF16)
            sbc = jnp.concatenate([sb_all[j["c"]], jnp.zeros((SSD_N, LANES), BF16)], axis=0)
            j["yi"] = _dot(att, _blockdiag_rows(j["xc"], 64).astype(BF16))
            j["ysb"] = _dot(j["cc"], sbc)
            j["dec"], j["kv"] = state_update_terms(j["rows"], ccf_s, dcf_s, False)
        for j in jobs:
            y = (j["yi"] + _dot(j["cc"], sf.astype(BF16)) * jnp.exp(j["ccf"]) + j["ysb"] * jnp.exp(j["ccb"])
                 + dsum * j["xc"])
            y_ref[j["rows"], :] = (y * _silu(z_ref[j["rows"], :].astype(F32))).astype(y_ref.dtype)
            sf = j["dec"] * sf + j["kv"]
        return sf

    sf_fin = lax.fori_loop(0, n // nch, fsweep, state0(0))
    if want_state:
        sn_ref[0] = sf_fin[:SSD_N]
        sn_ref[1] = sb_fin[:SSD_N]


def _ssd(p16, p32, conv_w, conv_b, alog, dtb, dd, s0, n_batch, seq, want_state):
    has_s0 = s0 is not None
    n_tok = n_batch * seq
    col = lambda off: (lambda b, p: (b, off // LANES + p))
    grp = lambda off: (lambda b, p: (b, off // LANES + p // 2))
    wcol = lambda off: (lambda b, p: (0, off + p))
    wgrp = lambda off: (lambda b, p: (0, off + p // 2))
    in_specs = [pl.BlockSpec((seq, LANES), col(M_SX)), pl.BlockSpec((seq, LANES), grp(M_SB)),
                pl.BlockSpec((seq, LANES), grp(M_SC)), pl.BlockSpec((seq, LANES), col(M_SZ)),
                pl.BlockSpec((seq, LANES), lambda b, p: (b, S_DT // LANES)),
                pl.BlockSpec((3, LANES), wcol(0)), pl.BlockSpec((3, LANES), wgrp(4)), pl.BlockSpec((3, LANES), wgrp(6)),
                pl.BlockSpec((1, LANES), wcol(0)), pl.BlockSpec((1, LANES), wgrp(4)), pl.BlockSpec((1, LANES), wgrp(6)),
                pl.BlockSpec((1, LANES), lambda b, p: (0, 0)), pl.BlockSpec((1, LANES), lambda b, p: (0, 0)),
                pl.BlockSpec((2, LANES), wcol(0))]
    args = [p16, p16, p16, p16, p32, conv_w, conv_w, conv_w, conv_b, conv_b, conv_b, alog, dtb, dd]
    st_spec = pl.BlockSpec((None, 2, None, SSD_N, LANES), lambda b, p: (b, 0, p, 0, 0))
    if has_s0:
        in_specs.append(st_spec)
        args.append(s0)
    out_specs = [pl.BlockSpec((seq, LANES), lambda b, p: (b, p))]
    out_shape = [jax.ShapeDtypeStruct((n_tok, MIX_W), BF16)]
    if want_state:
        out_specs.append(st_spec)
        out_shape.append(jax.ShapeDtypeStruct((n_batch, 2, SSD_H // 2, SSD_N, LANES), F32))
    res = pl.pallas_call(
        functools.partial(_ssd_body, has_s0=has_s0, want_state=want_state, seq=seq),
        grid=(n_batch, SSD_H // 2),
        in_specs=in_specs, out_specs=out_specs, out_shape=out_shape,
        scratch_shapes=[pltpu.VMEM((seq, LANES), F32), pltpu.VMEM((seq, LANES), BF16), pltpu.VMEM((seq, LANES), BF16)]
        + [pltpu.VMEM((seq, LANES), F32)] * 4
        + [pltpu.VMEM((seq // CHUNK, 8, LANES), F32), pltpu.VMEM((seq // CHUNK, SSD_N, LANES), BF16)],
        compiler_params=_params(("parallel", "parallel")),
        name="ssd",
    )(*args)
    return res if want_state else (res[0], None)


def _ssd_pack_state(st):
    B = st.shape[0]
    return st.reshape(B, 2, SSD_H // 2, 2, SSD_N, SSD_P).transpose(0, 1, 2, 4, 3, 5).reshape(B, 2, SSD_H // 2, SSD_N, 2 * SSD_P)


def _ssd_unpack_state(st):
    B = st.shape[0]
    return st.reshape(B, 2, SSD_H // 2, SSD_N, 2, SSD_P).transpose(0, 1, 2, 4, 3, 5).reshape(B, 2, SSD_H, SSD_N, SSD_P)


def _seg_ones():
    r = _iota((MIX_W, MIX_W), 0) // RWKV_D
    c = _iota((MIX_W, MIX_W), 1) // RWKV_D
    return jnp.where(r == c, 1.0, 0.0).astype(BF16)


def _seg_sum(x, ones):
    hi = x.astype(BF16)
    lo = (x - hi.astype(F32)).astype(BF16)
    return _dot(hi, ones) + _dot(lo, ones)


def _rwkv_prep_body(x_ref, xp_ref, xn_ref, sm_ref, cw_ref, w0_ref, wup_ref, a0_ref, aup_ref,
                    kk_ref, ka_ref, rk_ref, r_o, v_o, a_o, w_o, kd_o, b_o, bon_o, *, rb):
    j = pl.program_id(1)
    nj = pl.num_programs(1)
    prev = jnp.where(j == 0, 0.0, xp_ref[...].astype(F32)[15:16])
    nxt = jnp.where(j == nj - 1, 0.0, xn_ref[...].astype(F32)[0:1])
    row = _iota((rb, 1), 0)
    cw = cw_ref[...]

    def conv(c0):
        x = x_ref[:, c0:c0 + MIX_W].astype(F32)
        x_dn = jnp.where(row == 0, prev[:, c0:c0 + MIX_W], pltpu.roll(x, 1, 0))
        x_up = jnp.where(row == rb - 1, nxt[:, c0:c0 + MIX_W], pltpu.roll(x, rb - 1, 0))
        w = cw[:, c0:c0 + MIX_W]
        return w[0:1] * x_dn + w[1:2] * x + w[2:3] * x_up

    r_ = conv(0)
    k_ = conv(MIX_W)
    v_ = conv(2 * MIX_W)
    ones = _seg_ones()
    kk = k_ * kk_ref[...]
    nrm = jnp.sqrt(_seg_sum(kk * kk, ones))
    kk = kk / jnp.maximum(nrm, 1e-12)
    lora = sm_ref[...]
    w_low = jnp.tanh(lora).astype(BF16)
    a_low = lora.astype(BF16)
    kd_sum = jnp.zeros_like(k_)
    for d in range(2):
        w_log = -_softplus(-(w0_ref[d:d + 1, :] + _dot(w_low, wup_ref[d]))) - 0.5
        w_o[d] = jnp.exp(-jnp.exp(w_log))
        iclr = jax.nn.sigmoid(a0_ref[d:d + 1, :] + _dot(a_low, aup_ref[d]))
        kd = k_ * (1.0 + (iclr - 1.0) * ka_ref[...])
        kd_o[d] = kd
        b_o[d] = kk * iclr
        kd_sum = kd_sum + kd
    r_o[...] = r_
    v_o[...] = v_
    a_o[...] = -kk
    bon_o[...] = _seg_sum(r_ * kd_sum * rk_ref[...], ones) * v_


def _rwkv_prep(p16, p32, conv_w, w0, wup, a0, aup, k_k, k_a, r_k, n_batch, seq):
    n_tok = n_batch * seq
    rb = 256
    nj = seq // rb
    nb16 = seq // 16
    one = lambda b, j: (0, 0)
    tok = jax.ShapeDtypeStruct((n_tok, MIX_W), F32)
    tok2 = jax.ShapeDtypeStruct((2, n_tok, MIX_W), F32)
    o1 = pl.BlockSpec((rb, MIX_W), lambda b, j: (b * nj + j, 0))
    o2 = pl.BlockSpec((2, rb, MIX_W), lambda b, j: (0, b * nj + j, 0))
    return pl.pallas_call(
        functools.partial(_rwkv_prep_body, rb=rb),
        grid=(n_batch, nj),
        in_specs=[pl.BlockSpec((rb, 3 * MIX_W), lambda b, j: (b * nj + j, M_RWKV // (3 * MIX_W))),
                  pl.BlockSpec((16, 3 * MIX_W),
                               lambda b, j: (b * nb16 + jnp.maximum(j * (rb // 16) - 1, 0), M_RWKV // (3 * MIX_W))),
                  pl.BlockSpec((16, 3 * MIX_W),
                               lambda b, j: (b * nb16 + jnp.minimum((j + 1) * (rb // 16), nb16 - 1), M_RWKV // (3 * MIX_W))),
                  pl.BlockSpec((rb, LANES), lambda b, j: (b * nj + j, S_WWD // LANES)),
                  pl.BlockSpec((3, 3 * MIX_W), one),
                  pl.BlockSpec((2, MIX_W), one), pl.BlockSpec((2, LANES, MIX_W), lambda b, j: (0, 0, 0)),
                  pl.BlockSpec((2, MIX_W), one), pl.BlockSpec((2, LANES, MIX_W), lambda b, j: (0, 0, 0)),
                  pl.BlockSpec((1, MIX_W), one), pl.BlockSpec((1, MIX_W), one), pl.BlockSpec((1, MIX_W), one)],
        out_specs=[o1, o1, o1, o2, o2, o2, o1],
        out_shape=[tok, tok, tok, tok2, tok2, tok2, tok],
        compiler_params=_params(("parallel", "parallel")),
        name="rwkv_prep",
    )(p16, p16, p16, p32, conv_w, w0, wup, a0, aup, k_k, k_a, r_k)


def _rwkv_scan_body(*refs, has_s0, want_state, tb):
    it = iter(refs)
    r_ref, w_ref, k_ref, v_ref, a_ref, b_ref = (next(it) for _ in range(6))
    s0_ref = sn_ref = None
    if has_s0:
        s0_ref = next(it)
    y_ref = next(it)
    if want_state:
        sn_ref = next(it)
    st = next(it)
    i = pl.program_id(1)

    @pl.when(i == 0)
    def _():
        st[...] = s0_ref[...] if has_s0 else jnp.zeros(st.shape, F32)

    def step(t, carry):
        sa = jnp.zeros((RWKV_D, LANES), F32)
        for kx in range(RWKV_D):
            sa = sa + st[kx] * a_ref[t, pl.ds(kx, 1), :]
        vt = v_ref[t]
        y = jnp.zeros((RWKV_D, LANES), F32)
        for kx in range(RWKV_D):
            row = pl.ds(kx, 1)
            sn = st[kx] * w_ref[t, row, :] + sa * b_ref[t, row, :] + vt * k_ref[t, row, :]
            st[kx] = sn
            y = y + sn * r_ref[t, row, :]
        y_ref[t] = y
        return carry

    lax.fori_loop(0, tb, step, 0)

    if want_state:
        @pl.when(i == pl.num_programs(1) - 1)
        def _():
            sn_ref[...] = st[...]


def _rwkv_scan(r, w, k, v, a, b, s0, want_state):
    seq, _, n_lane = r.shape
    tb = 32
    has_s0 = s0 is not None
    blk = pl.BlockSpec((tb, RWKV_D, LANES), lambda g, i: (i, 0, g))
    sblk = pl.BlockSpec((RWKV_D, RWKV_D, LANES), lambda g, i: (0, 0, g))
    in_specs = [blk] * 6
    args = [r, w, k, v, a, b]
    if has_s0:
        in_specs.append(sblk)
        args.append(s0)
    out_specs = [blk]
    out_shape = [jax.ShapeDtypeStruct((seq, RWKV_D, n_lane), F32)]
    if want_state:
        out_specs.append(sblk)
        out_shape.append(jax.ShapeDtypeStruct((RWKV_D, RWKV_D, n_lane), F32))
    res = pl.pallas_call(
        functools.partial(_rwkv_scan_body, has_s0=has_s0, want_state=want_state, tb=tb),
        grid=(n_lane // LANES, seq // tb),
        in_specs=in_specs, out_specs=out_specs, out_shape=out_shape,
        scratch_shapes=[pltpu.VMEM((RWKV_D, RWKV_D, LANES), F32)],
        compiler_params=_params(("parallel", "arbitrary")),
        name="rwkv_scan",
    )(*args)
    return res if want_state else (res[0], None)


def _rwkv_post_body(yf_ref, yb_ref, bon_ref, g_ref, gup_ref, lw_ref, lb_ref, o_ref):
    ones = _seg_ones()
    y = yf_ref[...] + yb_ref[...]
    mu = _seg_sum(y, ones) * (1.0 / RWKV_D)
    d = y - mu
    var = _seg_sum(d * d, ones) * (1.0 / RWKV_D)
    out = d * lax.rsqrt(var + RWKV_GN_EPS) * lw_ref[...] + lb_ref[...] + bon_ref[...]
    g = _dot(jax.nn.sigmoid(g_ref[...]).astype(BF16), gup_ref[...])
    o_ref[...] = (out * g).astype(o_ref.dtype)


def _rwkv_post(yf, yb, bonus, p32, g_up, ln_w, ln_b, tm):
    n_tok = yf.shape[0]
    tokb = pl.BlockSpec((tm, MIX_W), lambda i: (i, 0))
    one = lambda i: (0, 0)
    return pl.pallas_call(
        _rwkv_post_body,
        grid=(n_tok // tm,),
        in_specs=[tokb, tokb, tokb, pl.BlockSpec((tm, LANES), lambda i: (i, S_WGD // LANES)),
                  pl.BlockSpec((LANES, MIX_W), one), pl.BlockSpec((1, MIX_W), one), pl.BlockSpec((1, MIX_W), one)],
        out_specs=tokb,
        out_shape=jax.ShapeDtypeStruct((n_tok, MIX_W), BF16),
        compiler_params=_params(("parallel",)),
        name="rwkv_post",
    )(yf, yb, bonus, p32, g_up, ln_w, ln_b)


def _to_scan_layout(x, n_batch, seq):
    return x.reshape(n_batch, seq, RWKV_H, RWKV_D).transpose(1, 3, 0, 2).reshape(seq, RWKV_D, n_batch * RWKV_H)


def _from_scan_layout(y, n_batch, seq):
    return y.reshape(seq, RWKV_D, n_batch, RWKV_H).transpose(2, 0, 3, 1).reshape(n_batch * seq, MIX_W)


def _rwkv(p16, p32, lw, s0, n_batch, seq, want_state, tm):
    r_, v_, a_, w_, kd_, b_, bonus = _rwkv_prep(p16, p32, lw["conv_w"], lw["w0"], lw["w_up"], lw["a0"], lw["a_up"],
                                                lw["k_k"], lw["k_a"], lw["r_k"], n_batch, seq)
    tl = lambda x: _to_scan_layout(x, n_batch, seq)
    both = lambda x: jnp.concatenate([tl(x), tl(x)[::-1]], axis=-1)
    per_dir = lambda x: jnp.concatenate([tl(x[0]), tl(x[1])[::-1]], axis=-1)
    s0t = None
    if s0 is not None:
        s0t = s0.transpose(4, 3, 1, 0, 2).reshape(RWKV_D, RWKV_D, 2 * n_batch * RWKV_H)
    y, sn = _rwkv_scan(both(r_), per_dir(w_), per_dir(kd_), both(v_), both(a_), per_dir(b_), s0t, want_state)
    nl = n_batch * RWKV_H
    yf = _from_scan_layout(y[:, :, :nl], n_batch, seq)
    yb = _from_scan_layout(y[::-1, :, nl:], n_batch, seq)
    out = _rwkv_post(yf, yb, bonus, p32, lw["g_up"], lw["ln_w"], lw["ln_b"], tm)
    if want_state:
        sn = sn.reshape(RWKV_D, RWKV_D, 2, n_batch, RWKV_H).transpose(3, 2, 4, 1, 0)
    return out, sn


def _seg_ones_pair():
    r = _iota((LANES, LANES), 0) // RWKV_D
    c = _iota((LANES, LANES), 1) // RWKV_D
    return jnp.where(r == c, 1.0, 0.0).astype(BF16)


def _b16(x):
    return x.astype(BF16)


def _rwkv_chunk_stage_a(job, cum, refs):
    d, r0, rev = job["d"], job["r0"], job["rev"]
    rs, vs, as_, lws, kds, bs = refs
    C = CHUNK
    lw = lws[d, pl.ds(r0, C), :]
    cs = _sel_dot(cum, lw)
    g = cs[C:] if rev else cs[:C]
    g_end = g[0:1] if rev else g[C - 1:C]
    r, a = rs[pl.ds(r0, C), :], as_[pl.ds(r0, C), :]
    kd, b = kds[d, pl.ds(r0, C), :], bs[d, pl.ds(r0, C), :]
    e_g, e_ng, e_end = jnp.exp(g), jnp.exp(-g), jnp.exp(g_end - g)
    job.update(v=vs[pl.ds(r0, C), :], at=a * jnp.exp(g - lw), rt=r * e_g, bt=b * e_ng, kt=kd * e_ng,
               bh=b * e_end, kh=kd * e_end, dec=jnp.exp(g_end))


def _rwkv_chunk_stage_b(job, m0, strict_f, incl_f, strict_b, incl_b):
    C = CHUNK
    strict, incl = (strict_b, incl_b) if job["rev"] else (strict_f, incl_f)
    lhs = _b16(jnp.concatenate([job["at"], job["rt"]], axis=0))
    bt, kt = job["bt"], job["kt"]
    rhs = _b16(jnp.concatenate([jnp.where(m0, bt, 0.0), jnp.where(m0, 0.0, bt),
                                jnp.where(m0, kt, 0.0), jnp.where(m0, 0.0, kt)], axis=0))
    a4 = _ntdot(lhs, rhs)
    job.update(lab=jnp.where(strict, a4[:C, :LANES], 0.0), lak=jnp.where(strict, a4[:C, LANES:], 0.0),
               mrb=jnp.where(incl, a4[C:, :LANES], 0.0), mrk=jnp.where(incl, a4[C:, LANES:], 0.0))
    lbd = _blockdiag_rows(job["lab"], 64)
    eye = _iota((LANES, LANES), 0) == _iota((LANES, LANES), 1)
    job.update(tm=jnp.where(eye, 1.0, 0.0) + lbd, lbd=lbd)


def _rwkv_chunk_inverse(jobs):
    pws = [_dot(_b16(j["lbd"]), _b16(j["lbd"])) for j in jobs]
    levels = int(math.log2(CHUNK)) - 1
    for lvl in range(levels):
        for j, pw in zip(jobs, pws):
            j["tm"] = j["tm"] + _dot(_b16(j["tm"]), _b16(pw))
        if lvl < levels - 1:
            pws = [_dot(_b16(pw), _b16(pw)) for pw in pws]


def _rwkv_chunk_stage_c(job):
    C = CHUNK
    v = job["v"]
    at_rows = _blockdiag_rows(job["at"], 64)
    v_rows = _blockdiag_rows(v, 64)
    wv = _dot(_b16(job["lak"]), _b16(v_rows))
    job.update(at_rows=at_rows, v_rows=v_rows, wv_rows=_blockdiag_rows(wv, 64))


def _rwkv_chunk_stage_d(job):
    C = CHUNK
    tx = _dot(_b16(job["tm"]), _b16(jnp.concatenate([job["at_rows"], job["wv_rows"]], axis=1)))
    job.update(ah_rows=tx[:, :LANES], uv_rows=tx[:, LANES:],
               ah=tx[:C, :LANES] + tx[C:, :LANES], uv=tx[:C, LANES:] + tx[C:, LANES:])


def _rwkv_chunk_stage_e(job, bd128, eye):
    C = CHUNK
    z = jnp.zeros((2 * C, LANES), F32)
    rhs_m = jnp.concatenate([jnp.concatenate([job["ah_rows"], job["uv_rows"]], axis=1),
                             jnp.concatenate([z, job["v_rows"]], axis=1)], axis=0)
    my = _dot(_b16(jnp.concatenate([job["mrb"], job["mrk"]], axis=1)), _b16(rhs_m))
    zc = jnp.zeros((C, LANES), F32)
    rhs_g = jnp.concatenate([jnp.concatenate([job["ah"], job["uv"]], axis=1),
                             jnp.concatenate([zc, job["v"]], axis=1)], axis=0)
    gh = _tdot(_b16(jnp.concatenate([job["bh"], job["kh"]], axis=0)), _b16(rhs_g))
    job.update(rh=job["rt"] + my[:, :LANES], yv=my[:, LANES:],
               gm=jnp.where(bd128, gh[:, :LANES], 0.0) + jnp.where(eye, job["dec"], 0.0),
               hm=jnp.where(bd128, gh[:, LANES:], 0.0))


def _rwkv_chunk_stage_f(job, p):
    C = CHUNK
    out = _dot(_b16(jnp.concatenate([job["rh"], job["gm"]], axis=0)), _b16(p))
    return out[:C] + job["yv"], out[C:] + job["hm"]


def _rwkv_body(*refs, has_s0, want_state, seq):
    it = iter(refs)
    (r_ref, k_ref, v_ref, lora_ref, g_ref, cwr_ref, cwk_ref, cwv_ref, w0_ref, wup_ref, a0_ref, aup_ref,
     kkw_ref, kaw_ref, rkw_ref, gup_ref, lnw_ref, lnb_ref) = (next(it) for _ in range(18))
    s0_ref = sn_ref = None
    if has_s0:
        s0_ref = next(it)
    y_ref = next(it)
    if want_state:
        sn_ref = next(it)
    rs, vs, as_, lws, kds, bs, yf, yb = (next(it) for _ in range(8))

    C = CHUNK
    n = seq // C
    m0, _ = _pair_masks()
    ones = _seg_ones_pair()
    rb = min(seq, 256)

    def prep(i, carry):
        r0 = pl.multiple_of(i * rb, rb)
        r_ = _conv3_rows(r_ref, r0, rb, seq, cwr_ref[...], None)
        k_ = _conv3_rows(k_ref, r0, rb, seq, cwk_ref[...], None)
        rs[pl.ds(r0, rb), :] = r_
        vs[pl.ds(r0, rb), :] = _conv3_rows(v_ref, r0, rb, seq, cwv_ref[...], None)
        kk = k_ * kkw_ref[...]
        kk = kk / jnp.maximum(jnp.sqrt(_seg_sum(kk * kk, ones)), 1e-12)
        as_[pl.ds(r0, rb), :] = -kk
        lora = lora_ref[pl.ds(r0, rb), :]
        w_low, a_low = _b16(jnp.tanh(lora)), _b16(lora)
        for d in range(2):
            w_log = -_softplus(-(w0_ref[d:d + 1, :] + _dot(w_low, wup_ref[d]))) - 0.5
            lws[d, pl.ds(r0, rb), :] = -jnp.exp(w_log)
            iclr = jax.nn.sigmoid(a0_ref[d:d + 1, :] + _dot(a_low, aup_ref[d]))
            kds[d, pl.ds(r0, rb), :] = k_ * (1.0 + (iclr - 1.0) * kaw_ref[...])
            bs[d, pl.ds(r0, rb), :] = kk * iclr
        return carry

    lax.fori_loop(0, seq // rb, prep, 0)

    cum = _cumsum_mats()
    ii = _iota((C, LANES), 0)
    jj = _iota((C, LANES), 1) % C
    strict_f, incl_f, strict_b, incl_b = jj < ii, jj <= ii, jj > ii, jj >= ii
    bd128 = (_iota((LANES, LANES), 0) < 64) == (_iota((LANES, LANES), 1) < 64)
    eye = _iota((LANES, LANES), 0) == _iota((LANES, LANES), 1)
    scr = (rs, vs, as_, lws, kds, bs)

    nch = min(RWKV_CHUNKS_PER_STEP, n)

    def sweep(i, carry):
        pf, pb = carry
        jobs = []
        for q in range(nch):
            jobs.append(dict(d=0, rev=False, r0=pl.multiple_of((i * nch + q) * C, C)))
            jobs.append(dict(d=1, rev=True, r0=pl.multiple_of((n - 1 - i * nch - q) * C, C)))
        for j in jobs:
            _rwkv_chunk_stage_a(j, cum, scr)
        for j in jobs:
            _rwkv_chunk_stage_b(j, m0, strict_f, incl_f, strict_b, incl_b)
        _rwkv_chunk_inverse(jobs)
        for j in jobs:
            _rwkv_chunk_stage_c(j)
        for j in jobs:
            _rwkv_chunk_stage_d(j)
        for j in jobs:
            _rwkv_chunk_stage_e(j, bd128, eye)
        for q in range(nch):
            y_f, pf = _rwkv_chunk_stage_f(jobs[2 * q], pf)
            y_b, pb = _rwkv_chunk_stage_f(jobs[2 * q + 1], pb)
            yf[pl.ds(jobs[2 * q]["r0"], C), :] = y_f
            yb[pl.ds(jobs[2 * q + 1]["r0"], C), :] = y_b
        return pf, pb

    p0 = (s0_ref[0], s0_ref[1]) if has_s0 else (jnp.zeros((LANES, LANES), F32),) * 2
    pf, pb = lax.fori_loop(0, n // nch, sweep, p0)
    if want_state:
        sn_ref[0] = pf
        sn_ref[1] = pb

    def post(i, carry):
        r0 = pl.multiple_of(i * rb, rb)
        rows = pl.ds(r0, rb)
        y = yf[rows, :] + yb[rows, :]
        mu = _seg_sum(y, ones) * (1.0 / RWKV_D)
        dv = y - mu
        var = _seg_sum(dv * dv, ones) * (1.0 / RWKV_D)
        bonus = _seg_sum(rs[rows, :] * (kds[0, rows, :] + kds[1, rows, :]) * rkw_ref[...], ones) * vs[rows, :]
        out = dv * lax.rsqrt(var + RWKV_GN_EPS) * lnw_ref[...] + lnb_ref[...] + bonus
        g = _dot(_b16(jax.nn.sigmoid(g_ref[rows, :])), gup_ref[...])
        y_ref[rows, :] = (out * g).astype(y_ref.dtype)
        return carry

    lax.fori_loop(0, seq // rb, post, 0)


def _rwkv_chunked(p16, p32, lw, s0, n_batch, seq, want_state):
    has_s0 = s0 is not None
    n_tok = n_batch * seq
    base = M_RWKV // LANES
    tok = lambda off: pl.BlockSpec((seq, LANES), lambda b, p: (b, off + p))
    wcol = lambda rows: pl.BlockSpec((rows, LANES), lambda b, p: (0, p))
    in_specs = [tok(base), tok(base + 4), tok(base + 8),
                pl.BlockSpec((seq, LANES), lambda b, p: (b, S_WWD // LANES)),
                pl.BlockSpec((seq, LANES), lambda b, p: (b, S_WGD // LANES)),
                pl.BlockSpec((3, LANES), lambda b, p: (0, p)),
                pl.BlockSpec((3, LANES), lambda b, p: (0, 4 + p)),
                pl.BlockSpec((3, LANES), lambda b, p: (0, 8 + p)),
                wcol(2), pl.BlockSpec((2, LANES, LANES), lambda b, p: (0, 0, p)),
                wcol(2), pl.BlockSpec((2, LANES, LANES), lambda b, p: (0, 0, p)),
                wcol(1), wcol(1), wcol(1),
                pl.BlockSpec((LANES, LANES), lambda b, p: (0, p)), wcol(1), wcol(1)]
    args = [p16, p16, p16, p32, p32, lw["conv_w"], lw["conv_w"], lw["conv_w"], lw["w0"], lw["w_up"], lw["a0"],
            lw["a_up"], lw["k_k"], lw["k_a"], lw["r_k"], lw["g_up"], lw["ln_w"], lw["ln_b"]]
    st_spec = pl.BlockSpec((None, 2, None, LANES, LANES), lambda b, p: (b, 0, p, 0, 0))
    if has_s0:
        in_specs.append(st_spec)
        args.append(s0)
    out_specs = [pl.BlockSpec((seq, LANES), lambda b, p: (b, p))]
    out_shape = [jax.ShapeDtypeStruct((n_tok, MIX_W), BF16)]
    if want_state:
        out_specs.append(st_spec)
        out_shape.append(jax.ShapeDtypeStruct((n_batch, 2, RWKV_H // 2, LANES, LANES), F32))
    tokf = pltpu.VMEM((seq, LANES), F32)
    tok2 = pltpu.VMEM((2, seq, LANES), F32)
    res = pl.pallas_call(
        functools.partial(_rwkv_body, has_s0=has_s0, want_state=want_state, seq=seq),
        grid=(n_batch, RWKV_H // 2),
        in_specs=in_specs, out_specs=out_specs, out_shape=out_shape,
        scratch_shapes=[tokf, tokf, tokf, tok2, tok2, tok2, tokf, tokf],
        compiler_params=_params(("parallel", "parallel")),
        name="rwkv",
    )(*args)
    return res if want_state else (res[0], None)


def _rwkv_pack_state(st):
    B = st.shape[0]
    p = jnp.swapaxes(st, -1, -2).reshape(B, 2, RWKV_H // 2, 2, RWKV_D, RWKV_D)
    z = jnp.zeros_like(p[:, :, :, 0])
    return jnp.concatenate([jnp.concatenate([p[:, :, :, 0], z], axis=-1),
                            jnp.concatenate([z, p[:, :, :, 1]], axis=-1)], axis=-2)


def _rwkv_unpack_state(pbd):
    h0 = pbd[:, :, :, :RWKV_D, :RWKV_D]
    h1 = pbd[:, :, :, RWKV_D:, RWKV_D:]
    p = jnp.stack([h0, h1], axis=3)
    B = p.shape[0]
    return jnp.swapaxes(p, -1, -2).reshape(B, 2, RWKV_H, RWKV_D, RWKV_D)


def _merge_body(yr_ref, ys_ref, yw_ref, ym_ref, g_ref, wb_ref, wo_ref, x_ref, mod_ref, sn_ref, o_ref):
    ys = ys_ref[...].astype(F32)
    ys = ys * lax.rsqrt(jnp.mean(ys * ys, axis=-1, keepdims=True) + EPS) * sn_ref[...]
    branches = (yr_ref[...], ys.astype(BF16), yw_ref[...], ym_ref[...])
    merged = None
    for i, br in enumerate(branches):
        gate = jax.nn.sigmoid(g_ref[:, i * D_MODEL:(i + 1) * D_MODEL].astype(F32))
        term = gate * _dot(br, wb_ref[i])
        merged = term if merged is None else merged + term
    out = _dot(merged.astype(BF16), wo_ref[...])
    o_ref[...] = x_ref[...] + mod_ref[2:3, :] * out


def _merge(y_ret, y_ssd, y_rw, y_ml, p16, w_branch, w_out, x, mod, ssd_norm, tm, seq):
    n_tok = x.shape[0]
    mi = _mod_index(mod.shape[0], tm, seq)
    yb = pl.BlockSpec((tm, MIX_W), lambda i: (i, 0))
    xb = pl.BlockSpec((tm, D_MODEL), lambda i: (i, 0))
    return pl.pallas_call(
        _merge_body,
        grid=(n_tok // tm,),
        in_specs=[yb, yb, yb, yb,
                  pl.BlockSpec((tm, N_BRANCH * D_MODEL), lambda i: (i, M_GATE)),
                  pl.BlockSpec((N_BRANCH, MIX_W, D_MODEL), lambda i: (0, 0, 0)),
                  pl.BlockSpec((D_MODEL, D_MODEL), lambda i: (0, 0)),
                  xb,
                  pl.BlockSpec((None, 8, D_MODEL), lambda i: (mi(i), 0, 0)),
                  pl.BlockSpec((1, MIX_W), lambda i: (0, 0))],
        out_specs=xb,
        out_shape=jax.ShapeDtypeStruct((n_tok, D_MODEL), F32),
        compiler_params=_params(("parallel",)),
        name="merge",
    )(y_ret, y_ssd, y_rw, y_ml, p16, w_branch, w_out, x, mod, ssd_norm)


def _ffn_body(x_ref, xp_ref, xn_ref, mod_ref, nw_ref, uv_ref, ug_ref, cwv_ref, cwg_ref, cbv_ref, cbg_ref,
              dn_ref, fw_ref, o_ref, h_ref, *, tm, seq, final):
    i = pl.program_id(0)
    f = pl.program_id(1)
    shift, scale, gate = mod_ref[3:4, :], mod_ref[4:5, :], mod_ref[5:6, :]

    @pl.when(f == 0)
    def _():
        nw = nw_ref[...]
        h_ref[0:tm, :] = _rms_mod(x_ref[...], nw, shift, scale).astype(BF16)
        h_ref[tm:tm + 8, :] = _rms_mod(xp_ref[...], nw, shift, scale).astype(BF16)
        h_ref[tm + 8:tm + 16, :] = _rms_mod(xn_ref[...], nw, shift, scale).astype(BF16)
        o_ref[...] = jnp.zeros(o_ref.shape, F32)

    row = _iota((tm, 1), 0)
    pos = (i * tm + row) % seq
    first, last = pos == 0, pos == seq - 1
    h = h_ref[...]

    def conv(u_ref, cw_ref, cb_ref):
        u = _dot(h, u_ref[...])
        um = u[0:tm]
        u_dn = jnp.where(row == 0, u[tm + 7:tm + 8], pltpu.roll(um, 1, 0))
        u_up = jnp.where(row == tm - 1, u[tm + 8:tm + 9], pltpu.roll(um, tm - 1, 0))
        u_dn = jnp.where(first, 0.0, u_dn)
        u_up = jnp.where(last, 0.0, u_up)
        cw = cw_ref[...]
        return cw[0:1] * u_dn + cw[1:2] * um + cw[2:3] * u_up + cb_ref[...]

    act = conv(uv_ref, cwv_ref, cbv_ref) * _silu(conv(ug_ref, cwg_ref, cbg_ref))
    o_ref[...] += _dot(act.astype(BF16), dn_ref[...])

    @pl.when(f == pl.num_programs(1) - 1)
    def _():
        xn = x_ref[...] + gate * o_ref[...]
        if final:
            xn = xn * lax.rsqrt(jnp.mean(xn * xn, axis=-1, keepdims=True) + EPS) * fw_ref[...]
        o_ref[...] = xn


def _ffn(x, mod, nw, up, conv_w, conv_b, down, final_w, tm, seq, final):
    n_tok = x.shape[0]
    fc = FFN_DIM // 2
    nf = FFN_DIM // fc
    mi = _mod_index(mod.shape[0], tm, seq)
    n8 = n_tok // 8
    xb = pl.BlockSpec((tm, D_MODEL), lambda i, f: (i, 0))
    one = lambda i, f: (0, 0)
    return pl.pallas_call(
        functools.partial(_ffn_body, tm=tm, seq=seq, final=final),
        grid=(n_tok // tm, nf),
        in_specs=[xb,
                  pl.BlockSpec((8, D_MODEL), lambda i, f: (jnp.maximum(i * (tm // 8) - 1, 0), 0)),
                  pl.BlockSpec((8, D_MODEL), lambda i, f: (jnp.minimum((i + 1) * (tm // 8), n8 - 1), 0)),
                  pl.BlockSpec((None, 8, D_MODEL), lambda i, f: (mi(i), 0, 0)),
                  pl.BlockSpec((1, D_MODEL), one),
                  pl.BlockSpec((D_MODEL, fc), lambda i, f: (0, f)),
                  pl.BlockSpec((D_MODEL, fc), lambda i, f: (0, nf + f)),
                  pl.BlockSpec((3, fc), lambda i, f: (0, f)),
                  pl.BlockSpec((3, fc), lambda i, f: (0, nf + f)),
                  pl.BlockSpec((1, fc), lambda i, f: (0, f)),
                  pl.BlockSpec((1, fc), lambda i, f: (0, nf + f)),
                  pl.BlockSpec((fc, D_MODEL), lambda i, f: (f, 0)),
                  pl.BlockSpec((1, D_MODEL), one)],
        out_specs=xb,
        out_shape=jax.ShapeDtypeStruct((n_tok, D_MODEL), F32),
        scratch_shapes=[pltpu.VMEM((tm + 16, D_MODEL), BF16)],
        compiler_params=_params(("parallel", "arbitrary")),
        name="conv_ffn",
    )(x, x, x, mod, nw, up, up, conv_w, conv_w, conv_b, conv_b, down, final_w)


def _rope_tables(seq):
    rows = seq // GRID_W
    rr, cc = jnp.meshgrid(jnp.arange(rows), jnp.arange(GRID_W), indexing='ij')
    nf = RET_DK // 4
    inv = ROPE_BASE ** (-jnp.arange(nf, dtype=F32) / nf)
    ang = jnp.concatenate([rr.reshape(-1, 1) * inv, cc.reshape(-1, 1) * inv], axis=-1)
    cos, sin = jnp.cos(ang), jnp.sin(ang)
    return (jnp.tile(jnp.concatenate([cos, cos], axis=-1), (1, 2)),
            jnp.tile(jnp.concatenate([-sin, sin], axis=-1), (1, 2)))


def _lanes16(a):
    return jnp.zeros((1, LANES), F32).at[0, :16].set(a.reshape(-1).astype(F32))


def _pad_rows(w, lo):
    return jnp.zeros((2, LANES, MIX_W), F32).at[:, lo:lo + w.shape[1]].set(w).astype(BF16)


def _layer(x, mod, lw, states, rope, n_batch, seq, want_state, final, final_w):
    tm_p = 1024
    tm = 512
    p16 = _in_proj(x, mod, lw["norm1"], lw["w_main"], BF16, tm_p, 2048, seq, "in_proj_main")
    p32 = _in_proj(x, mod, lw["norm1"], lw["w_small"], F32, tm_p, N_SMALL, seq, "in_proj_small")
    st_ret, st_ssd, st_rwkv, st_c, st_nm = states
    cos, sin = rope if rope is not None else (None, None)
    y_ret, n_ret = _retention(lw["ret_lg"], p16, None, cos, sin, st_ret, n_batch, seq, want_state)
    y_ssd, n_ssd = _ssd(p16, p32, lw["ssd_conv_w"], lw["ssd_conv_b"], lw["ssd_alog"], lw["ssd_dtb"], lw["ssd_dd"],
                        st_ssd, n_batch, seq, want_state)
    y_rw, n_rwkv = _rwkv_chunked(p16, p32, lw["rwkv"], st_rwkv, n_batch, seq, want_state)
    y_ml, n_c, n_nm = _mlstm(p16, p32, lw["ml_gbias"], lw["ml_norm"], st_c, st_nm, n_batch, seq, want_state)
    x = _merge(y_ret, y_ssd, y_rw, y_ml, p16, lw["w_branch"], lw["w_out"], x, mod, lw["ssd_norm"], tm, seq)
    x = _ffn(x, mod, lw["norm2"], lw["ffn_up"], lw["ffn_conv_w"], lw["ffn_conv_b"], lw["ffn_down"], final_w,
             tm, seq, final)
    return x, (n_ret, n_ssd, n_rwkv, n_c, n_nm)


def kernel(x_prompt, x_sample, state_ret, state_ssd, state_rwkv, state_mlstm_c, state_mlstm_n, state_mlstm_m, c, c_ctx, ada_w, ada_b, norm1, norm2, w_in, ret_log_rate, ssd_conv_w, ssd_conv_b, ssd_A_log, ssd_dt_bias, ssd_D, ssd_norm, rwkv_conv_w, rwkv_w0, rwkv_w_up, rwkv_a0, rwkv_a_up, rwkv_g_up, rwkv_k_k, rwkv_k_a, rwkv_r_k, rwkv_ln_w, rwkv_ln_b, ml_i_bias, ml_f_bias, ml_norm, w_branch, w_out, ffn_up, ffn_conv_w, ffn_conv_b, ffn_down, final_norm):
    nb_c, seq_c, _ = x_prompt.shape
    nb_l, seq_l, _ = x_sample.shape
    assert seq_c % 256 == 0 and seq_l % 256 == 0 and nb_l * RWKV_H * 2 % LANES == 0

    cvec = jnp.zeros((16, D_MODEL), F32).at[0].set(c_ctx).at[1:1 + nb_l].set(c)
    mod_all = _modulation(cvec, ada_w, ada_b).reshape(DEPTH, 16, 6, D_MODEL)
    mod_all = jnp.pad(mod_all, ((0, 0), (0, 0), (0, 2), (0, 0)))
    rope = _rope_tables(seq_l)
    main_perm, small_perm, conv_perm = _main_perm(), _small_perm(), _ssd_conv_perm()
    row = lambda a: a.reshape(1, -1).astype(F32)
    final_w = row(final_norm)

    xp = x_prompt.reshape(nb_c * seq_c, D_MODEL)
    xs = x_sample.reshape(nb_l * seq_l, D_MODEL)
    new_states = []
    for l in range(DEPTH):
        lw = dict(
            norm1=row(norm1[l]), norm2=row(norm2[l]),
            w_main=_take_cols(w_in[l], main_perm).astype(BF16),
            w_small=_take_cols(w_in[l], small_perm).astype(BF16),
            ret_lg=-jnp.exp(ret_log_rate[l].astype(F32)),
            ssd_conv_w=_take_cols(ssd_conv_w[l], conv_perm), ssd_conv_b=_take_cols(row(ssd_conv_b[l]), conv_perm),
            ssd_alog=_lanes16(ssd_A_log[l]), ssd_dtb=_lanes16(ssd_dt_bias[l]),
            ssd_dd=jnp.repeat(ssd_D[l].astype(F32), SSD_P, axis=1), ssd_norm=row(ssd_norm[l]),
            rwkv=dict(conv_w=rwkv_conv_w[l], w0=rwkv_w0[l], w_up=_pad_rows(rwkv_w_up[l], 0),
                      a0=rwkv_a0[l], a_up=_pad_rows(rwkv_a_up[l], RWKV_W_LORA),
                      k_k=row(rwkv_k_k[l]), k_a=row(rwkv_k_a[l]), r_k=row(rwkv_r_k[l]),
                      g_up=rwkv_g_up[l].astype(BF16), ln_w=row(rwkv_ln_w[l]), ln_b=row(rwkv_ln_b[l])),
            ml_gbias=jnp.zeros((1, LANES), F32).at[0, 16:32].set(
                jnp.stack([ml_i_bias[l], ml_f_bias[l]], axis=1).reshape(-1)),
            ml_norm=row(ml_norm[l]),
            w_branch=w_branch[l].astype(BF16), w_out=w_out[l].astype(BF16),
            ffn_up=ffn_up[l].astype(BF16), ffn_conv_w=ffn_conv_w[l], ffn_conv_b=row(ffn_conv_b[l]),
            ffn_down=ffn_down[l].astype(BF16),
        )
        final = l == DEPTH - 1
        mod_c = mod_all[l, 0:1]
        mod_l = mod_all[l, 1:1 + nb_l]
        xp, st = _layer(xp, mod_c, lw, (None,) * 5, None, nb_c, seq_c, True, final, final_w)
        new_states.append(st)
        lat_states = (state_ret[:, l], _ssd_pack_state(state_ssd[:, l]), _rwkv_pack_state(state_rwkv[:, l]),
                      state_mlstm_c[:, l],
                      _ml_pack_nm(state_mlstm_n[:, l], state_mlstm_m[:, l]))
        xs, _ = _layer(xs, mod_l, lw, lat_states, rope, nb_l, seq_l, False, final, final_w)

    new_ret = jnp.stack([s[0] for s in new_states], axis=1)
    new_ssd = jnp.stack([_ssd_unpack_state(s[1]) for s in new_states], axis=1)
    new_rwkv = jnp.stack([_rwkv_unpack_state(s[2]) for s in new_states], axis=1)
    new_c = jnp.stack([s[3] for s in new_states], axis=1)
    nm = [_ml_unpack_nm(s[4]) for s in new_states]
    new_n = jnp.stack([a for a, _ in nm], axis=1)
    new_m = jnp.stack([b for _, b in nm], axis=1)
    return (xp.reshape(nb_c, seq_c, D_MODEL), xs.reshape(nb_l, seq_l, D_MODEL),
            new_ret, new_ssd, new_rwkv, new_c, new_n, new_m)
```

```python
import functools
import math

import jax
import jax.numpy as jnp
import numpy as np
from jax import lax
from jax.experimental import pallas as pl
from jax.experimental.pallas import tpu as pltpu

F32 = jnp.float32
BF16 = jnp.bfloat16

D_MODEL = 1024
DEPTH = 2
GRID_W = 64
CHUNK = 64
N_BRANCH = 4
MIX_W = 512
RET_H, RET_DK, RET_DV = 4, 64, 128
SSD_H, SSD_P, SSD_N, SSD_G = 8, 64, 64, 2
RWKV_H, RWKV_D = 8, 64
RWKV_W_LORA, RWKV_A_LORA, RWKV_G_LORA = 64, 64, 128
ML_H, ML_DK, ML_DV = 4, 64, 128
FFN_DIM = 2816
ROPE_BASE = 10000.0
EPS = 1e-6
GN_EPS = 1e-5
RWKV_GN_EPS = 64e-5

LANES = 128
VMEM_LIMIT = 56 * 1024 * 1024
CHUNKS_PER_STEP = 4
RWKV_CHUNKS_PER_STEP = 8

_O_RQ, _O_RK, _O_RV, _O_RG = 0, 256, 512, 1024
_O_SZ, _O_SX, _O_SB, _O_SC, _O_SDT = 1536, 2048, 2560, 2688, 2816
_O_WRKV, _O_WWD, _O_WAD, _O_WGD = 2832, 4368, 4432, 4496
_O_MQ, _O_MK, _O_MV, _O_MO, _O_MIF, _O_MG = 4624, 4880, 5136, 5648, 6160, 6176

M_GATE, M_RQ, M_RK, M_RV, M_RG = 0, 4096, 4352, 4608, 5120
M_SZ, M_RWKV = 5632, 6144
M_MQ, M_MK, M_MV, M_MO = 7680, 7936, 8192, 8704
M_SX, M_SB, M_SC = 9216, 9728, 9984
N_MAIN = 10240
S_WGD, S_WWD, S_WAD, S_DT, S_MIF = 0, 128, 192, 256, 272
N_SMALL = 384


def _main_perm():
    idx = np.full((N_MAIN,), -1, np.int64)

    def put(dst, src, n):
        idx[dst:dst + n] = np.arange(src, src + n)

    put(M_GATE, _O_MG, 4096)
    put(M_RQ, _O_RQ, 256); put(M_RK, _O_RK, 256); put(M_RV, _O_RV, 512); put(M_RG, _O_RG, 512)
    put(M_SZ, _O_SZ, 512); put(M_RWKV, _O_WRKV, 1536)
    put(M_MQ, _O_MQ, 256); put(M_MK, _O_MK, 256); put(M_MV, _O_MV, 512); put(M_MO, _O_MO, 512)
    put(M_SX, _O_SX, 512)
    for g in range(SSD_G):
        put(M_SB + g * LANES, _O_SB + g * SSD_N, SSD_N)
        put(M_SC + g * LANES, _O_SC + g * SSD_N, SSD_N)
    return idx


def _small_perm():
    idx = np.full((N_SMALL,), -1, np.int64)
    idx[S_WGD:S_WGD + 128] = np.arange(_O_WGD, _O_WGD + 128)
    idx[S_WWD:S_WWD + 64] = np.arange(_O_WWD, _O_WWD + 64)
    idx[S_WAD:S_WAD + 64] = np.arange(_O_WAD, _O_WAD + 64)
    idx[S_DT:S_DT + 16] = np.arange(_O_SDT, _O_SDT + 16)
    idx[S_MIF:S_MIF + 16] = np.arange(_O_MIF, _O_MIF + 16)
    return idx


def _ssd_conv_perm():
    idx = np.full((1024,), -1, np.int64)
    idx[0:512] = np.arange(0, 512)
    for g in range(SSD_G):
        idx[512 + g * LANES:512 + g * LANES + SSD_N] = np.arange(512 + g * SSD_N, 512 + (g + 1) * SSD_N)
        idx[768 + g * LANES:768 + g * LANES + SSD_N] = np.arange(640 + g * SSD_N, 640 + (g + 1) * SSD_N)
    return idx


def _take_cols(a, idx):
    safe = np.where(idx < 0, 0, idx)
    out = jnp.take(a, jnp.asarray(safe, jnp.int32), axis=-1)
    return jnp.where(jnp.asarray(idx >= 0), out, 0).astype(a.dtype)


def _params(sem, vmem=VMEM_LIMIT):
    return pltpu.CompilerParams(dimension_semantics=sem, vmem_limit_bytes=vmem)


def _tdot(a, b, **kw):
    return lax.dot_general(a, b, (((0,), (0,)), ((), ())), preferred_element_type=F32, **kw)


def _ntdot(a, b, **kw):
    return lax.dot_general(a, b, (((1,), (1,)), ((), ())), preferred_element_type=F32, **kw)


def _dot(a, b, **kw):
    return jnp.dot(a, b, preferred_element_type=F32, **kw)


def _silu(x):
    return x * jax.nn.sigmoid(x)


def _softplus(x):
    return jnp.maximum(x, 0.0) + jnp.log1p(jnp.exp(-jnp.abs(x)))


def _iota(shape, dim):
    return lax.broadcasted_iota(jnp.int32, shape, dim)


def _mod_body(c_ref, w_ref, b_ref, o_ref):
    c = c_ref[...]
    o_ref[...] = _dot(_silu(c).astype(BF16), w_ref[...].astype(BF16)) + b_ref[...]


def _modulation(cvec, ada_w, ada_b):
    L = ada_w.shape[0]
    tn = 512
    return pl.pallas_call(
        _mod_body,
        grid=(L, 6 * D_MODEL // tn),
        in_specs=[pl.BlockSpec((16, D_MODEL), lambda l, j: (0, 0)),
                  pl.BlockSpec((None, D_MODEL, tn), lambda l, j: (l, 0, j)),
                  pl.BlockSpec((None, 1, tn), lambda l, j: (l, 0, j))],
        out_specs=pl.BlockSpec((None, 16, tn), lambda l, j: (l, 0, j)),
        out_shape=jax.ShapeDtypeStruct((L, 16, 6 * D_MODEL), F32),
        compiler_params=_params(("parallel", "parallel")),
        name="modulation",
    )(cvec, ada_w, ada_b.reshape(L, 1, 6 * D_MODEL))


def _rms_mod(x, nw, shift, scale):
    ms = jnp.mean(x * x, axis=-1, keepdims=True)
    return (x * lax.rsqrt(ms + EPS) * nw) * (1.0 + scale) + shift


def _in_proj_body(x_ref, mod_ref, nw_ref, w_ref, ws_ref, o_ref, os_ref, h_ref):
    @pl.when(pl.program_id(1) == 0)
    def _():
        h = _rms_mod(x_ref[...], nw_ref[...], mod_ref[0:1, :], mod_ref[1:2, :])
        h_ref[...] = h.astype(BF16)
        os_ref[...] = _dot(h_ref[...], ws_ref[...])

    o_ref[...] = _dot(h_ref[...], w_ref[...]).astype(o_ref.dtype)


def _mod_index(n_mod, tm, seq):
    if n_mod == 1:
        return lambda i: 0
    return lambda i: (i * tm) // seq


def _in_proj(x, mod, nw, w, w_small, tm, tn, seq):
    n_tok, n = x.shape[0], w.shape[1]
    mi = _mod_index(mod.shape[0], tm, seq)
    return pl.pallas_call(
        _in_proj_body,
        grid=(n_tok // tm, n // tn),
        in_specs=[pl.BlockSpec((tm, D_MODEL), lambda i, j: (i, 0)),
                  pl.BlockSpec((None, 8, D_MODEL), lambda i, j: (mi(i), 0, 0)),
                  pl.BlockSpec((1, D_MODEL), lambda i, j: (0, 0)),
                  pl.BlockSpec((D_MODEL, tn), lambda i, j: (0, j)),
                  pl.BlockSpec((D_MODEL, N_SMALL), lambda i, j: (0, 0))],
        out_specs=[pl.BlockSpec((tm, tn), lambda i, j: (i, j)),
                   pl.BlockSpec((tm, N_SMALL), lambda i, j: (i, 0))],
        out_shape=[jax.ShapeDtypeStruct((n_tok, n), BF16), jax.ShapeDtypeStruct((n_tok, N_SMALL), F32)],
        scratch_shapes=[pltpu.VMEM((tm, D_MODEL), BF16)],
        compiler_params=_params(("parallel", "arbitrary")),
        name="in_proj",
    )(x, mod, nw, w, w_small)


def _pair_masks():
    lane = _iota((1, LANES), 1)
    return lane < 64, lane


def _head_norm(y, eps):
    mu = jnp.mean(y, axis=-1, keepdims=True)
    d = y - mu
    var = jnp.mean(d * d, axis=-1, keepdims=True)
    return d * lax.rsqrt(var + eps)


def _blockdiag_rows(v, width):
    col = _iota((1, 2 * width), 1)
    left = col < width
    return jnp.concatenate([jnp.where(left, v, 0), jnp.where(left, 0, v)], axis=0)


def _tri_masks():
    ii = _iota((CHUNK, LANES), 0)
    jj = _iota((CHUNK, LANES), 1) % CHUNK
    return jj <= ii, jj >= ii, ii - jj


def _cumsum_mats():
    r = _iota((2 * CHUNK, CHUNK), 0)
    t = _iota((2 * CHUNK, CHUNK), 1)
    pre = (r < CHUNK) & (t <= r)
    suf = (r >= CHUNK) & (t >= r - CHUNK)
    return jnp.where(pre | suf, 1.0, 0.0).astype(F32)


def _split2(x):
    hi = x.astype(BF16)
    return hi, (x - hi.astype(F32)).astype(BF16)


def _sel_dot(sel, x):
    s = sel.astype(BF16)
    hi, lo = _split2(x)
    return _dot(s, hi) + _dot(s, lo)


def _dot_sel(x, sel):
    s = sel.astype(BF16)
    hi, lo = _split2(x)
    return _dot(hi, s) + _dot(lo, s)


def _sel_ntdot(sel, x):
    s = sel.astype(BF16)
    hi, lo = _split2(x)
    return _ntdot(s, hi) + _ntdot(s, lo)


def _seg_ones_pair():
    r = _iota((LANES, LANES), 0) // 64
    c = _iota((LANES, LANES), 1) // 64
    return jnp.where(r == c, 1.0, 0.0).astype(BF16)


def _seg_sum(x, ones):
    hi, lo = _split2(x)
    return _dot(hi, ones) + _dot(lo, ones)


def _pair_select(lane0s):
    k = len(lane0s)
    l = _iota((LANES, k * LANES), 0)
    c = _iota((LANES, k * LANES), 1)
    tgt = jnp.where((c % LANES) >= 64, 1, 0)
    for q in range(k):
        tgt = tgt + jnp.where(c // LANES == q, lane0s[q], 0)
    return jnp.where(l == tgt, 1.0, 0.0).astype(F32)


def _pair_pick(lane0s):
    r = _iota((8, LANES), 0)
    base = jnp.full((8, LANES), -8, jnp.int32)
    for q in range(len(lane0s)):
        base = jnp.where(r == q, lane0s[q], base)
    off = _iota((8, LANES), 1) - base
    return jnp.where((off == 0) | (off == 1), 1.0, 0.0).astype(F32)


def _even_odd_stack(x):
    even = (_iota((1, LANES), 1) % 2) == 0
    return jnp.concatenate([jnp.where(even, x, 0.0), jnp.where(even, 0.0, x)], axis=0)


def _ret_body(*refs, rope, has_s0, want_state, seq):
    it = iter(refs)
    lg_ref, q_ref, k_ref, v_ref, g_ref = (next(it) for _ in range(5))
    cos_ref = sin_ref = s0_ref = sn_ref = None
    if rope:
        cos_ref, sin_ref = next(it), next(it)
    if has_s0:
        s0_ref = next(it)
    y_ref = next(it)
    if want_state:
        sn_ref = next(it)
    qs, ks, sb_all = next(it), next(it), next(it)

    C = CHUNK
    n = seq // C
    hp = pl.program_id(1)
    m0, lane = _pair_masks()
    lgf0, lgf1 = lg_ref[0, 2 * hp], lg_ref[0, 2 * hp + 1]
    lgb0, lgb1 = lg_ref[1, 2 * hp], lg_ref[1, 2 * hp + 1]
    lgf = jnp.where(m0, lgf0, lgf1)
    lgb = jnp.where(m0, lgb0, lgb1)
    pos = _iota((C, 1), 0).astype(F32)
    e_qf = jnp.exp(lgf * (pos + 1.0))
    e_qb = jnp.exp(lgb * (C - pos))
    e_kf = jnp.exp(lgf * (C - 1.0 - pos))
    e_kb = jnp.exp(lgb * pos)
    row0 = _iota((LANES, 1), 0) < 64
    dec_f = jnp.exp(jnp.where(row0, lgf0, lgf1) * C)
    dec_b = jnp.exp(jnp.where(row0, lgb0, lgb1) * C)
    bd = (_iota((LANES, 2 * LANES), 0) < 64) == (_iota((LANES, 2 * LANES), 1) < LANES)
    _, _, diff = _tri_masks()
    dif = diff.astype(F32)
    dcomb = jnp.where(diff > 0, jnp.exp(lgf * jnp.maximum(dif, 0.0)),
                      jnp.where(diff < 0, jnp.exp(lgb * jnp.maximum(-dif, 0.0)), 2.0))

    rb = min(seq, 512)
    m32 = (lane % 64) < 32

    def swap(x):
        return jnp.where(m32, pltpu.roll(x, 96, 1), pltpu.roll(x, 32, 1))

    def prep(i, carry):
        r0 = pl.multiple_of(i * rb, rb)
        q = q_ref[pl.ds(r0, rb), :].astype(F32)
        k = k_ref[pl.ds(r0, rb), :].astype(F32)
        if rope:
            cs = cos_ref[pl.ds(r0, rb), :]
            sn = sin_ref[pl.ds(r0, rb), :]
            q = q * cs + swap(q) * sn
            k = k * cs + swap(k) * sn
        qs[pl.ds(r0, rb), :] = (q * RET_DK ** -0.5).astype(BF16)
        ks[pl.ds(r0, rb), :] = k.astype(BF16)
        return carry

    lax.fori_loop(0, seq // rb, prep, 0)

    def bd_state(d):
        if not has_s0:
            return jnp.zeros((LANES, 2 * LANES), F32)
        z = jnp.zeros((RET_DK, RET_DV), F32)
        return jnp.concatenate([jnp.concatenate([s0_ref[d, 0], z], axis=1),
                                jnp.concatenate([z, s0_ref[d, 1]], axis=1)], axis=0)

    nch = CHUNKS_PER_STEP

    def bsweep(i, sb):
        kvs = []
        for q in range(nch):
            c = n - 1 - (i * nch + q)
            rows = pl.ds(pl.multiple_of(c * C, C), C)
            kb = (ks[rows, :].astype(F32) * e_kb).astype(BF16)
            kvs.append((c, _tdot(kb, v_ref[rows, :])))
        for c, kv in kvs:
            sb_all[c] = sb.astype(BF16)
            sb = dec_b * sb + jnp.where(bd, kv, 0.0)
        return sb

    sb_fin = lax.fori_loop(0, n // nch, bsweep, bd_state(1))

    def fsweep(i, sf):
        jobs = []
        for q in range(nch):
            c = i * nch + q
            rows = pl.ds(pl.multiple_of(c * C, C), C)
            qc, kc, vc = qs[rows, :], ks[rows, :], v_ref[rows, :]
            k2 = jnp.concatenate([jnp.where(m0, kc, 0), jnp.where(m0, 0, kc)], axis=0)
            kf = (kc.astype(F32) * e_kf).astype(BF16)
            jobs.append(dict(c=c, rows=rows, qc=qc, vc=vc, s2=_ntdot(qc, k2), kv=_tdot(kf, vc)))
        for j in jobs:
            rows, qf = j["rows"], j["qc"].astype(F32)
            lhs = jnp.concatenate([(j["s2"] * dcomb).astype(BF16), (qf * e_qf).astype(BF16),
                                   (qf * e_qb).astype(BF16)], axis=1)
            rhs = jnp.concatenate([_blockdiag_rows(j["vc"], LANES), sf.astype(BF16), sb_all[j["c"]]], axis=0)
            y = _dot(lhs, rhs)
            y = jnp.concatenate([_head_norm(y[:, :LANES], GN_EPS), _head_norm(y[:, LANES:], GN_EPS)], axis=1)
            y_ref[rows, :] = (_silu(g_ref[rows, :].astype(F32)) * y).astype(y_ref.dtype)
            sf = dec_f * sf + jnp.where(bd, j["kv"], 0.0)
        return sf

    sf_fin = lax.fori_loop(0, n // nch, fsweep, bd_state(0))

    if want_state:
        for d, s in ((0, sf_fin), (1, sb_fin)):
            sn_ref[d, 0] = s[:RET_DK, :RET_DV]
            sn_ref[d, 1] = s[RET_DK:, RET_DV:]


def _retention(lg, p16, cos, sin, s0, n_batch, seq, want_state):
    rope = cos is not None
    has_s0 = s0 is not None
    n_tok = n_batch * seq
    in_specs = [pl.BlockSpec(memory_space=pltpu.SMEM),
                pl.BlockSpec((seq, LANES), lambda b, p: (b, M_RQ // LANES + p)),
                pl.BlockSpec((seq, LANES), lambda b, p: (b, M_RK // LANES + p)),
                pl.BlockSpec((seq, 2 * LANES), lambda b, p: (b, M_RV // 256 + p)),
                pl.BlockSpec((seq, 2 * LANES), lambda b, p: (b, M_RG // 256 + p))]
    args = [lg, p16, p16, p16, p16]
    if rope:
        in_specs += [pl.BlockSpec((seq, LANES), lambda b, p: (0, 0))] * 2
        args += [cos, sin]
    st_spec = pl.BlockSpec((None, 2, 2, RET_DK, RET_DV), lambda b, p: (b, 0, p, 0, 0))
    if has_s0:
        in_specs.append(st_spec)
        args.append(s0)
    out_specs = [pl.BlockSpec((seq, 2 * LANES), lambda b, p: (b, p))]
    out_shape = [jax.ShapeDtypeStruct((n_tok, MIX_W), BF16)]
    if want_state:
        out_specs.append(st_spec)
        out_shape.append(jax.ShapeDtypeStruct((n_batch, 2, RET_H, RET_DK, RET_DV), F32))
    res = pl.pallas_call(
        functools.partial(_ret_body, rope=rope, has_s0=has_s0, want_state=want_state, seq=seq),
        grid=(n_batch, RET_H // 2),
        in_specs=in_specs, out_specs=out_specs, out_shape=out_shape,
        scratch_shapes=[pltpu.VMEM((seq, LANES), BF16), pltpu.VMEM((seq, LANES), BF16),
                        pltpu.VMEM((seq // CHUNK, LANES, 2 * LANES), BF16)],
        compiler_params=_params(("parallel", "parallel")),
        name="retention",
    )(*args)
    return res if want_state else (res[0], None)


def _ml_chunk_gate(bcol, icol, kc, n_x, m_x, reverse):
    C = CHUNK
    b_end = bcol[0:1] if reverse else bcol[C - 1:C]
    lwe = b_end - bcol + icol
    m_new = jnp.maximum(b_end + m_x, jnp.max(lwe, axis=0, keepdims=True))
    scale = jnp.exp(b_end + m_x - m_new)
    kw = kc.astype(F32) * jnp.exp(lwe - m_new)
    n_new = scale * n_x + jnp.sum(kw, axis=0, keepdims=True)
    return kw.astype(BF16), scale, n_new, m_new


def _chunk_cummax(x, reverse):
    rb = x.shape[0]
    row = _iota((rb, 1), 0) % CHUNK
    s = 1
    while s < CHUNK:
        if reverse:
            shifted, ok = pltpu.roll(x, rb - s, 0), row < CHUNK - s
        else:
            shifted, ok = pltpu.roll(x, s, 0), row >= s
        x = jnp.maximum(x, jnp.where(ok, shifted, -jnp.inf))
        s *= 2
    return x


def _ml_dir_att(s2, qf, bcol, brow, irow, mx, n_x, m_x, mask, ones):
    logw = jnp.where(mask, bcol - brow + irow, -jnp.inf)
    inter = bcol + m_x
    mt = jnp.maximum(inter, mx)
    att = s2 * jnp.exp(logw - mt)
    return dict(att=att, a_in=jnp.exp(inter - mt), mt=mt, rs=_seg_sum(att, ones), qs=_seg_sum(qf * n_x, ones))


def _ml_dir_lhs(t, qf):
    den = t["rs"] + t["a_in"] * t["qs"]
    sc = 1.0 / jnp.maximum(jnp.abs(den), jnp.exp(-t["mt"]))
    return (t["att"] * sc).astype(BF16), (qf * (t["a_in"] * sc)).astype(BF16)


def _ml_body(*refs, has_s0, want_state, seq):
    it = iter(refs)
    q_ref, k_ref, v_ref, o_ref, gt_ref, gb_ref, nw_ref = (next(it) for _ in range(7))
    s0_ref = nm0_ref = sn_ref = nmn_ref = None
    if has_s0:
        s0_ref, nm0_ref = next(it), next(it)
    y_ref = next(it)
    if want_state:
        sn_ref, nmn_ref = next(it), next(it)
    bcf_s, bcb_s, icf_s, icb_s, mxf_s, mxb_s, rows_s, cb_all, nmb_all = (next(it) for _ in range(9))
    ones = _seg_ones_pair()

    C = CHUNK
    n = seq // C
    nch = CHUNKS_PER_STEP
    hp = pl.program_id(1)
    m0, _ = _pair_masks()
    row0 = _iota((LANES, 1), 0) < 64
    bd = (_iota((LANES, 2 * LANES), 0) < 64) == (_iota((LANES, 2 * LANES), 1) < LANES)
    tril, triu, _ = _tri_masks()
    cum = _cumsum_mats()
    gbias = gb_ref[...]
    lanes = [20 + 2 * hp, 28 + 2 * hp, 16 + 2 * hp, 24 + 2 * hp]
    sel = _pair_select(lanes)
    pick = _pair_pick(lanes)
    rb = min(seq, 256)

    def prep(i, carry):
        r0 = pl.multiple_of(i * rb, rb)
        rows = pl.ds(r0, rb)
        g = gt_ref[rows, :] + gbias
        lf = -_softplus(-g)
        cs2 = [_sel_dot(cum, lf[q * C:(q + 1) * C]) for q in range(rb // C)]
        pre = jnp.concatenate([c2[:C] for c2 in cs2], axis=0)
        suf = jnp.concatenate([c2[C:] for c2 in cs2], axis=0)
        e = _dot_sel(jnp.concatenate([pre, suf, g], axis=0), sel)
        bcf, bcb = e[0:rb, 0:LANES], e[rb:2 * rb, LANES:2 * LANES]
        icf, icb = e[2 * rb:, 2 * LANES:3 * LANES], e[2 * rb:, 3 * LANES:]
        bcf_s[rows, :] = bcf
        bcb_s[rows, :] = bcb
        icf_s[rows, :] = icf
        icb_s[rows, :] = icb
        mxf_s[rows, :] = bcf + _chunk_cummax(icf - bcf, False)
        mxb_s[rows, :] = bcb + _chunk_cummax(icb - bcb, True)
        for q in range(rb // C):
            st = jnp.concatenate([_even_odd_stack(cs2[q][:C]), _even_odd_stack(cs2[q][C:]),
                                  _even_odd_stack(g[q * C:(q + 1) * C])], axis=0)
            o = _sel_ntdot(pick, st)
            rows_s[i * (rb // C) + q] = jnp.concatenate(
                [o[0:1, 0:LANES], o[1:2, LANES:2 * LANES], o[2:3, 2 * LANES:], o[3:4, 2 * LANES:],
                 jnp.zeros((4, LANES), F32)], axis=0)
        return carry

    lax.fori_loop(0, seq // rb, prep, 0)

    def row_scale(scale):
        return jnp.where(row0, scale[:, 0:1], scale[:, 64:65])

    def init(d):
        if not has_s0:
            return (jnp.zeros((LANES, 2 * LANES), F32), jnp.zeros((1, LANES), F32), jnp.zeros((1, LANES), F32))
        z = jnp.zeros((ML_DK, ML_DV), F32)
        cst = jnp.concatenate([jnp.concatenate([s0_ref[d, 0], z], axis=1),
                               jnp.concatenate([z, s0_ref[d, 1]], axis=1)], axis=0)
        return cst, nm0_ref[d:d + 1, :], nm0_ref[2 + d:3 + d, :]

    def bsweep(i, carry):
        cst, n_x, m_x = carry
        terms = []
        for q in range(nch):
            c = n - 1 - (i * nch + q)
            rows = pl.ds(pl.multiple_of(c * C, C), C)
            kw, scale, n_new, m_new = _ml_chunk_gate(bcb_s[rows, :], icb_s[rows, :], k_ref[rows, :], n_x, m_x, True)
            terms.append((c, n_x, m_x, scale, _tdot(kw, v_ref[rows, :])))
            n_x, m_x = n_new, m_new
        for c, n_in, m_in, scale, kv in terms:
            cb_all[c] = cst.astype(BF16)
            nmb_all[c, 0:1, :] = n_in
            nmb_all[c, 1:2, :] = m_in
            cst = row_scale(scale) * cst + jnp.where(bd, kv, 0.0)
        return cst, n_x, m_x

    cb_fin = lax.fori_loop(0, n // nch, bsweep, init(1))

    def fsweep(i, carry):
        cst, n_x, m_x = carry
        jobs = []
        for q in range(nch):
            c = i * nch + q
            rows = pl.ds(pl.multiple_of(c * C, C), C)
            kc = k_ref[rows, :]
            vc = v_ref[rows, :]
            qf = q_ref[rows, :].astype(F32) * ML_DK ** -0.5
            k2 = jnp.concatenate([jnp.where(m0, kc, 0), jnp.where(m0, 0, kc)], axis=0)
            bcf = bcf_s[rows, :]
            kw, scale, n_new, m_new = _ml_chunk_gate(bcf, icf_s[rows, :], kc, n_x, m_x, False)
            jobs.append(dict(c=c, rows=rows, qf=qf, vc=vc, bcf=bcf, n_in=n_x, m_in=m_x, scale=scale,
                             s2=_ntdot(qf.astype(BF16), k2), kv=_tdot(kw, vc)))
            n_x, m_x = n_new, m_new
        for j in jobs:
            c, rows, s2, qf = j["c"], j["rows"], j["s2"], j["qf"]
            rw = rows_s[c]
            j["tf"] = _ml_dir_att(s2, qf, j["bcf"], rw[0:1], rw[2:3], mxf_s[rows, :], j["n_in"], j["m_in"], tril, ones)
            j["tb"] = _ml_dir_att(s2, qf, bcb_s[rows, :], rw[1:2], rw[3:4], mxb_s[rows, :], nmb_all[c, 0:1, :],
                                  nmb_all[c, 1:2, :], triu, ones)
        for j in jobs:
            c, rows, qf = j["c"], j["rows"], j["qf"]
            af, qaf = _ml_dir_lhs(j["tf"], qf)
            ab, qab = _ml_dir_lhs(j["tb"], qf)
            vbd = _blockdiag_rows(j["vc"], LANES)
            lhs = jnp.concatenate([af, qaf, ab, qab], axis=1)
            rhs = jnp.concatenate([vbd, cst.astype(BF16), vbd, cb_all[c]], axis=0)
            y = _dot(lhs, rhs)
            y = jnp.concatenate([_head_norm(y[:, :LANES], GN_EPS), _head_norm(y[:, LANES:], GN_EPS)], axis=1)
            y = jax.nn.sigmoid(o_ref[rows, :].astype(F32)) * (y * nw_ref[...])
            y_ref[rows, :] = y.astype(y_ref.dtype)
            cst = row_scale(j["scale"]) * cst + jnp.where(bd, j["kv"], 0.0)
        return cst, n_x, m_x

    cf_fin = lax.fori_loop(0, n // nch, fsweep, init(0))

    if want_state:
        nmn_ref[...] = jnp.zeros((8, LANES), F32)
        for d, (cst, n_x, m_x) in ((0, cf_fin), (1, cb_fin)):
            sn_ref[d, 0] = cst[:ML_DK, :ML_DV]
            sn_ref[d, 1] = cst[ML_DK:, ML_DV:]
            nmn_ref[d:d + 1, :] = n_x
            nmn_ref[2 + d:3 + d, :] = m_x


def _mlstm(p16, p32, gbias, nw, s0, nm0, n_batch, seq, want_state):
    has_s0 = s0 is not None
    n_tok = n_batch * seq
    in_specs = [pl.BlockSpec((seq, LANES), lambda b, p: (b, M_MQ // LANES + p)),
                pl.BlockSpec((seq, LANES), lambda b, p: (b, M_MK // LANES + p)),
                pl.BlockSpec((seq, 2 * LANES), lambda b, p: (b, M_MV // 256 + p)),
                pl.BlockSpec((seq, 2 * LANES), lambda b, p: (b, M_MO // 256 + p)),
                pl.BlockSpec((seq, LANES), lambda b, p: (b, S_DT // LANES)),
                pl.BlockSpec((1, LANES), lambda b, p: (0, 0)),
                pl.BlockSpec((1, 2 * LANES), lambda b, p: (0, p))]
    args = [p16, p16, p16, p16, p32, gbias, nw]
    st_spec = pl.BlockSpec((None, 2, 2, ML_DK, ML_DV), lambda b, p: (b, 0, p, 0, 0))
    nm_spec = pl.BlockSpec((None, None, 8, LANES), lambda b, p: (b, p, 0, 0))
    if has_s0:
        in_specs += [st_spec, nm_spec]
        args += [s0, nm0]
    out_specs = [pl.BlockSpec((seq, 2 * LANES), lambda b, p: (b, p))]
    out_shape = [jax.ShapeDtypeStruct((n_tok, MIX_W), BF16)]
    if want_state:
        out_specs += [st_spec, nm_spec]
        out_shape += [jax.ShapeDtypeStruct((n_batch, 2, ML_H, ML_DK, ML_DV), F32),
                      jax.ShapeDtypeStruct((n_batch, ML_H // 2, 8, LANES), F32)]
    res = pl.pallas_call(
        functools.partial(_ml_body, has_s0=has_s0, want_state=want_state, seq=seq),
        grid=(n_batch, ML_H // 2),
        in_specs=in_specs, out_specs=out_specs, out_shape=out_shape,
        scratch_shapes=[pltpu.VMEM((seq, LANES), F32)] * 6
        + [pltpu.VMEM((seq // CHUNK, 8, LANES), F32), pltpu.VMEM((seq // CHUNK, LANES, 2 * LANES), BF16),
           pltpu.VMEM((seq // CHUNK, 8, LANES), F32)],
        compiler_params=_params(("parallel", "parallel")),
        name="mlstm",
    )(*args)
    return res if want_state else (res[0], None, None)


def _ml_pack_nm(st_n, st_m):
    B = st_n.shape[0]
    n = st_n.reshape(B, 2, ML_H // 2, 2 * ML_DK).transpose(0, 2, 1, 3)
    m = jnp.repeat(st_m.reshape(B, 2, ML_H // 2, 2), ML_DK, axis=-1).transpose(0, 2, 1, 3)
    return jnp.concatenate([n, m, jnp.zeros((B, ML_H // 2, 4, LANES), F32)], axis=2)


def _ml_unpack_nm(nm):
    B = nm.shape[0]
    n = nm[:, :, 0:2, :].transpose(0, 2, 1, 3).reshape(B, 2, ML_H, ML_DK)
    m = nm[:, :, 2:4, :].transpose(0, 2, 1, 3).reshape(B, 2, ML_H, ML_DK)[..., 0]
    return n, m


def _conv3_rows(ref, r0, rb, seq, w, bias):
    x = ref[pl.ds(r0, rb), :].astype(F32)
    prev = ref[pl.ds(pl.multiple_of(jnp.maximum(r0 - 16, 0), 16), 16), :].astype(F32)[15:16]
    nxt = ref[pl.ds(pl.multiple_of(jnp.minimum(r0 + rb, seq - 16), 16), 16), :].astype(F32)[0:1]
    prev = jnp.where(r0 == 0, 0.0, prev)
    nxt = jnp.where(r0 + rb == seq, 0.0, nxt)
    row = _iota((rb, 1), 0)
    x_dn = jnp.where(row == 0, prev, pltpu.roll(x, 1, 0))
    x_up = jnp.where(row == rb - 1, nxt, pltpu.roll(x, rb - 1, 0))
    y = w[0:1] * x_dn + w[1:2] * x + w[2:3] * x_up
    return y if bias is None else y + bias


def _ssd_body(*refs, has_s0, want_state, seq):
    it = iter(refs)
    (x_ref, b_ref, c_ref, z_ref, dt_ref, wx_ref, wb_ref, wc_ref, bx_ref, bb_ref, bc_ref,
     alog_ref, dtb_ref, dd_ref) = (next(it) for _ in range(14))
    s0_ref = sn_ref = None
    if has_s0:
        s0_ref = next(it)
    y_ref = next(it)
    if want_state:
        sn_ref = next(it)
    xs, bs, cs_, ccf_s, ccb_s, dcf_s, dcb_s, rows_s, sb_all = (next(it) for _ in range(9))

    C = CHUNK
    n = seq // C
    nch = CHUNKS_PER_STEP
    p = pl.program_id(1)
    tril, triu, _ = _tri_masks()
    a_lane = -jnp.exp(alog_ref[...])
    dsum = dd_ref[0:1, :] + dd_ref[1:2, :]
    rb = min(seq, 256)
    cum = _cumsum_mats()
    sel = _pair_select([2 * p, 8 + 2 * p])
    pick = _pair_pick([2 * p, 8 + 2 * p])

    def prep(i, carry):
        r0 = pl.multiple_of(i * rb, rb)
        rows = pl.ds(r0, rb)
        xs[rows, :] = _silu(_conv3_rows(x_ref, r0, rb, seq, wx_ref[...], bx_ref[...]))
        bs[rows, :] = _silu(_conv3_rows(b_ref, r0, rb, seq, wb_ref[...], bb_ref[...])).astype(BF16)
        cs_[rows, :] = _silu(_conv3_rows(c_ref, r0, rb, seq, wc_ref[...], bc_ref[...])).astype(BF16)
        dt = _softplus(dt_ref[rows, :] + dtb_ref[...])
        lw = dt * a_lane
        cs2 = [_sel_dot(cum, lw[q * C:(q + 1) * C]) for q in range(rb // C)]
        pre = jnp.concatenate([c2[:C] for c2 in cs2], axis=0)
        suf = jnp.concatenate([c2[C:] for c2 in cs2], axis=0)
        e = _dot_sel(jnp.concatenate([pre, suf, dt], axis=0), sel)
        ccf_s[rows, :] = e[0:rb, :LANES]
        ccb_s[rows, :] = e[rb:2 * rb, LANES:]
        dcf_s[rows, :] = e[2 * rb:, :LANES]
        dcb_s[rows, :] = e[2 * rb:, LANES:]
        for q in range(rb // C):
            st = jnp.concatenate([_even_odd_stack(cs2[q][:C]), _even_odd_stack(cs2[q][C:]),
                                  _even_odd_stack(dt[q * C:(q + 1) * C])], axis=0)
            o = _sel_ntdot(pick, st)
            rows_s[i * (rb // C) + q] = jnp.concatenate(
                [o[0:1, 0:LANES], o[1:2, LANES:2 * LANES], o[0:1, 2 * LANES:], o[1:2, 2 * LANES:],
                 jnp.zeros((4, LANES), F32)], axis=0)
        return carry

    lax.fori_loop(0, seq // rb, prep, 0)

    def state0(d):
        if not has_s0:
            return jnp.zeros((LANES, LANES), F32)
        return jnp.concatenate([s0_ref[d], jnp.zeros((SSD_N, LANES), F32)], axis=0)

    def state_update_terms(rows, ccol_s, dcol_s, rev):
        ccol = ccol_s[rows, :]
        last = ccol[0:1] if rev else ccol[C - 1:C]
        vdt = (xs[rows, :] * dcol_s[rows, :] * jnp.exp(last - ccol)).astype(BF16)
        return jnp.exp(last), _tdot(bs[rows, :], vdt)

    def bsweep(i, sb):
        terms = []
        for q in range(nch):
            c = n - 1 - (i * nch + q)
            rows = pl.ds(pl.multiple_of(c * C, C), C)
            terms.append((c,) + state_update_terms(rows, ccb_s, dcb_s, True))
        for c, dec, kv in terms:
            sb_all[c] = sb[:SSD_N].astype(BF16)
            sb = dec * sb + kv
        return sb

    sb_fin = lax.fori_loop(0, n // nch, bsweep, state0(1))

    def fsweep(i, sf):
        jobs = []
        for q in range(nch):
            c = i * nch + q
            rows = pl.ds(pl.multiple_of(c * C, C), C)
            j = dict(c=c, rows=rows, xc=xs[rows, :], bc=bs[rows, :], cc=cs_[rows, :],
                     ccf=ccf_s[rows, :], ccb=ccb_s[rows, :])
            rw = rows_s[c]
            j["m"] = (jnp.where(tril, jnp.exp(jnp.minimum(j["ccf"] - rw[0:1], 0.0)) * rw[2:3], 0.0)
                      + jnp.where(triu, jnp.exp(jnp.minimum(j["ccb"] - rw[1:2], 0.0)) * rw[3:4], 0.0))
            j["s2"] = _ntdot(j["cc"], jnp.concatenate([j["bc"], j["bc"]], axis=0))
            jobs.append(j)
        for j in jobs:
            att = (j["s2"] * j["m"]).astype(BF16)
            sbc = jnp.concatenate([sb_all[j["c"]], jnp.zeros((SSD_N, LANES), BF16)], axis=0)
            j["yi"] = _dot(att, _blockdiag_rows(j["xc"], 64).astype(BF16))
            j["ysb"] = _dot(j["cc"], sbc)
            j["dec"], j["kv"] = state_update_terms(j["rows"], ccf_s, dcf_s, False)
        for j in jobs:
            y = (j["yi"] + _dot(j["cc"], sf.astype(BF16)) * jnp.exp(j["ccf"]) + j["ysb"] * jnp.exp(j["ccb"])
                 + dsum * j["xc"])
            y_ref[j["rows"], :] = (y * _silu(z_ref[j["rows"], :].astype(F32))).astype(y_ref.dtype)
            sf = j["dec"] * sf + j["kv"]
        return sf

    sf_fin = lax.fori_loop(0, n // nch, fsweep, state0(0))
    if want_state:
        sn_ref[0] = sf_fin[:SSD_N]
        sn_ref[1] = sb_fin[:SSD_N]


def _ssd(p16, p32, conv_w, conv_b, alog, dtb, dd, s0, n_batch, seq, want_state):
    has_s0 = s0 is not None
    n_tok = n_batch * seq
    col = lambda off: (lambda b, p: (b, off // LANES + p))
    grp = lambda off: (lambda b, p: (b, off // LANES + p // 2))
    wcol = lambda off: (lambda b, p: (0, off + p))
    wgrp = lambda off: (lambda b, p: (0, off + p // 2))
    in_specs = [pl.BlockSpec((seq, LANES), col(M_SX)), pl.BlockSpec((seq, LANES), grp(M_SB)),
                pl.BlockSpec((seq, LANES), grp(M_SC)), pl.BlockSpec((seq, LANES), col(M_SZ)),
                pl.BlockSpec((seq, LANES), lambda b, p: (b, S_DT // LANES)),
                pl.BlockSpec((3, LANES), wcol(0)), pl.BlockSpec((3, LANES), wgrp(4)), pl.BlockSpec((3, LANES), wgrp(6)),
                pl.BlockSpec((1, LANES), wcol(0)), pl.BlockSpec((1, LANES), wgrp(4)), pl.BlockSpec((1, LANES), wgrp(6)),
                pl.BlockSpec((1, LANES), lambda b, p: (0, 0)), pl.BlockSpec((1, LANES), lambda b, p: (0, 0)),
                pl.BlockSpec((2, LANES), wcol(0))]
    args = [p16, p16, p16, p16, p32, conv_w, conv_w, conv_w, conv_b, conv_b, conv_b, alog, dtb, dd]
    st_spec = pl.BlockSpec((None, 2, None, SSD_N, LANES), lambda b, p: (b, 0, p, 0, 0))
    if has_s0:
        in_specs.append(st_spec)
        args.append(s0)
    out_specs = [pl.BlockSpec((seq, LANES), lambda b, p: (b, p))]
    out_shape = [jax.ShapeDtypeStruct((n_tok, MIX_W), BF16)]
    if want_state:
        out_specs.append(st_spec)
        out_shape.append(jax.ShapeDtypeStruct((n_batch, 2, SSD_H // 2, SSD_N, LANES), F32))
    res = pl.pallas_call(
        functools.partial(_ssd_body, has_s0=has_s0, want_state=want_state, seq=seq),
        grid=(n_batch, SSD_H // 2),
        in_specs=in_specs, out_specs=out_specs, out_shape=out_shape,
        scratch_shapes=[pltpu.VMEM((seq, LANES), F32), pltpu.VMEM((seq, LANES), BF16), pltpu.VMEM((seq, LANES), BF16)]
        + [pltpu.VMEM((seq, LANES), F32)] * 4
        + [pltpu.VMEM((seq // CHUNK, 8, LANES), F32), pltpu.VMEM((seq // CHUNK, SSD_N, LANES), BF16)],
        compiler_params=_params(("parallel", "parallel")),
        name="ssd",
    )(*args)
    return res if want_state else (res[0], None)


def _ssd_pack_state(st):
    B = st.shape[0]
    return st.reshape(B, 2, SSD_H // 2, 2, SSD_N, SSD_P).transpose(0, 1, 2, 4, 3, 5).reshape(B, 2, SSD_H // 2, SSD_N, 2 * SSD_P)


def _ssd_unpack_state(st):
    B = st.shape[0]
    return st.reshape(B, 2, SSD_H // 2, SSD_N, 2, SSD_P).transpose(0, 1, 2, 4, 3, 5).reshape(B, 2, SSD_H, SSD_N, SSD_P)


def _b16(x):
    return x.astype(BF16)


def _rwkv_chunk_stage_a(job, cum, refs):
    d, r0, rev = job["d"], job["r0"], job["rev"]
    rs, vs, as_, lws, kds, bs = refs
    C = CHUNK
    lw = lws[d, pl.ds(r0, C), :]
    g = _sel_dot(cum[C:] if rev else cum[:C], lw)
    g_end = g[0:1] if rev else g[C - 1:C]
    r, a = rs[pl.ds(r0, C), :], as_[pl.ds(r0, C), :]
    kd, b = kds[d, pl.ds(r0, C), :], bs[d, pl.ds(r0, C), :]
    e_g, e_ng, e_end = jnp.exp(g), jnp.exp(-g), jnp.exp(g_end - g)
    job.update(v=vs[pl.ds(r0, C), :], at=a * jnp.exp(g - lw), rt=r * e_g, bt=b * e_ng, kt=kd * e_ng,
               bh=b * e_end, kh=kd * e_end, dec=jnp.exp(g_end))


def _rwkv_chunk_stage_b(job, m0, strict_f, incl_f, strict_b, incl_b):
    C = CHUNK
    strict, incl = (strict_b, incl_b) if job["rev"] else (strict_f, incl_f)
    lhs = _b16(jnp.concatenate([job["at"], job["rt"]], axis=0))
    bt, kt = job["bt"], job["kt"]
    rhs = _b16(jnp.concatenate([jnp.where(m0, bt, 0.0), jnp.where(m0, 0.0, bt),
                                jnp.where(m0, kt, 0.0), jnp.where(m0, 0.0, kt)], axis=0))
    a4 = _ntdot(lhs, rhs)
    job.update(lab=jnp.where(strict, a4[:C, :LANES], 0.0), lak=jnp.where(strict, a4[:C, LANES:], 0.0),
               mrb=jnp.where(incl, a4[C:, :LANES], 0.0), mrk=jnp.where(incl, a4[C:, LANES:], 0.0))
    job.update(lbd=_blockdiag_rows(job["lab"], 64))


RWKV_INV_BASE = 8


def _rwkv_chunk_inverse(jobs):
    i = _iota((LANES, LANES), 0)
    c = _iota((LANES, LANES), 1)
    diag = (i // RWKV_INV_BASE) == (c // RWKV_INV_BASE)
    eye = jnp.where(i == c, 1.0, 0.0)
    for j in jobs:
        j["pw"] = jnp.where(diag, j["lbd"], 0.0)
        j["tm"] = eye + j["pw"]
    s = 2
    while s < RWKV_INV_BASE:
        for j in jobs:
            j["pw"] = _dot(_b16(j["pw"]), _b16(j["pw"]))
        for j in jobs:
            j["tm"] = j["tm"] + _dot(_b16(j["tm"]), _b16(j["pw"]))
        s *= 2
    s = RWKV_INV_BASE
    while s < CHUNK:
        off = ((i // (2 * s)) == (c // (2 * s))) & ((i // s) != (c // s))
        for j in jobs:
            j["tl"] = _dot(_b16(j["tm"]), _b16(jnp.where(off, j["lbd"], 0.0)))
        for j in jobs:
            j["tm"] = j["tm"] + _dot(_b16(j["tl"]), _b16(j["tm"]))
        s *= 2


def _rwkv_chunk_stage_c(job):
    C = CHUNK
    v = job["v"]
    at_rows = _blockdiag_rows(job["at"], 64)
    v_rows = _blockdiag_rows(v, 64)
    wv = _dot(_b16(job["lak"]), _b16(v_rows))
    job.update(at_rows=at_rows, v_rows=v_rows, wv_rows=_blockdiag_rows(wv, 64))


def _rwkv_chunk_stage_d(job):
    C = CHUNK
    tx = _dot(_b16(job["tm"]), _b16(jnp.concatenate([job["at_rows"], job["wv_rows"]], axis=1)))
    job.update(ah_rows=tx[:, :LANES], uv_rows=tx[:, LANES:],
               ah=tx[:C, :LANES] + tx[C:, :LANES], uv=tx[:C, LANES:] + tx[C:, LANES:])


def _rwkv_chunk_stage_e(job, bd128, eye):
    C = CHUNK
    z = jnp.zeros((2 * C, LANES), F32)
    rhs_m = jnp.concatenate([jnp.concatenate([job["ah_rows"], job["uv_rows"]], axis=1),
                             jnp.concatenate([z, job["v_rows"]], axis=1)], axis=0)
    my = _dot(_b16(jnp.concatenate([job["mrb"], job["mrk"]], axis=1)), _b16(rhs_m))
    zc = jnp.zeros((C, LANES), F32)
    rhs_g = jnp.concatenate([jnp.concatenate([job["ah"], job["uv"]], axis=1),
                             jnp.concatenate([zc, job["v"]], axis=1)], axis=0)
    gh = _tdot(_b16(jnp.concatenate([job["bh"], job["kh"]], axis=0)), _b16(rhs_g))
    job.update(rh=job["rt"] + my[:, :LANES], yv=my[:, LANES:],
               gm=jnp.where(bd128, gh[:, :LANES], 0.0) + jnp.where(eye, job["dec"], 0.0),
               hm=jnp.where(bd128, gh[:, LANES:], 0.0))


def _rwkv_chunk_stage_f(job, p):
    C = CHUNK
    out = _dot(_b16(jnp.concatenate([job["rh"], job["gm"]], axis=0)), _b16(p))
    return out[:C] + job["yv"], out[C:] + job["hm"]


def _rwkv_body(*refs, has_s0, want_state, seq):
    it = iter(refs)
    (r_ref, k_ref, v_ref, lora_ref, g_ref, cwr_ref, cwk_ref, cwv_ref, w0_ref, wup_ref, a0_ref, aup_ref,
     kkw_ref, kaw_ref, rkw_ref, gup_ref, lnw_ref, lnb_ref) = (next(it) for _ in range(18))
    s0_ref = sn_ref = None
    if has_s0:
        s0_ref = next(it)
    y_ref = next(it)
    if want_state:
        sn_ref = next(it)
    rs, vs, as_, lws, kds, bs, yf, yb = (next(it) for _ in range(8))

    C = CHUNK
    n = seq // C
    m0, _ = _pair_masks()
    ones = _seg_ones_pair()
    rb = min(seq, 256)

    def prep(i, carry):
        r0 = pl.multiple_of(i * rb, rb)
        r_ = _conv3_rows(r_ref, r0, rb, seq, cwr_ref[...], None)
        k_ = _conv3_rows(k_ref, r0, rb, seq, cwk_ref[...], None)
        rs[pl.ds(r0, rb), :] = r_
        vs[pl.ds(r0, rb), :] = _conv3_rows(v_ref, r0, rb, seq, cwv_ref[...], None)
        kk = k_ * kkw_ref[...]
        kk = kk / jnp.maximum(jnp.sqrt(_seg_sum(kk * kk, ones)), 1e-12)
        as_[pl.ds(r0, rb), :] = -kk
        lora = lora_ref[pl.ds(r0, rb), :]
        w_low, a_low = _b16(jnp.tanh(lora)), _b16(lora)
        for d in range(2):
            w_log = -_softplus(-(w0_ref[d:d + 1, :] + _dot(w_low, wup_ref[d]))) - 0.5
            lws[d, pl.ds(r0, rb), :] = -jnp.exp(w_log)
            iclr = jax.nn.sigmoid(a0_ref[d:d + 1, :] + _dot(a_low, aup_ref[d]))
            kds[d, pl.ds(r0, rb), :] = k_ * (1.0 + (iclr - 1.0) * kaw_ref[...])
            bs[d, pl.ds(r0, rb), :] = kk * iclr
        return carry

    lax.fori_loop(0, seq // rb, prep, 0)

    cum = _cumsum_mats()
    ii = _iota((C, LANES), 0)
    jj = _iota((C, LANES), 1) % C
    strict_f, incl_f, strict_b, incl_b = jj < ii, jj <= ii, jj > ii, jj >= ii
    bd128 = (_iota((LANES, LANES), 0) < 64) == (_iota((LANES, LANES), 1) < 64)
    eye = _iota((LANES, LANES), 0) == _iota((LANES, LANES), 1)
    scr = (rs, vs, as_, lws, kds, bs)

    nch = min(RWKV_CHUNKS_PER_STEP, n)

    def sweep(i, carry):
        pf, pb = carry
        jobs = []
        for q in range(nch):
            jobs.append(dict(d=0, rev=False, r0=pl.multiple_of((i * nch + q) * C, C)))
            jobs.append(dict(d=1, rev=True, r0=pl.multiple_of((n - 1 - i * nch - q) * C, C)))
        for j in jobs:
            _rwkv_chunk_stage_a(j, cum, scr)
        for j in jobs:
            _rwkv_chunk_stage_b(j, m0, strict_f, incl_f, strict_b, incl_b)
        _rwkv_chunk_inverse(jobs)
        for j in jobs:
            _rwkv_chunk_stage_c(j)
        for j in jobs:
            _rwkv_chunk_stage_d(j)
        for j in jobs:
            _rwkv_chunk_stage_e(j, bd128, eye)
        for q in range(nch):
            y_f, pf = _rwkv_chunk_stage_f(jobs[2 * q], pf)
            y_b, pb = _rwkv_chunk_stage_f(jobs[2 * q + 1], pb)
            yf[pl.ds(jobs[2 * q]["r0"], C), :] = y_f
            yb[pl.ds(jobs[2 * q + 1]["r0"], C), :] = y_b
        return pf, pb

    p0 = (s0_ref[0], s0_ref[1]) if has_s0 else (jnp.zeros((LANES, LANES), F32),) * 2
    pf, pb = lax.fori_loop(0, n // nch, sweep, p0)
    if want_state:
        sn_ref[0] = pf
        sn_ref[1] = pb

    def post(i, carry):
        r0 = pl.multiple_of(i * rb, rb)
        rows = pl.ds(r0, rb)
        y = yf[rows, :] + yb[rows, :]
        mu = _seg_sum(y, ones) * (1.0 / RWKV_D)
        dv = y - mu
        var = _seg_sum(dv * dv, ones) * (1.0 / RWKV_D)
        bonus = _seg_sum(rs[rows, :] * (kds[0, rows, :] + kds[1, rows, :]) * rkw_ref[...], ones) * vs[rows, :]
        out = dv * lax.rsqrt(var + RWKV_GN_EPS) * lnw_ref[...] + lnb_ref[...] + bonus
        g = _dot(_b16(jax.nn.sigmoid(g_ref[rows, :])), gup_ref[...])
        y_ref[rows, :] = (out * g).astype(y_ref.dtype)
        return carry

    lax.fori_loop(0, seq // rb, post, 0)


def _rwkv_chunked(p16, p32, lw, s0, n_batch, seq, want_state):
    has_s0 = s0 is not None
    n_tok = n_batch * seq
    base = M_RWKV // LANES
    tok = lambda off: pl.BlockSpec((seq, LANES), lambda b, p: (b, off + p))
    wcol = lambda rows: pl.BlockSpec((rows, LANES), lambda b, p: (0, p))
    in_specs = [tok(base), tok(base + 4), tok(base + 8),
                pl.BlockSpec((seq, LANES), lambda b, p: (b, S_WWD // LANES)),
                pl.BlockSpec((seq, LANES), lambda b, p: (b, S_WGD // LANES)),
                pl.BlockSpec((3, LANES), lambda b, p: (0, p)),
                pl.BlockSpec((3, LANES), lambda b, p: (0, 4 + p)),
                pl.BlockSpec((3, LANES), lambda b, p: (0, 8 + p)),
                wcol(2), pl.BlockSpec((2, LANES, LANES), lambda b, p: (0, 0, p)),
                wcol(2), pl.BlockSpec((2, LANES, LANES), lambda b, p: (0, 0, p)),
                wcol(1), wcol(1), wcol(1),
                pl.BlockSpec((LANES, LANES), lambda b, p: (0, p)), wcol(1), wcol(1)]
    args = [p16, p16, p16, p32, p32, lw["conv_w"], lw["conv_w"], lw["conv_w"], lw["w0"], lw["w_up"], lw["a0"],
            lw["a_up"], lw["k_k"], lw["k_a"], lw["r_k"], lw["g_up"], lw["ln_w"], lw["ln_b"]]
    st_spec = pl.BlockSpec((None, 2, None, LANES, LANES), lambda b, p: (b, 0, p, 0, 0))
    if has_s0:
        in_specs.append(st_spec)
        args.append(s0)
    out_specs = [pl.BlockSpec((seq, LANES), lambda b, p: (b, p))]
    out_shape = [jax.ShapeDtypeStruct((n_tok, MIX_W), BF16)]
    if want_state:
        out_specs.append(st_spec)
        out_shape.append(jax.ShapeDtypeStruct((n_batch, 2, RWKV_H // 2, LANES, LANES), F32))
    tokf = pltpu.VMEM((seq, LANES), F32)
    tok2 = pltpu.VMEM((2, seq, LANES), F32)
    res = pl.pallas_call(
        functools.partial(_rwkv_body, has_s0=has_s0, want_state=want_state, seq=seq),
        grid=(n_batch, RWKV_H // 2),
        in_specs=in_specs, out_specs=out_specs, out_shape=out_shape,
        scratch_shapes=[tokf, tokf, tokf, tok2, tok2, tok2, tokf, tokf],
        compiler_params=_params(("parallel", "parallel")),
        name="rwkv",
    )(*args)
    return res if want_state else (res[0], None)


def _rwkv_pack_state(st):
    B = st.shape[0]
    p = jnp.swapaxes(st, -1, -2).reshape(B, 2, RWKV_H // 2, 2, RWKV_D, RWKV_D)
    z = jnp.zeros_like(p[:, :, :, 0])
    return jnp.concatenate([jnp.concatenate([p[:, :, :, 0], z], axis=-1),
                            jnp.concatenate([z, p[:, :, :, 1]], axis=-1)], axis=-2)


def _rwkv_unpack_state(pbd):
    h0 = pbd[:, :, :, :RWKV_D, :RWKV_D]
    h1 = pbd[:, :, :, RWKV_D:, RWKV_D:]
    p = jnp.stack([h0, h1], axis=3)
    B = p.shape[0]
    return jnp.swapaxes(p, -1, -2).reshape(B, 2, RWKV_H, RWKV_D, RWKV_D)


def _merge_body(yr_ref, ys_ref, yw_ref, ym_ref, g_ref, wb_ref, wo_ref, x_ref, mod_ref, sn_ref, o_ref):
    ys = ys_ref[...].astype(F32)
    ys = ys * lax.rsqrt(jnp.mean(ys * ys, axis=-1, keepdims=True) + EPS) * sn_ref[...]
    branches = (yr_ref[...], ys.astype(BF16), yw_ref[...], ym_ref[...])
    merged = None
    for i, br in enumerate(branches):
        gate = jax.nn.sigmoid(g_ref[:, i * D_MODEL:(i + 1) * D_MODEL].astype(F32))
        term = gate * _dot(br, wb_ref[i])
        merged = term if merged is None else merged + term
    out = _dot(merged.astype(BF16), wo_ref[...])
    o_ref[...] = x_ref[...] + mod_ref[2:3, :] * out


def _merge(y_ret, y_ssd, y_rw, y_ml, p16, w_branch, w_out, x, mod, ssd_norm, tm, seq):
    n_tok = x.shape[0]
    mi = _mod_index(mod.shape[0], tm, seq)
    yb = pl.BlockSpec((tm, MIX_W), lambda i: (i, 0))
    xb = pl.BlockSpec((tm, D_MODEL), lambda i: (i, 0))
    return pl.pallas_call(
        _merge_body,
        grid=(n_tok // tm,),
        in_specs=[yb, yb, yb, yb,
                  pl.BlockSpec((tm, N_BRANCH * D_MODEL), lambda i: (i, M_GATE)),
                  pl.BlockSpec((N_BRANCH, MIX_W, D_MODEL), lambda i: (0, 0, 0)),
                  pl.BlockSpec((D_MODEL, D_MODEL), lambda i: (0, 0)),
                  xb,
                  pl.BlockSpec((None, 8, D_MODEL), lambda i: (mi(i), 0, 0)),
                  pl.BlockSpec((1, MIX_W), lambda i: (0, 0))],
        out_specs=xb,
        out_shape=jax.ShapeDtypeStruct((n_tok, D_MODEL), F32),
        compiler_params=_params(("parallel",)),
        name="merge",
    )(y_ret, y_ssd, y_rw, y_ml, p16, w_branch, w_out, x, mod, ssd_norm)


def _ffn_body(x_ref, xp_ref, xn_ref, mod_ref, nw_ref, uv_ref, ug_ref, cwv_ref, cwg_ref, cbv_ref, cbg_ref,
              dn_ref, fw_ref, o_ref, h_ref, *, tm, seq, final):
    i = pl.program_id(0)
    f = pl.program_id(1)
    shift, scale, gate = mod_ref[3:4, :], mod_ref[4:5, :], mod_ref[5:6, :]

    @pl.when(f == 0)
    def _():
        nw = nw_ref[...]
        h_ref[0:tm, :] = _rms_mod(x_ref[...], nw, shift, scale).astype(BF16)
        h_ref[tm:tm + 8, :] = _rms_mod(xp_ref[...], nw, shift, scale).astype(BF16)
        h_ref[tm + 8:tm + 16, :] = _rms_mod(xn_ref[...], nw, shift, scale).astype(BF16)
        o_ref[...] = jnp.zeros(o_ref.shape, F32)

    row = _iota((tm, 1), 0)
    pos = (i * tm + row) % seq
    first, last = pos == 0, pos == seq - 1
    h = h_ref[...]

    def conv(u_ref, cw_ref, cb_ref):
        u = _dot(h, u_ref[...])
        um = u[0:tm]
        u_dn = jnp.where(row == 0, u[tm + 7:tm + 8], pltpu.roll(um, 1, 0))
        u_up = jnp.where(row == tm - 1, u[tm + 8:tm + 9], pltpu.roll(um, tm - 1, 0))
        u_dn = jnp.where(first, 0.0, u_dn)
        u_up = jnp.where(last, 0.0, u_up)
        cw = cw_ref[...]
        return cw[0:1] * u_dn + cw[1:2] * um + cw[2:3] * u_up + cb_ref[...]

    act = conv(uv_ref, cwv_ref, cbv_ref) * _silu(conv(ug_ref, cwg_ref, cbg_ref))
    o_ref[...] += _dot(act.astype(BF16), dn_ref[...])

    @pl.when(f == pl.num_programs(1) - 1)
    def _():
        xn = x_ref[...] + gate * o_ref[...]
        if final:
            xn = xn * lax.rsqrt(jnp.mean(xn * xn, axis=-1, keepdims=True) + EPS) * fw_ref[...]
        o_ref[...] = xn


def _ffn(x, mod, nw, up, conv_w, conv_b, down, final_w, tm, seq, final):
    n_tok = x.shape[0]
    fc = FFN_DIM // 2
    nf = FFN_DIM // fc
    mi = _mod_index(mod.shape[0], tm, seq)
    n8 = n_tok // 8
    xb = pl.BlockSpec((tm, D_MODEL), lambda i, f: (i, 0))
    one = lambda i, f: (0, 0)
    return pl.pallas_call(
        functools.partial(_ffn_body, tm=tm, seq=seq, final=final),
        grid=(n_tok // tm, nf),
        in_specs=[xb,
                  pl.BlockSpec((8, D_MODEL), lambda i, f: (jnp.maximum(i * (tm // 8) - 1, 0), 0)),
                  pl.BlockSpec((8, D_MODEL), lambda i, f: (jnp.minimum((i + 1) * (tm // 8), n8 - 1), 0)),
                  pl.BlockSpec((None, 8, D_MODEL), lambda i, f: (mi(i), 0, 0)),
                  pl.BlockSpec((1, D_MODEL), one),
                  pl.BlockSpec((D_MODEL, fc), lambda i, f: (0, f)),
                  pl.BlockSpec((D_MODEL, fc), lambda i, f: (0, nf + f)),
                  pl.BlockSpec((3, fc), lambda i, f: (0, f)),
                  pl.BlockSpec((3, fc), lambda i, f: (0, nf + f)),
                  pl.BlockSpec((1, fc), lambda i, f: (0, f)),
                  pl.BlockSpec((1, fc), lambda i, f: (0, nf + f)),
                  pl.BlockSpec((fc, D_MODEL), lambda i, f: (f, 0)),
                  pl.BlockSpec((1, D_MODEL), one)],
        out_specs=xb,
        out_shape=jax.ShapeDtypeStruct((n_tok, D_MODEL), F32),
        scratch_shapes=[pltpu.VMEM((tm + 16, D_MODEL), BF16)],
        compiler_params=_params(("parallel", "arbitrary")),
        name="conv_ffn",
    )(x, x, x, mod, nw, up, up, conv_w, conv_w, conv_b, conv_b, down, final_w)


def _rope_tables(seq):
    rows = seq // GRID_W
    rr, cc = jnp.meshgrid(jnp.arange(rows), jnp.arange(GRID_W), indexing='ij')
    nf = RET_DK // 4
    inv = ROPE_BASE ** (-jnp.arange(nf, dtype=F32) / nf)
    ang = jnp.concatenate([rr.reshape(-1, 1) * inv, cc.reshape(-1, 1) * inv], axis=-1)
    cos, sin = jnp.cos(ang), jnp.sin(ang)
    return (jnp.tile(jnp.concatenate([cos, cos], axis=-1), (1, 2)),
            jnp.tile(jnp.concatenate([-sin, sin], axis=-1), (1, 2)))


def _lanes16(a):
    return jnp.zeros((1, LANES), F32).at[0, :16].set(a.reshape(-1).astype(F32))


def _pad_rows(w, lo):
    return jnp.zeros((2, LANES, MIX_W), F32).at[:, lo:lo + w.shape[1]].set(w).astype(BF16)


def _layer(x, mod, lw, states, rope, n_batch, seq, want_state, final, final_w):
    tm_p = 1024
    tm = 512
    p16, p32 = _in_proj(x, mod, lw["norm1"], lw["w_main"], lw["w_small"], tm_p, 2048, seq)
    st_ret, st_ssd, st_rwkv, st_c, st_nm = states
    cos, sin = rope if rope is not None else (None, None)
    y_ret, n_ret = _retention(lw["ret_lg"], p16, cos, sin, st_ret, n_batch, seq, want_state)
    y_ssd, n_ssd = _ssd(p16, p32, lw["ssd_conv_w"], lw["ssd_conv_b"], lw["ssd_alog"], lw["ssd_dtb"], lw["ssd_dd"],
                        st_ssd, n_batch, seq, want_state)
    y_rw, n_rwkv = _rwkv_chunked(p16, p32, lw["rwkv"], st_rwkv, n_batch, seq, want_state)
    y_ml, n_c, n_nm = _mlstm(p16, p32, lw["ml_gbias"], lw["ml_norm"], st_c, st_nm, n_batch, seq, want_state)
    x = _merge(y_ret, y_ssd, y_rw, y_ml, p16, lw["w_branch"], lw["w_out"], x, mod, lw["ssd_norm"], tm, seq)
    x = _ffn(x, mod, lw["norm2"], lw["ffn_up"], lw["ffn_conv_w"], lw["ffn_conv_b"], lw["ffn_down"], final_w,
             tm, seq, final)
    return x, (n_ret, n_ssd, n_rwkv, n_c, n_nm)


def kernel(x_prompt, x_sample, state_ret, state_ssd, state_rwkv, state_mlstm_c, state_mlstm_n, state_mlstm_m, c, c_ctx, ada_w, ada_b, norm1, norm2, w_in, ret_log_rate, ssd_conv_w, ssd_conv_b, ssd_A_log, ssd_dt_bias, ssd_D, ssd_norm, rwkv_conv_w, rwkv_w0, rwkv_w_up, rwkv_a0, rwkv_a_up, rwkv_g_up, rwkv_k_k, rwkv_k_a, rwkv_r_k, rwkv_ln_w, rwkv_ln_b, ml_i_bias, ml_f_bias, ml_norm, w_branch, w_out, ffn_up, ffn_conv_w, ffn_conv_b, ffn_down, final_norm):
    nb_c, seq_c, _ = x_prompt.shape
    nb_l, seq_l, _ = x_sample.shape
    assert seq_c % 256 == 0 and seq_l % 256 == 0 and nb_l * RWKV_H * 2 % LANES == 0

    cvec = jnp.zeros((16, D_MODEL), F32).at[0].set(c_ctx).at[1:1 + nb_l].set(c)
    mod_all = _modulation(cvec, ada_w, ada_b).reshape(DEPTH, 16, 6, D_MODEL)
    mod_all = jnp.pad(mod_all, ((0, 0), (0, 0), (0, 2), (0, 0)))
    rope = _rope_tables(seq_l)
    main_perm, small_perm, conv_perm = _main_perm(), _small_perm(), _ssd_conv_perm()
    row = lambda a: a.reshape(1, -1).astype(F32)
    final_w = row(final_norm)

    xp = x_prompt.reshape(nb_c * seq_c, D_MODEL)
    xs = x_sample.reshape(nb_l * seq_l, D_MODEL)
    new_states = []
    for l in range(DEPTH):
        lw = dict(
            norm1=row(norm1[l]), norm2=row(norm2[l]),
            w_main=_take_cols(w_in[l], main_perm).astype(BF16),
            w_small=_take_cols(w_in[l], small_perm).astype(BF16),
            ret_lg=-jnp.exp(ret_log_rate[l].astype(F32)),
            ssd_conv_w=_take_cols(ssd_conv_w[l], conv_perm), ssd_conv_b=_take_cols(row(ssd_conv_b[l]), conv_perm),
            ssd_alog=_lanes16(ssd_A_log[l]), ssd_dtb=_lanes16(ssd_dt_bias[l]),
            ssd_dd=jnp.repeat(ssd_D[l].astype(F32), SSD_P, axis=1), ssd_norm=row(ssd_norm[l]),
            rwkv=dict(conv_w=rwkv_conv_w[l], w0=rwkv_w0[l], w_up=_pad_rows(rwkv_w_up[l], 0),
                      a0=rwkv_a0[l], a_up=_pad_rows(rwkv_a_up[l], RWKV_W_LORA),
                      k_k=row(rwkv_k_k[l]), k_a=row(rwkv_k_a[l]), r_k=row(rwkv_r_k[l]),
                      g_up=rwkv_g_up[l].astype(BF16), ln_w=row(rwkv_ln_w[l]), ln_b=row(rwkv_ln_b[l])),
            ml_gbias=jnp.zeros((1, LANES), F32).at[0, 16:32].set(
                jnp.stack([ml_i_bias[l], ml_f_bias[l]], axis=1).reshape(-1)),
            ml_norm=row(ml_norm[l]),
            w_branch=w_branch[l].astype(BF16), w_out=w_out[l].astype(BF16),
            ffn_up=ffn_up[l].astype(BF16), ffn_conv_w=ffn_conv_w[l], ffn_conv_b=row(ffn_conv_b[l]),
            ffn_down=ffn_down[l].astype(BF16),
        )
        final = l == DEPTH - 1
        mod_c = mod_all[l, 0:1]
        mod_l = mod_all[l, 1:1 + nb_l]
        xp, st = _layer(xp, mod_c, lw, (None,) * 5, None, nb_c, seq_c, True, final, final_w)
        new_states.append(st)
        lat_states = (state_ret[:, l], _ssd_pack_state(state_ssd[:, l]), _rwkv_pack_state(state_rwkv[:, l]),
                      state_mlstm_c[:, l],
                      _ml_pack_nm(state_mlstm_n[:, l], state_mlstm_m[:, l]))
        xs, _ = _layer(xs, mod_l, lw, lat_states, rope, nb_l, seq_l, False, final, final_w)

    new_ret = jnp.stack([s[0] for s in new_states], axis=1)
    new_ssd = jnp.stack([_ssd_unpack_state(s[1]) for s in new_states], axis=1)
    new_rwkv = jnp.stack([_rwkv_unpack_state(s[2]) for s in new_states], axis=1)
    new_c = jnp.stack([s[3] for s in new_states], axis=1)
    nm = [_ml_unpack_nm(s[4]) for s in new_states]
    new_n = jnp.stack([a for a, _ in nm], axis=1)
    new_m = jnp.stack([b for _, b in nm], axis=1)
    return (xp.reshape(nb_c, seq_c, D_MODEL), xs.reshape(nb_l, seq_l, D_MODEL),
            new_ret, new_ssd, new_rwkv, new_c, new_n, new_m)
```

```python
import functools
import math

import jax
import jax.numpy as jnp
import numpy as np
from jax import lax
from jax.experimental import pallas as pl
from jax.experimental.pallas import tpu as pltpu

F32 = jnp.float32
BF16 = jnp.bfloat16

D_MODEL = 1024
DEPTH = 2
GRID_W = 64
CHUNK = 64
N_BRANCH = 4
MIX_W = 512
RET_H, RET_DK, RET_DV = 4, 64, 128
SSD_H, SSD_P, SSD_N, SSD_G = 8, 64, 64, 2
RWKV_H, RWKV_D = 8, 64
RWKV_W_LORA, RWKV_A_LORA, RWKV_G_LORA = 64, 64, 128
ML_H, ML_DK, ML_DV = 4, 64, 128
FFN_DIM = 2816
ROPE_BASE = 10000.0
EPS = 1e-6
GN_EPS = 1e-5
RWKV_GN_EPS = 64e-5

LANES = 128
VMEM_LIMIT = 56 * 1024 * 1024
PREP_UNROLL = 2
CHUNKS_PER_STEP = 8
RWKV_CHUNKS_PER_STEP = 8

_O_RQ, _O_RK, _O_RV, _O_RG = 0, 256, 512, 1024
_O_SZ, _O_SX, _O_SB, _O_SC, _O_SDT = 1536, 2048, 2560, 2688, 2816
_O_WRKV, _O_WWD, _O_WAD, _O_WGD = 2832, 4368, 4432, 4496
_O_MQ, _O_MK, _O_MV, _O_MO, _O_MIF, _O_MG = 4624, 4880, 5136, 5648, 6160, 6176

M_GATE, M_RQ, M_RK, M_RV, M_RG = 0, 4096, 4352, 4608, 5120
M_SZ, M_RWKV = 5632, 6144
M_MQ, M_MK, M_MV, M_MO = 7680, 7936, 8192, 8704
M_SX, M_SB, M_SC = 9216, 9728, 9984
N_MAIN = 10240
S_WGD, S_WWD, S_WAD, S_DT, S_MIF = 0, 128, 192, 256, 272
N_SMALL = 384


def _main_perm():
    idx = np.full((N_MAIN,), -1, np.int64)

    def put(dst, src, n):
        idx[dst:dst + n] = np.arange(src, src + n)

    put(M_GATE, _O_MG, 4096)
    put(M_RQ, _O_RQ, 256); put(M_RK, _O_RK, 256); put(M_RV, _O_RV, 512); put(M_RG, _O_RG, 512)
    put(M_SZ, _O_SZ, 512); put(M_RWKV, _O_WRKV, 1536)
    put(M_MQ, _O_MQ, 256); put(M_MK, _O_MK, 256); put(M_MV, _O_MV, 512); put(M_MO, _O_MO, 512)
    put(M_SX, _O_SX, 512)
    for g in range(SSD_G):
        put(M_SB + g * LANES, _O_SB + g * SSD_N, SSD_N)
        put(M_SC + g * LANES, _O_SC + g * SSD_N, SSD_N)
    return idx


def _small_perm():
    idx = np.full((N_SMALL,), -1, np.int64)
    idx[S_WGD:S_WGD + 128] = np.arange(_O_WGD, _O_WGD + 128)
    idx[S_WWD:S_WWD + 64] = np.arange(_O_WWD, _O_WWD + 64)
    idx[S_WAD:S_WAD + 64] = np.arange(_O_WAD, _O_WAD + 64)
    idx[S_DT:S_DT + 16] = np.arange(_O_SDT, _O_SDT + 16)
    idx[S_MIF:S_MIF + 16] = np.arange(_O_MIF, _O_MIF + 16)
    return idx


def _ssd_conv_perm():
    idx = np.full((1024,), -1, np.int64)
    idx[0:512] = np.arange(0, 512)
    for g in range(SSD_G):
        idx[512 + g * LANES:512 + g * LANES + SSD_N] = np.arange(512 + g * SSD_N, 512 + (g + 1) * SSD_N)
        idx[768 + g * LANES:768 + g * LANES + SSD_N] = np.arange(640 + g * SSD_N, 640 + (g + 1) * SSD_N)
    return idx


def _take_cols(a, idx):
    parts, start = [], 0
    for pos in range(1, len(idx) + 1):
        if pos == len(idx) or idx[pos] != idx[pos - 1] + (1 if idx[pos - 1] >= 0 else 0):
            n = pos - start
            if idx[start] < 0:
                parts.append(jnp.zeros(a.shape[:-1] + (n,), a.dtype))
            else:
                parts.append(a[..., int(idx[start]):int(idx[start]) + n])
            start = pos
    return jnp.concatenate(parts, axis=-1)


def _params(sem, vmem=VMEM_LIMIT):
    return pltpu.CompilerParams(dimension_semantics=sem, vmem_limit_bytes=vmem)


def _tdot(a, b, **kw):
    return lax.dot_general(a, b, (((0,), (0,)), ((), ())), preferred_element_type=F32, **kw)


def _ntdot(a, b, **kw):
    return lax.dot_general(a, b, (((1,), (1,)), ((), ())), preferred_element_type=F32, **kw)


def _dot(a, b, **kw):
    return jnp.dot(a, b, preferred_element_type=F32, **kw)


def _silu(x):
    return x * jax.nn.sigmoid(x)


def _softplus(x):
    return jnp.maximum(x, 0.0) + jnp.log1p(jnp.exp(-jnp.abs(x)))


def _iota(shape, dim):
    return lax.broadcasted_iota(jnp.int32, shape, dim)


def _mod_body(c_ref, w_ref, b_ref, o_ref):
    c = c_ref[...]
    o_ref[...] = _dot(_silu(c).astype(BF16), w_ref[...].astype(BF16)) + b_ref[...]


def _modulation(cvec, ada_w, ada_b):
    L = ada_w.shape[0]
    tn = 512
    return pl.pallas_call(
        _mod_body,
        grid=(L, 6 * D_MODEL // tn),
        in_specs=[pl.BlockSpec((16, D_MODEL), lambda l, j: (0, 0)),
                  pl.BlockSpec((None, D_MODEL, tn), lambda l, j: (l, 0, j)),
                  pl.BlockSpec((None, 1, tn), lambda l, j: (l, 0, j))],
        out_specs=pl.BlockSpec((None, 16, tn), lambda l, j: (l, 0, j)),
        out_shape=jax.ShapeDtypeStruct((L, 16, 6 * D_MODEL), F32),
        compiler_params=_params(("parallel", "parallel")),
        name="modulation",
    )(cvec, ada_w, ada_b.reshape(L, 1, 6 * D_MODEL))


def _rms_mod(x, nw, shift, scale):
    ms = jnp.mean(x * x, axis=-1, keepdims=True)
    return (x * lax.rsqrt(ms + EPS) * nw) * (1.0 + scale) + shift


def _in_proj_body(x_ref, mod_ref, nw_ref, w_ref, ws_ref, o_ref, os_ref, h_ref):
    @pl.when(pl.program_id(1) == 0)
    def _():
        h = _rms_mod(x_ref[...], nw_ref[...], mod_ref[0:1, :], mod_ref[1:2, :])
        h_ref[...] = h.astype(BF16)
        os_ref[...] = _dot(h_ref[...], ws_ref[...])

    o_ref[...] = _dot(h_ref[...], w_ref[...]).astype(o_ref.dtype)


def _mod_index(n_mod, tm, seq):
    if n_mod == 1:
        return lambda i: 0
    return lambda i: (i * tm) // seq


def _in_proj(x, mod, nw, w, w_small, tm, tn, seq):
    n_tok, n = x.shape[0], w.shape[1]
    mi = _mod_index(mod.shape[0], tm, seq)
    return pl.pallas_call(
        _in_proj_body,
        grid=(n_tok // tm, n // tn),
        in_specs=[pl.BlockSpec((tm, D_MODEL), lambda i, j: (i, 0)),
                  pl.BlockSpec((None, 8, D_MODEL), lambda i, j: (mi(i), 0, 0)),
                  pl.BlockSpec((1, D_MODEL), lambda i, j: (0, 0)),
                  pl.BlockSpec((D_MODEL, tn), lambda i, j: (0, j)),
                  pl.BlockSpec((D_MODEL, N_SMALL), lambda i, j: (0, 0))],
        out_specs=[pl.BlockSpec((tm, tn), lambda i, j: (i, j)),
                   pl.BlockSpec((tm, N_SMALL), lambda i, j: (i, 0))],
        out_shape=[jax.ShapeDtypeStruct((n_tok, n), BF16), jax.ShapeDtypeStruct((n_tok, N_SMALL), F32)],
        scratch_shapes=[pltpu.VMEM((tm, D_MODEL), BF16)],
        compiler_params=_params(("parallel", "arbitrary")),
        name="in_proj",
    )(x, mod, nw, w, w_small)


def _pair_masks():
    lane = _iota((1, LANES), 1)
    return lane < 64, lane


def _head_norm(y, eps):
    mu = jnp.mean(y, axis=-1, keepdims=True)
    d = y - mu
    var = jnp.mean(d * d, axis=-1, keepdims=True)
    return d * lax.rsqrt(var + eps)


def _blockdiag_rows(v, width):
    col = _iota((1, 2 * width), 1)
    left = col < width
    return jnp.concatenate([jnp.where(left, v, 0), jnp.where(left, 0, v)], axis=0)


def _tri_masks():
    ii = _iota((CHUNK, LANES), 0)
    jj = _iota((CHUNK, LANES), 1) % CHUNK
    return jj <= ii, jj >= ii, ii - jj


def _cumsum_mats():
    r = _iota((2 * CHUNK, CHUNK), 0)
    t = _iota((2 * CHUNK, CHUNK), 1)
    pre = (r < CHUNK) & (t <= r)
    suf = (r >= CHUNK) & (t >= r - CHUNK)
    return jnp.where(pre | suf, 1.0, 0.0).astype(F32)


def _split2(x):
    hi = x.astype(BF16)
    return hi, (x - hi.astype(F32)).astype(BF16)


def _sel_dot(sel, x):
    s = sel.astype(BF16)
    hi, lo = _split2(x)
    return _dot(s, hi) + _dot(s, lo)


def _dot_sel(x, sel):
    s = sel.astype(BF16)
    hi, lo = _split2(x)
    return _dot(hi, s) + _dot(lo, s)


def _sel_ntdot(sel, x):
    s = sel.astype(BF16)
    hi, lo = _split2(x)
    return _ntdot(s, hi) + _ntdot(s, lo)


def _seg_ones_pair():
    r = _iota((LANES, LANES), 0) // 64
    c = _iota((LANES, LANES), 1) // 64
    return jnp.where(r == c, 1.0, 0.0).astype(BF16)


def _seg_sum(x, ones):
    hi, lo = _split2(x)
    return _dot(hi, ones) + _dot(lo, ones)


def _pair_select(lane0s):
    k = len(lane0s)
    l = _iota((LANES, k * LANES), 0)
    c = _iota((LANES, k * LANES), 1)
    tgt = jnp.where((c % LANES) >= 64, 1, 0)
    for q in range(k):
        tgt = tgt + jnp.where(c // LANES == q, lane0s[q], 0)
    return jnp.where(l == tgt, 1.0, 0.0).astype(F32)


def _pair_pick(lane0s):
    r = _iota((8, LANES), 0)
    base = jnp.full((8, LANES), -8, jnp.int32)
    for q in range(len(lane0s)):
        base = jnp.where(r == q, lane0s[q], base)
    off = _iota((8, LANES), 1) - base
    return jnp.where((off == 0) | (off == 1), 1.0, 0.0).astype(F32)


def _even_odd_stack(x):
    even = (_iota((1, LANES), 1) % 2) == 0
    return jnp.concatenate([jnp.where(even, x, 0.0), jnp.where(even, 0.0, x)], axis=0)


def _ret_body(*refs, rope, has_s0, want_state, seq):
    it = iter(refs)
    lg_ref, q_ref, k_ref, v_ref, g_ref = (next(it) for _ in range(5))
    cos_ref = sin_ref = s0_ref = sn_ref = None
    if rope:
        cos_ref, sin_ref = next(it), next(it)
    if has_s0:
        s0_ref = next(it)
    y_ref = next(it)
    if want_state:
        sn_ref = next(it)
    qs, ks, sb_all = next(it), next(it), next(it)

    C = CHUNK
    n = seq // C
    hp = pl.program_id(1)
    m0, lane = _pair_masks()
    lgf0, lgf1 = lg_ref[0, 2 * hp], lg_ref[0, 2 * hp + 1]
    lgb0, lgb1 = lg_ref[1, 2 * hp], lg_ref[1, 2 * hp + 1]
    lgf = jnp.where(m0, lgf0, lgf1)
    lgb = jnp.where(m0, lgb0, lgb1)
    pos = _iota((C, 1), 0).astype(F32)
    e_qf = jnp.exp(lgf * (pos + 1.0))
    e_qb = jnp.exp(lgb * (C - pos))
    e_kf = jnp.exp(lgf * (C - 1.0 - pos))
    e_kb = jnp.exp(lgb * pos)
    row0 = _iota((LANES, 1), 0) < 64
    dec_f = jnp.exp(jnp.where(row0, lgf0, lgf1) * C)
    dec_b = jnp.exp(jnp.where(row0, lgb0, lgb1) * C)
    bd = (_iota((LANES, 2 * LANES), 0) < 64) == (_iota((LANES, 2 * LANES), 1) < LANES)
    _, _, diff = _tri_masks()
    dif = diff.astype(F32)
    dcomb = jnp.where(diff > 0, jnp.exp(lgf * jnp.maximum(dif, 0.0)),
                      jnp.where(diff < 0, jnp.exp(lgb * jnp.maximum(-dif, 0.0)), 2.0))

    rb = min(seq, 512)
    m32 = (lane % 64) < 32

    def swap(x):
        return jnp.where(m32, pltpu.roll(x, 96, 1), pltpu.roll(x, 32, 1))

    def prep(i, carry):
        r0 = pl.multiple_of(i * rb, rb)
        q = q_ref[pl.ds(r0, rb), :].astype(F32)
        k = k_ref[pl.ds(r0, rb), :].astype(F32)
        if rope:
            cs = cos_ref[pl.ds(r0, rb), :]
            sn = sin_ref[pl.ds(r0, rb), :]
            q = q * cs + swap(q) * sn
            k = k * cs + swap(k) * sn
        qs[pl.ds(r0, rb), :] = (q * RET_DK ** -0.5).astype(BF16)
        ks[pl.ds(r0, rb), :] = k.astype(BF16)
        return carry

    lax.fori_loop(0, seq // rb, prep, 0, unroll=min(PREP_UNROLL, seq // rb))

    def bd_state(d):
        if not has_s0:
            return jnp.zeros((LANES, 2 * LANES), F32)
        z = jnp.zeros((RET_DK, RET_DV), F32)
        return jnp.concatenate([jnp.concatenate([s0_ref[d, 0], z], axis=1),
                                jnp.concatenate([z, s0_ref[d, 1]], axis=1)], axis=0)

    nch = min(CHUNKS_PER_STEP, n)

    def bsweep(i, sb):
        kvs = []
        for q in range(nch):
            c = n - 1 - (i * nch + q)
            rows = pl.ds(pl.multiple_of(c * C, C), C)
            kb = (ks[rows, :].astype(F32) * e_kb).astype(BF16)
            kvs.append((c, _tdot(kb, v_ref[rows, :])))
        for c, kv in kvs:
            sb_all[c] = sb.astype(BF16)
            sb = dec_b * sb + jnp.where(bd, kv, 0.0)
        return sb

    sb_fin = lax.fori_loop(0, n // nch, bsweep, bd_state(1))

    def fsweep(i, sf):
        jobs = []
        for q in range(nch):
            c = i * nch + q
            rows = pl.ds(pl.multiple_of(c * C, C), C)
            qc, kc, vc = qs[rows, :], ks[rows, :], v_ref[rows, :]
            k2 = jnp.concatenate([jnp.where(m0, kc, 0), jnp.where(m0, 0, kc)], axis=0)
            kf = (kc.astype(F32) * e_kf).astype(BF16)
            jobs.append(dict(c=c, rows=rows, qc=qc, vc=vc, s2=_ntdot(qc, k2), kv=_tdot(kf, vc)))
        for j in jobs:
            rows, qf = j["rows"], j["qc"].astype(F32)
            lhs = jnp.concatenate([(j["s2"] * dcomb).astype(BF16), (qf * e_qf).astype(BF16),
                                   (qf * e_qb).astype(BF16)], axis=1)
            rhs = jnp.concatenate([_blockdiag_rows(j["vc"], LANES), sf.astype(BF16), sb_all[j["c"]]], axis=0)
            y = _dot(lhs, rhs)
            y = jnp.concatenate([_head_norm(y[:, :LANES], GN_EPS), _head_norm(y[:, LANES:], GN_EPS)], axis=1)
            y_ref[rows, :] = (_silu(g_ref[rows, :].astype(F32)) * y).astype(y_ref.dtype)
            sf = dec_f * sf + jnp.where(bd, j["kv"], 0.0)
        return sf

    sf_fin = lax.fori_loop(0, n // nch, fsweep, bd_state(0))

    if want_state:
        for d, s in ((0, sf_fin), (1, sb_fin)):
            sn_ref[d, 0] = s[:RET_DK, :RET_DV]
            sn_ref[d, 1] = s[RET_DK:, RET_DV:]


def _retention(lg, p16, cos, sin, s0, n_batch, seq, want_state):
    rope = cos is not None
    has_s0 = s0 is not None
    n_tok = n_batch * seq
    in_specs = [pl.BlockSpec(memory_space=pltpu.SMEM),
                pl.BlockSpec((seq, LANES), lambda b, p: (b, M_RQ // LANES + p)),
                pl.BlockSpec((seq, LANES), lambda b, p: (b, M_RK // LANES + p)),
                pl.BlockSpec((seq, 2 * LANES), lambda b, p: (b, M_RV // 256 + p)),
                pl.BlockSpec((seq, 2 * LANES), lambda b, p: (b, M_RG // 256 + p))]
    args = [lg, p16, p16, p16, p16]
    if rope:
        in_specs += [pl.BlockSpec((seq, LANES), lambda b, p: (0, 0))] * 2
        args += [cos, sin]
    st_spec = pl.BlockSpec((None, 2, 2, RET_DK, RET_DV), lambda b, p: (b, 0, p, 0, 0))
    if has_s0:
        in_specs.append(st_spec)
        args.append(s0)
    out_specs = [pl.BlockSpec((seq, 2 * LANES), lambda b, p: (b, p))]
    out_shape = [jax.ShapeDtypeStruct((n_tok, MIX_W), BF16)]
    if want_state:
        out_specs.append(st_spec)
        out_shape.append(jax.ShapeDtypeStruct((n_batch, 2, RET_H, RET_DK, RET_DV), F32))
    res = pl.pallas_call(
        functools.partial(_ret_body, rope=rope, has_s0=has_s0, want_state=want_state, seq=seq),
        grid=(n_batch, RET_H // 2),
        in_specs=in_specs, out_specs=out_specs, out_shape=out_shape,
        scratch_shapes=[pltpu.VMEM((seq, LANES), BF16), pltpu.VMEM((seq, LANES), BF16),
                        pltpu.VMEM((seq // CHUNK, LANES, 2 * LANES), BF16)],
        compiler_params=_params(("parallel", "parallel")),
        name="retention",
    )(*args)
    return res if want_state else (res[0], None)


def _ml_chunk_gate(bcol, icol, kc, n_x, m_x, reverse):
    C = CHUNK
    b_end = bcol[0:1] if reverse else bcol[C - 1:C]
    lwe = b_end - bcol + icol
    m_new = jnp.maximum(b_end + m_x, jnp.max(lwe, axis=0, keepdims=True))
    scale = jnp.exp(b_end + m_x - m_new)
    kw = kc.astype(F32) * jnp.exp(lwe - m_new)
    n_new = scale * n_x + jnp.sum(kw, axis=0, keepdims=True)
    return kw.astype(BF16), scale, n_new, m_new


def _chunk_cummax(x, reverse):
    rb = x.shape[0]
    row = _iota((rb, 1), 0) % CHUNK
    s = 1
    while s < CHUNK:
        if reverse:
            shifted, ok = pltpu.roll(x, rb - s, 0), row < CHUNK - s
        else:
            shifted, ok = pltpu.roll(x, s, 0), row >= s
        x = jnp.maximum(x, jnp.where(ok, shifted, -jnp.inf))
        s *= 2
    return x


def _ml_dir_att(s2, qf, bcol, brow, irow, mx, n_x, m_x, mask, ones):
    logw = jnp.where(mask, bcol - brow + irow, -jnp.inf)
    inter = bcol + m_x
    mt = jnp.maximum(inter, mx)
    att = s2 * jnp.exp(logw - mt)
    return dict(att=att, a_in=jnp.exp(inter - mt), mt=mt, rs=_seg_sum(att, ones), qs=_seg_sum(qf * n_x, ones))


def _ml_dir_lhs(t, qf):
    den = t["rs"] + t["a_in"] * t["qs"]
    sc = 1.0 / jnp.maximum(jnp.abs(den), jnp.exp(-t["mt"]))
    return (t["att"] * sc).astype(BF16), (qf * (t["a_in"] * sc)).astype(BF16)


def _ml_body(*refs, has_s0, want_state, seq):
    it = iter(refs)
    q_ref, k_ref, v_ref, o_ref, gt_ref, gb_ref, nw_ref = (next(it) for _ in range(7))
    s0_ref = nm0_ref = sn_ref = nmn_ref = None
    if has_s0:
        s0_ref, nm0_ref = next(it), next(it)
    y_ref = next(it)
    if want_state:
        sn_ref, nmn_ref = next(it), next(it)
    bcf_s, bcb_s, icf_s, icb_s, mxf_s, mxb_s, rows_s, cb_all, nmb_all = (next(it) for _ in range(9))
    ones = _seg_ones_pair()

    C = CHUNK
    n = seq // C
    nch = min(CHUNKS_PER_STEP, n)
    hp = pl.program_id(1)
    m0, _ = _pair_masks()
    row0 = _iota((LANES, 1), 0) < 64
    bd = (_iota((LANES, 2 * LANES), 0) < 64) == (_iota((LANES, 2 * LANES), 1) < LANES)
    tril, triu, _ = _tri_masks()
    cum = _cumsum_mats()
    gbias = gb_ref[...]
    lanes = [20 + 2 * hp, 28 + 2 * hp, 16 + 2 * hp, 24 + 2 * hp]
    sel = _pair_select(lanes)
    pick = _pair_pick(lanes)
    rb = min(seq, 256)

    def prep(i, carry):
        r0 = pl.multiple_of(i * rb, rb)
        rows = pl.ds(r0, rb)
        g = gt_ref[rows, :] + gbias
        lf = -_softplus(-g)
        cs2 = [_sel_dot(cum, lf[q * C:(q + 1) * C]) for q in range(rb // C)]
        pre = jnp.concatenate([c2[:C] for c2 in cs2], axis=0)
        suf = jnp.concatenate([c2[C:] for c2 in cs2], axis=0)
        e = _dot_sel(jnp.concatenate([pre, suf, g], axis=0), sel)
        bcf, bcb = e[0:rb, 0:LANES], e[rb:2 * rb, LANES:2 * LANES]
        icf, icb = e[2 * rb:, 2 * LANES:3 * LANES], e[2 * rb:, 3 * LANES:]
        bcf_s[rows, :] = bcf
        bcb_s[rows, :] = bcb
        icf_s[rows, :] = icf
        icb_s[rows, :] = icb
        mxf_s[rows, :] = bcf + _chunk_cummax(icf - bcf, False)
        mxb_s[rows, :] = bcb + _chunk_cummax(icb - bcb, True)
        for q in range(rb // C):
            st = jnp.concatenate([_even_odd_stack(cs2[q][:C]), _even_odd_stack(cs2[q][C:]),
                                  _even_odd_stack(g[q * C:(q + 1) * C])], axis=0)
            o = _sel_ntdot(pick, st)
            rows_s[i * (rb // C) + q] = jnp.concatenate(
                [o[0:1, 0:LANES], o[1:2, LANES:2 * LANES], o[2:3, 2 * LANES:], o[3:4, 2 * LANES:],
                 jnp.zeros((4, LANES), F32)], axis=0)
        return carry

    lax.fori_loop(0, seq // rb, prep, 0, unroll=min(PREP_UNROLL, seq // rb))

    def row_scale(scale):
        return jnp.where(row0, scale[:, 0:1], scale[:, 64:65])

    def init(d):
        if not has_s0:
            return (jnp.zeros((LANES, 2 * LANES), F32), jnp.zeros((1, LANES), F32), jnp.zeros((1, LANES), F32))
        z = jnp.zeros((ML_DK, ML_DV), F32)
        cst = jnp.concatenate([jnp.concatenate([s0_ref[d, 0], z], axis=1),
                               jnp.concatenate([z, s0_ref[d, 1]], axis=1)], axis=0)
        return cst, nm0_ref[d:d + 1, :], nm0_ref[2 + d:3 + d, :]

    def bsweep(i, carry):
        cst, n_x, m_x = carry
        terms = []
        for q in range(nch):
            c = n - 1 - (i * nch + q)
            rows = pl.ds(pl.multiple_of(c * C, C), C)
            kw, scale, n_new, m_new = _ml_chunk_gate(bcb_s[rows, :], icb_s[rows, :], k_ref[rows, :], n_x, m_x, True)
            terms.append((c, n_x, m_x, scale, _tdot(kw, v_ref[rows, :])))
            n_x, m_x = n_new, m_new
        for c, n_in, m_in, scale, kv in terms:
            cb_all[c] = cst.astype(BF16)
            nmb_all[c, 0:1, :] = n_in
            nmb_all[c, 1:2, :] = m_in
            cst = row_scale(scale) * cst + jnp.where(bd, kv, 0.0)
        return cst, n_x, m_x

    cb_fin = lax.fori_loop(0, n // nch, bsweep, init(1))

    def fsweep(i, carry):
        cst, n_x, m_x = carry
        jobs = []
        for q in range(nch):
            c = i * nch + q
            rows = pl.ds(pl.multiple_of(c * C, C), C)
            kc = k_ref[rows, :]
            vc = v_ref[rows, :]
            qf = q_ref[rows, :].astype(F32) * ML_DK ** -0.5
            k2 = jnp.concatenate([jnp.where(m0, kc, 0), jnp.where(m0, 0, kc)], axis=0)
            bcf = bcf_s[rows, :]
            kw, scale, n_new, m_new = _ml_chunk_gate(bcf, icf_s[rows, :], kc, n_x, m_x, False)
            jobs.append(dict(c=c, rows=rows, qf=qf, vc=vc, bcf=bcf, n_in=n_x, m_in=m_x, scale=scale,
                             s2=_ntdot(qf.astype(BF16), k2), kv=_tdot(kw, vc)))
            n_x, m_x = n_new, m_new
        for j in jobs:
            c, rows, s2, qf = j["c"], j["rows"], j["s2"], j["qf"]
            rw = rows_s[c]
            j["tf"] = _ml_dir_att(s2, qf, j["bcf"], rw[0:1], rw[2:3], mxf_s[rows, :], j["n_in"], j["m_in"], tril, ones)
            j["tb"] = _ml_dir_att(s2, qf, bcb_s[rows, :], rw[1:2], rw[3:4], mxb_s[rows, :], nmb_all[c, 0:1, :],
                                  nmb_all[c, 1:2, :], triu, ones)
        for j in jobs:
            c, rows, qf = j["c"], j["rows"], j["qf"]
            af, qaf = _ml_dir_lhs(j["tf"], qf)
            ab, qab = _ml_dir_lhs(j["tb"], qf)
            vbd = _blockdiag_rows(j["vc"], LANES)
            lhs = jnp.concatenate([af, qaf, ab, qab], axis=1)
            rhs = jnp.concatenate([vbd, cst.astype(BF16), vbd, cb_all[c]], axis=0)
            y = _dot(lhs, rhs)
            y = jnp.concatenate([_head_norm(y[:, :LANES], GN_EPS), _head_norm(y[:, LANES:], GN_EPS)], axis=1)
            y = jax.nn.sigmoid(o_ref[rows, :].astype(F32)) * (y * nw_ref[...])
            y_ref[rows, :] = y.astype(y_ref.dtype)
            cst = row_scale(j["scale"]) * cst + jnp.where(bd, j["kv"], 0.0)
        return cst, n_x, m_x

    cf_fin = lax.fori_loop(0, n // nch, fsweep, init(0))

    if want_state:
        nmn_ref[...] = jnp.zeros((8, LANES), F32)
        for d, (cst, n_x, m_x) in ((0, cf_fin), (1, cb_fin)):
            sn_ref[d, 0] = cst[:ML_DK, :ML_DV]
            sn_ref[d, 1] = cst[ML_DK:, ML_DV:]
            nmn_ref[d:d + 1, :] = n_x
            nmn_ref[2 + d:3 + d, :] = m_x


def _mlstm(p16, p32, gbias, nw, s0, nm0, n_batch, seq, want_state):
    has_s0 = s0 is not None
    n_tok = n_batch * seq
    in_specs = [pl.BlockSpec((seq, LANES), lambda b, p: (b, M_MQ // LANES + p)),
                pl.BlockSpec((seq, LANES), lambda b, p: (b, M_MK // LANES + p)),
                pl.BlockSpec((seq, 2 * LANES), lambda b, p: (b, M_MV // 256 + p)),
                pl.BlockSpec((seq, 2 * LANES), lambda b, p: (b, M_MO // 256 + p)),
                pl.BlockSpec((seq, LANES), lambda b, p: (b, S_DT // LANES)),
                pl.BlockSpec((1, LANES), lambda b, p: (0, 0)),
                pl.BlockSpec((1, 2 * LANES), lambda b, p: (0, p))]
    args = [p16, p16, p16, p16, p32, gbias, nw]
    st_spec = pl.BlockSpec((None, 2, 2, ML_DK, ML_DV), lambda b, p: (b, 0, p, 0, 0))
    nm_spec = pl.BlockSpec((None, None, 8, LANES), lambda b, p: (b, p, 0, 0))
    if has_s0:
        in_specs += [st_spec, nm_spec]
        args += [s0, nm0]
    out_specs = [pl.BlockSpec((seq, 2 * LANES), lambda b, p: (b, p))]
    out_shape = [jax.ShapeDtypeStruct((n_tok, MIX_W), BF16)]
    if want_state:
        out_specs += [st_spec, nm_spec]
        out_shape += [jax.ShapeDtypeStruct((n_batch, 2, ML_H, ML_DK, ML_DV), F32),
                      jax.ShapeDtypeStruct((n_batch, ML_H // 2, 8, LANES), F32)]
    res = pl.pallas_call(
        functools.partial(_ml_body, has_s0=has_s0, want_state=want_state, seq=seq),
        grid=(n_batch, ML_H // 2),
        in_specs=in_specs, out_specs=out_specs, out_shape=out_shape,
        scratch_shapes=[pltpu.VMEM((seq, LANES), F32)] * 6
        + [pltpu.VMEM((seq // CHUNK, 8, LANES), F32), pltpu.VMEM((seq // CHUNK, LANES, 2 * LANES), BF16),
           pltpu.VMEM((seq // CHUNK, 8, LANES), F32)],
        compiler_params=_params(("parallel", "parallel")),
        name="mlstm",
    )(*args)
    return res if want_state else (res[0], None, None)


def _ml_pack_nm(st_n, st_m):
    B = st_n.shape[0]
    n = st_n.reshape(B, 2, ML_H // 2, 2 * ML_DK).transpose(0, 2, 1, 3)
    m = jnp.repeat(st_m.reshape(B, 2, ML_H // 2, 2), ML_DK, axis=-1).transpose(0, 2, 1, 3)
    return jnp.concatenate([n, m, jnp.zeros((B, ML_H // 2, 4, LANES), F32)], axis=2)


def _ml_unpack_nm(nm):
    B = nm.shape[0]
    n = nm[:, :, 0:2, :].transpose(0, 2, 1, 3).reshape(B, 2, ML_H, ML_DK)
    m = nm[:, :, 2:4, :].transpose(0, 2, 1, 3).reshape(B, 2, ML_H, ML_DK)[..., 0]
    return n, m


def _conv3_rows(ref, r0, rb, seq, w, bias):
    x = ref[pl.ds(r0, rb), :].astype(F32)
    prev = ref[pl.ds(pl.multiple_of(jnp.maximum(r0 - 16, 0), 16), 16), :].astype(F32)[15:16]
    nxt = ref[pl.ds(pl.multiple_of(jnp.minimum(r0 + rb, seq - 16), 16), 16), :].astype(F32)[0:1]
    prev = jnp.where(r0 == 0, 0.0, prev)
    nxt = jnp.where(r0 + rb == seq, 0.0, nxt)
    row = _iota((rb, 1), 0)
    x_dn = jnp.where(row == 0, prev, pltpu.roll(x, 1, 0))
    x_up = jnp.where(row == rb - 1, nxt, pltpu.roll(x, rb - 1, 0))
    y = w[0:1] * x_dn + w[1:2] * x + w[2:3] * x_up
    return y if bias is None else y + bias


def _ssd_body(*refs, has_s0, want_state, seq):
    it = iter(refs)
    (x_ref, b_ref, c_ref, z_ref, dt_ref, wx_ref, wb_ref, wc_ref, bx_ref, bb_ref, bc_ref,
     alog_ref, dtb_ref, dd_ref) = (next(it) for _ in range(14))
    s0_ref = sn_ref = None
    if has_s0:
        s0_ref = next(it)
    y_ref = next(it)
    if want_state:
        sn_ref = next(it)
    xs, bs, cs_, ccf_s, ccb_s, dcf_s, dcb_s, rows_s, sb_all = (next(it) for _ in range(9))

    C = CHUNK
    n = seq // C
    nch = min(CHUNKS_PER_STEP, n)
    p = pl.program_id(1)
    tril, triu, _ = _tri_masks()
    a_lane = -jnp.exp(alog_ref[...])
    dsum = dd_ref[0:1, :] + dd_ref[1:2, :]
    rb = min(seq, 256)
    cum = _cumsum_mats()
    sel = _pair_select([2 * p, 8 + 2 * p])
    pick = _pair_pick([2 * p, 8 + 2 * p])

    def prep(i, carry):
        r0 = pl.multiple_of(i * rb, rb)
        rows = pl.ds(r0, rb)
        xs[rows, :] = _silu(_conv3_rows(x_ref, r0, rb, seq, wx_ref[...], bx_ref[...]))
        bs[rows, :] = _silu(_conv3_rows(b_ref, r0, rb, seq, wb_ref[...], bb_ref[...])).astype(BF16)
        cs_[rows, :] = _silu(_conv3_rows(c_ref, r0, rb, seq, wc_ref[...], bc_ref[...])).astype(BF16)
        dt = _softplus(dt_ref[rows, :] + dtb_ref[...])
        lw = dt * a_lane
        cs2 = [_sel_dot(cum, lw[q * C:(q + 1) * C]) for q in range(rb // C)]
        pre = jnp.concatenate([c2[:C] for c2 in cs2], axis=0)
        suf = jnp.concatenate([c2[C:] for c2 in cs2], axis=0)
        e = _dot_sel(jnp.concatenate([pre, suf, dt], axis=0), sel)
        ccf_s[rows, :] = e[0:rb, :LANES]
        ccb_s[rows, :] = e[rb:2 * rb, LANES:]
        dcf_s[rows, :] = e[2 * rb:, :LANES]
        dcb_s[rows, :] = e[2 * rb:, LANES:]
        for q in range(rb // C):
            st = jnp.concatenate([_even_odd_stack(cs2[q][:C]), _even_odd_stack(cs2[q][C:]),
                                  _even_odd_stack(dt[q * C:(q + 1) * C])], axis=0)
            o = _sel_ntdot(pick, st)
            rows_s[i * (rb // C) + q] = jnp.concatenate(
                [o[0:1, 0:LANES], o[1:2, LANES:2 * LANES], o[0:1, 2 * LANES:], o[1:2, 2 * LANES:],
                 jnp.zeros((4, LANES), F32)], axis=0)
        return carry

    lax.fori_loop(0, seq // rb, prep, 0, unroll=min(PREP_UNROLL, seq // rb))

    def state0(d):
        if not has_s0:
            return jnp.zeros((LANES, LANES), F32)
        return jnp.concatenate([s0_ref[d], jnp.zeros((SSD_N, LANES), F32)], axis=0)

    def state_update_terms(rows, ccol_s, dcol_s, rev):
        ccol = ccol_s[rows, :]
        last = ccol[0:1] if rev else ccol[C - 1:C]
        vdt = (xs[rows, :] * dcol_s[rows, :] * jnp.exp(last - ccol)).astype(BF16)
        return jnp.exp(last), _tdot(bs[rows, :], vdt)

    def bsweep(i, sb):
        terms = []
        for q in range(nch):
            c = n - 1 - (i * nch + q)
            rows = pl.ds(pl.multiple_of(c * C, C), C)
            terms.append((c,) + state_update_terms(rows, ccb_s, dcb_s, True))
        for c, dec, kv in terms:
            sb_all[c] = sb[:SSD_N].astype(BF16)
            sb = dec * sb + kv
        return sb

    sb_fin = lax.fori_loop(0, n // nch, bsweep, state0(1))

    def fsweep(i, sf):
        jobs = []
        for q in range(nch):
            c = i * nch + q
            rows = pl.ds(pl.multiple_of(c * C, C), C)
            j = dict(c=c, rows=rows, xc=xs[rows, :], bc=bs[rows, :], cc=cs_[rows, :],
                     ccf=ccf_s[rows, :], ccb=ccb_s[rows, :])
            rw = rows_s[c]
            j["m"] = (jnp.where(tril, jnp.exp(jnp.minimum(j["ccf"] - rw[0:1], 0.0)) * rw[2:3], 0.0)
                      + jnp.where(triu, jnp.exp(jnp.minimum(j["ccb"] - rw[1:2], 0.0)) * rw[3:4], 0.0))
            j["s2"] = _ntdot(j["cc"], jnp.concatenate([j["bc"], j["bc"]], axis=0))
            jobs.append(j)
        for j in jobs:
            att = (j["s2"] * j["m"]).astype(BF16)
            sbc = jnp.concatenate([sb_all[j["c"]], jnp.zeros((SSD_N, LANES), BF16)], axis=0)
            j["yi"] = _dot(att, _blockdiag_rows(j["xc"], 64).astype(BF16))
            j["ysb"] = _dot(j["cc"], sbc)
            j["dec"], j["kv"] = state_update_terms(j["rows"], ccf_s, dcf_s, False)
        for j in jobs:
            y = (j["yi"] + _dot(j["cc"], sf.astype(BF16)) * jnp.exp(j["ccf"]) + j["ysb"] * jnp.exp(j["ccb"])
                 + dsum * j["xc"])
            y_ref[j["rows"], :] = (y * _silu(z_ref[j["rows"], :].astype(F32))).astype(y_ref.dtype)
            sf = j["dec"] * sf + j["kv"]
        return sf

    sf_fin = lax.fori_loop(0, n // nch, fsweep, state0(0))
    if want_state:
        sn_ref[0] = sf_fin[:SSD_N]
        sn_ref[1] = sb_fin[:SSD_N]


def _ssd(p16, p32, conv_w, conv_b, alog, dtb, dd, s0, n_batch, seq, want_state):
    has_s0 = s0 is not None
    n_tok = n_batch * seq
    col = lambda off: (lambda b, p: (b, off // LANES + p))
    grp = lambda off: (lambda b, p: (b, off // LANES + p // 2))
    wcol = lambda off: (lambda b, p: (0, off + p))
    wgrp = lambda off: (lambda b, p: (0, off + p // 2))
    in_specs = [pl.BlockSpec((seq, LANES), col(M_SX)), pl.BlockSpec((seq, LANES), grp(M_SB)),
                pl.BlockSpec((seq, LANES), grp(M_SC)), pl.BlockSpec((seq, LANES), col(M_SZ)),
                pl.BlockSpec((seq, LANES), lambda b, p: (b, S_DT // LANES)),
                pl.BlockSpec((3, LANES), wcol(0)), pl.BlockSpec((3, LANES), wgrp(4)), pl.BlockSpec((3, LANES), wgrp(6)),
                pl.BlockSpec((1, LANES), wcol(0)), pl.BlockSpec((1, LANES), wgrp(4)), pl.BlockSpec((1, LANES), wgrp(6)),
                pl.BlockSpec((1, LANES), lambda b, p: (0, 0)), pl.BlockSpec((1, LANES), lambda b, p: (0, 0)),
                pl.BlockSpec((2, LANES), wcol(0))]
    args = [p16, p16, p16, p16, p32, conv_w, conv_w, conv_w, conv_b, conv_b, conv_b, alog, dtb, dd]
    st_spec = pl.BlockSpec((None, 2, None, SSD_N, LANES), lambda b, p: (b, 0, p, 0, 0))
    if has_s0:
        in_specs.append(st_spec)
        args.append(s0)
    out_specs = [pl.BlockSpec((seq, LANES), lambda b, p: (b, p))]
    out_shape = [jax.ShapeDtypeStruct((n_tok, MIX_W), BF16)]
    if want_state:
        out_specs.append(st_spec)
        out_shape.append(jax.ShapeDtypeStruct((n_batch, 2, SSD_H // 2, SSD_N, LANES), F32))
    res = pl.pallas_call(
        functools.partial(_ssd_body, has_s0=has_s0, want_state=want_state, seq=seq),
        grid=(n_batch, SSD_H // 2),
        in_specs=in_specs, out_specs=out_specs, out_shape=out_shape,
        scratch_shapes=[pltpu.VMEM((seq, LANES), F32), pltpu.VMEM((seq, LANES), BF16), pltpu.VMEM((seq, LANES), BF16)]
        + [pltpu.VMEM((seq, LANES), F32)] * 4
        + [pltpu.VMEM((seq // CHUNK, 8, LANES), F32), pltpu.VMEM((seq // CHUNK, SSD_N, LANES), BF16)],
        compiler_params=_params(("parallel", "parallel")),
        name="ssd",
    )(*args)
    return res if want_state else (res[0], None)


def _ssd_pack_state(st):
    B = st.shape[0]
    return st.reshape(B, 2, SSD_H // 2, 2, SSD_N, SSD_P).transpose(0, 1, 2, 4, 3, 5).reshape(B, 2, SSD_H // 2, SSD_N, 2 * SSD_P)


def _ssd_unpack_state(st):
    B = st.shape[0]
    return st.reshape(B, 2, SSD_H // 2, SSD_N, 2, SSD_P).transpose(0, 1, 2, 4, 3, 5).reshape(B, 2, SSD_H, SSD_N, SSD_P)


def _b16(x):
    return x.astype(BF16)


def _rwkv_chunk_stage_a(job, cum, refs):
    d, r0, rev = job["d"], job["r0"], job["rev"]
    rs, vs, as_, lws, kds, bs = refs
    C = CHUNK
    lw = lws[d, pl.ds(r0, C), :]
    g = _sel_dot(cum[C:] if rev else cum[:C], lw)
    g_end = g[0:1] if rev else g[C - 1:C]
    r, a = rs[pl.ds(r0, C), :], as_[pl.ds(r0, C), :]
    kd, b = kds[d, pl.ds(r0, C), :], bs[d, pl.ds(r0, C), :]
    e_g, e_ng, e_end = jnp.exp(g), jnp.exp(-g), jnp.exp(g_end - g)
    job.update(v=vs[pl.ds(r0, C), :], at=a * jnp.exp(g - lw), rt=r * e_g, bt=b * e_ng, kt=kd * e_ng,
               bh=b * e_end, kh=kd * e_end, dec=jnp.exp(g_end))


def _rwkv_chunk_stage_b(job, m0, strict_f, incl_f, strict_b, incl_b):
    C = CHUNK
    strict, incl = (strict_b, incl_b) if job["rev"] else (strict_f, incl_f)
    lhs = _b16(jnp.concatenate([job["at"], job["rt"]], axis=0))
    bt, kt = job["bt"], job["kt"]
    rhs = _b16(jnp.concatenate([jnp.where(m0, bt, 0.0), jnp.where(m0, 0.0, bt),
                                jnp.where(m0, kt, 0.0), jnp.where(m0, 0.0, kt)], axis=0))
    a4 = _ntdot(lhs, rhs)
    job.update(lab=jnp.where(strict, a4[:C, :LANES], 0.0), lak=jnp.where(strict, a4[:C, LANES:], 0.0),
               mrb=jnp.where(incl, a4[C:, :LANES], 0.0), mrk=jnp.where(incl, a4[C:, LANES:], 0.0))


RWKV_INV_BASE = 8


def _bd16(x):
    return _b16(_blockdiag_rows(x, 64))


def _rwkv_chunk_inverse(jobs):
    i = _iota((CHUNK, LANES), 0)
    c = _iota((CHUNK, LANES), 1) % CHUNK
    diag = (i // RWKV_INV_BASE) == (c // RWKV_INV_BASE)
    eye = jnp.where(i == c, 1.0, 0.0)
    for j in jobs:
        j["pw"] = jnp.where(diag, j["lab"], 0.0)
        j["tm"] = eye + j["pw"]
    s = 2
    while s < RWKV_INV_BASE:
        for j in jobs:
            j["pw"] = _dot(_b16(j["pw"]), _bd16(j["pw"]))
        for j in jobs:
            j["tm"] = j["tm"] + _dot(_b16(j["tm"]), _bd16(j["pw"]))
        s *= 2
    s = RWKV_INV_BASE
    while s < CHUNK:
        off = ((i // (2 * s)) == (c // (2 * s))) & ((i // s) != (c // s))
        for j in jobs:
            j["tl"] = _dot(_b16(j["tm"]), _bd16(jnp.where(off, j["lab"], 0.0)))
        for j in jobs:
            j["tm"] = j["tm"] + _dot(_b16(j["tl"]), _bd16(j["tm"]))
        s *= 2


def _rwkv_chunk_stage_c(job):
    v_rows = _blockdiag_rows(job["v"], 64)
    wv = _dot(_b16(job["lak"]), _b16(v_rows))
    job.update(v_rows=v_rows, x_rows=jnp.concatenate([_bd16(job["at"]), _bd16(wv)], axis=1))


def _rwkv_chunk_stage_d(job):
    tx = _dot(_b16(job["tm"]), job["x_rows"])
    job.update(ah=tx[:, :LANES], uv=tx[:, LANES:])


def _rwkv_chunk_stage_e(job, m0, eye):
    C = CHUNK
    ah, uv = job["ah"], job["uv"]
    z = jnp.zeros((2 * C, LANES), BF16)
    rhs_m = jnp.concatenate([jnp.concatenate([_bd16(ah), _bd16(uv)], axis=1),
                             jnp.concatenate([z, _b16(job["v_rows"])], axis=1)], axis=0)
    my = _dot(_b16(jnp.concatenate([job["mrb"], job["mrk"]], axis=1)), rhs_m)
    zc = jnp.zeros((C, LANES), F32)
    rhs_g = jnp.concatenate([jnp.concatenate([ah, uv], axis=1),
                             jnp.concatenate([zc, job["v"]], axis=1)], axis=0)
    gh = _tdot(_b16(jnp.concatenate([job["bh"], job["kh"]], axis=0)), _b16(rhs_g))
    job.update(rh=job["rt"] + my[:, :LANES], yv=my[:, LANES:],
               gm=jnp.where(m0, gh[:C, :LANES], gh[C:, :LANES]) + jnp.where(eye, job["dec"], 0.0),
               hm=jnp.where(m0, gh[:C, LANES:], gh[C:, LANES:]))


def _rwkv_chunk_stage_f(job, p):
    C = CHUNK
    out = _dot(_b16(jnp.concatenate([job["rh"], job["gm"]], axis=0)), _bd16(p))
    return out[:C] + job["yv"], out[C:] + job["hm"]


def _rwkv_body(*refs, has_s0, want_state, seq):
    it = iter(refs)
    (r_ref, k_ref, v_ref, lora_ref, g_ref, cwr_ref, cwk_ref, cwv_ref, w0_ref, wup_ref, a0_ref, aup_ref,
     kkw_ref, kaw_ref, rkw_ref, gup_ref, lnw_ref, lnb_ref) = (next(it) for _ in range(18))
    s0_ref = sn_ref = None
    if has_s0:
        s0_ref = next(it)
    y_ref = next(it)
    if want_state:
        sn_ref = next(it)
    rs, vs, as_, lws, kds, bs, yf, yb = (next(it) for _ in range(8))

    C = CHUNK
    n = seq // C
    m0, _ = _pair_masks()
    ones = _seg_ones_pair()
    rb = min(seq, 256)

    def prep(i, carry):
        r0 = pl.multiple_of(i * rb, rb)
        r_ = _conv3_rows(r_ref, r0, rb, seq, cwr_ref[...], None)
        k_ = _conv3_rows(k_ref, r0, rb, seq, cwk_ref[...], None)
        rs[pl.ds(r0, rb), :] = r_
        vs[pl.ds(r0, rb), :] = _conv3_rows(v_ref, r0, rb, seq, cwv_ref[...], None)
        kk = k_ * kkw_ref[...]
        kk = kk / jnp.maximum(jnp.sqrt(_seg_sum(kk * kk, ones)), 1e-12)
        as_[pl.ds(r0, rb), :] = -kk
        lora = lora_ref[pl.ds(r0, rb), :]
        w_low, a_low = _b16(jnp.tanh(lora)), _b16(lora)
        for d in range(2):
            w_log = -_softplus(-(w0_ref[d:d + 1, :] + _dot(w_low, wup_ref[d]))) - 0.5
            lws[d, pl.ds(r0, rb), :] = -jnp.exp(w_log)
            iclr = jax.nn.sigmoid(a0_ref[d:d + 1, :] + _dot(a_low, aup_ref[d]))
            kds[d, pl.ds(r0, rb), :] = k_ * (1.0 + (iclr - 1.0) * kaw_ref[...])
            bs[d, pl.ds(r0, rb), :] = kk * iclr
        return carry

    lax.fori_loop(0, seq // rb, prep, 0, unroll=min(PREP_UNROLL, seq // rb))

    cum = _cumsum_mats()
    ii = _iota((C, LANES), 0)
    jj = _iota((C, LANES), 1) % C
    strict_f, incl_f, strict_b, incl_b = jj < ii, jj <= ii, jj > ii, jj >= ii
    eye_cat = ii == jj
    scr = (rs, vs, as_, lws, kds, bs)

    nch = min(RWKV_CHUNKS_PER_STEP, n)

    def sweep(i, carry):
        pf, pb = carry
        jobs = []
        for q in range(nch):
            jobs.append(dict(d=0, rev=False, r0=pl.multiple_of((i * nch + q) * C, C)))
            jobs.append(dict(d=1, rev=True, r0=pl.multiple_of((n - 1 - i * nch - q) * C, C)))
        for j in jobs:
            _rwkv_chunk_stage_a(j, cum, scr)
        for j in jobs:
            _rwkv_chunk_stage_b(j, m0, strict_f, incl_f, strict_b, incl_b)
        _rwkv_chunk_inverse(jobs)
        for j in jobs:
            _rwkv_chunk_stage_c(j)
        for j in jobs:
            _rwkv_chunk_stage_d(j)
        for j in jobs:
            _rwkv_chunk_stage_e(j, m0, eye_cat)
        for q in range(nch):
            y_f, pf = _rwkv_chunk_stage_f(jobs[2 * q], pf)
            y_b, pb = _rwkv_chunk_stage_f(jobs[2 * q + 1], pb)
            yf[pl.ds(jobs[2 * q]["r0"], C), :] = y_f
            yb[pl.ds(jobs[2 * q + 1]["r0"], C), :] = y_b
        return pf, pb

    if has_s0:
        p0 = tuple(jnp.where(m0, s0_ref[d, 0:RWKV_D, :], s0_ref[d, RWKV_D:, :]) for d in range(2))
    else:
        p0 = (jnp.zeros((RWKV_D, LANES), F32),) * 2
    pf, pb = lax.fori_loop(0, n // nch, sweep, p0)
    if want_state:
        sn_ref[0] = _blockdiag_rows(pf, 64)
        sn_ref[1] = _blockdiag_rows(pb, 64)

    def post(i, carry):
        r0 = pl.multiple_of(i * rb, rb)
        rows = pl.ds(r0, rb)
        y = yf[rows, :] + yb[rows, :]
        mu = _seg_sum(y, ones) * (1.0 / RWKV_D)
        dv = y - mu
        var = _seg_sum(dv * dv, ones) * (1.0 / RWKV_D)
        bonus = _seg_sum(rs[rows, :] * (kds[0, rows, :] + kds[1, rows, :]) * rkw_ref[...], ones) * vs[rows, :]
        out = dv * lax.rsqrt(var + RWKV_GN_EPS) * lnw_ref[...] + lnb_ref[...] + bonus
        g = _dot(_b16(jax.nn.sigmoid(g_ref[rows, :])), gup_ref[...])
        y_ref[rows, :] = (out * g).astype(y_ref.dtype)
        return carry

    lax.fori_loop(0, seq // rb, post, 0, unroll=min(PREP_UNROLL, seq // rb))


def _rwkv_chunked(p16, p32, lw, s0, n_batch, seq, want_state):
    has_s0 = s0 is not None
    n_tok = n_batch * seq
    base = M_RWKV // LANES
    tok = lambda off: pl.BlockSpec((seq, LANES), lambda b, p: (b, off + p))
    wcol = lambda rows: pl.BlockSpec((rows, LANES), lambda b, p: (0, p))
    in_specs = [tok(base), tok(base + 4), tok(base + 8),
                pl.BlockSpec((seq, LANES), lambda b, p: (b, S_WWD // LANES)),
                pl.BlockSpec((seq, LANES), lambda b, p: (b, S_WGD // LANES)),
                pl.BlockSpec((3, LANES), lambda b, p: (0, p)),
                pl.BlockSpec((3, LANES), lambda b, p: (0, 4 + p)),
                pl.BlockSpec((3, LANES), lambda b, p: (0, 8 + p)),
                wcol(2), pl.BlockSpec((2, LANES, LANES), lambda b, p: (0, 0, p)),
                wcol(2), pl.BlockSpec((2, LANES, LANES), lambda b, p: (0, 0, p)),
                wcol(1), wcol(1), wcol(1),
                pl.BlockSpec((LANES, LANES), lambda b, p: (0, p)), wcol(1), wcol(1)]
    args = [p16, p16, p16, p32, p32, lw["conv_w"], lw["conv_w"], lw["conv_w"], lw["w0"], lw["w_up"], lw["a0"],
            lw["a_up"], lw["k_k"], lw["k_a"], lw["r_k"], lw["g_up"], lw["ln_w"], lw["ln_b"]]
    st_spec = pl.BlockSpec((None, 2, None, LANES, LANES), lambda b, p: (b, 0, p, 0, 0))
    if has_s0:
        in_specs.append(st_spec)
        args.append(s0)
    out_specs = [pl.BlockSpec((seq, LANES), lambda b, p: (b, p))]
    out_shape = [jax.ShapeDtypeStruct((n_tok, MIX_W), BF16)]
    if want_state:
        out_specs.append(st_spec)
        out_shape.append(jax.ShapeDtypeStruct((n_batch, 2, RWKV_H // 2, LANES, LANES), F32))
    tokf = pltpu.VMEM((seq, LANES), F32)
    tok2 = pltpu.VMEM((2, seq, LANES), F32)
    res = pl.pallas_call(
        functools.partial(_rwkv_body, has_s0=has_s0, want_state=want_state, seq=seq),
        grid=(n_batch, RWKV_H // 2),
        in_specs=in_specs, out_specs=out_specs, out_shape=out_shape,
        scratch_shapes=[tokf, tokf, tokf, tok2, tok2, tok2, tokf, tokf],
        compiler_params=_params(("parallel", "parallel")),
        name="rwkv",
    )(*args)
    return res if want_state else (res[0], None)


def _rwkv_pack_state(st):
    B = st.shape[0]
    p = jnp.swapaxes(st, -1, -2).reshape(B, 2, RWKV_H // 2, 2, RWKV_D, RWKV_D)
    z = jnp.zeros_like(p[:, :, :, 0])
    return jnp.concatenate([jnp.concatenate([p[:, :, :, 0], z], axis=-1),
                            jnp.concatenate([z, p[:, :, :, 1]], axis=-1)], axis=-2)


def _rwkv_unpack_state(pbd):
    h0 = pbd[:, :, :, :RWKV_D, :RWKV_D]
    h1 = pbd[:, :, :, RWKV_D:, RWKV_D:]
    p = jnp.stack([h0, h1], axis=3)
    B = p.shape[0]
    return jnp.swapaxes(p, -1, -2).reshape(B, 2, RWKV_H, RWKV_D, RWKV_D)


def _merge_body(yr_ref, ys_ref, yw_ref, ym_ref, g_ref, wb_ref, wo_ref, x_ref, mod_ref, sn_ref, o_ref):
    ys = ys_ref[...].astype(F32)
    ys = ys * lax.rsqrt(jnp.mean(ys * ys, axis=-1, keepdims=True) + EPS) * sn_ref[...]
    branches = (yr_ref[...], ys.astype(BF16), yw_ref[...], ym_ref[...])
    merged = None
    for i, br in enumerate(branches):
        gate = jax.nn.sigmoid(g_ref[:, i * D_MODEL:(i + 1) * D_MODEL].astype(F32))
        term = gate * _dot(br, wb_ref[i])
        merged = term if merged is None else merged + term
    out = _dot(merged.astype(BF16), wo_ref[...])
    o_ref[...] = x_ref[...] + mod_ref[2:3, :] * out


def _merge(y_ret, y_ssd, y_rw, y_ml, p16, w_branch, w_out, x, mod, ssd_norm, tm, seq):
    n_tok = x.shape[0]
    mi = _mod_index(mod.shape[0], tm, seq)
    yb = pl.BlockSpec((tm, MIX_W), lambda i: (i, 0))
    xb = pl.BlockSpec((tm, D_MODEL), lambda i: (i, 0))
    return pl.pallas_call(
        _merge_body,
        grid=(n_tok // tm,),
        in_specs=[yb, yb, yb, yb,
                  pl.BlockSpec((tm, N_BRANCH * D_MODEL), lambda i: (i, M_GATE)),
                  pl.BlockSpec((N_BRANCH, MIX_W, D_MODEL), lambda i: (0, 0, 0)),
                  pl.BlockSpec((D_MODEL, D_MODEL), lambda i: (0, 0)),
                  xb,
                  pl.BlockSpec((None, 8, D_MODEL), lambda i: (mi(i), 0, 0)),
                  pl.BlockSpec((1, MIX_W), lambda i: (0, 0))],
        out_specs=xb,
        out_shape=jax.ShapeDtypeStruct((n_tok, D_MODEL), F32),
        compiler_params=_params(("parallel",)),
        name="merge",
    )(y_ret, y_ssd, y_rw, y_ml, p16, w_branch, w_out, x, mod, ssd_norm)


def _ffn_body(x_ref, xp_ref, xn_ref, mod_ref, nw_ref, uv_ref, ug_ref, cwv_ref, cwg_ref, cbv_ref, cbg_ref,
              dn_ref, fw_ref, o_ref, h_ref, *, tm, seq, final):
    i = pl.program_id(0)
    f = pl.program_id(1)
    shift, scale, gate = mod_ref[3:4, :], mod_ref[4:5, :], mod_ref[5:6, :]

    @pl.when(f == 0)
    def _():
        nw = nw_ref[...]
        h_ref[0:tm, :] = _rms_mod(x_ref[...], nw, shift, scale).astype(BF16)
        h_ref[tm:tm + 8, :] = _rms_mod(xp_ref[...], nw, shift, scale).astype(BF16)
        h_ref[tm + 8:tm + 16, :] = _rms_mod(xn_ref[...], nw, shift, scale).astype(BF16)
        o_ref[...] = jnp.zeros(o_ref.shape, F32)

    row = _iota((tm, 1), 0)
    pos = (i * tm + row) % seq
    first, last = pos == 0, pos == seq - 1
    h = h_ref[...]

    def conv(u_ref, cw_ref, cb_ref):
        u = _dot(h, u_ref[...])
        um = u[0:tm]
        u_dn = jnp.where(row == 0, u[tm + 7:tm + 8], pltpu.roll(um, 1, 0))
        u_up = jnp.where(row == tm - 1, u[tm + 8:tm + 9], pltpu.roll(um, tm - 1, 0))
        u_dn = jnp.where(first, 0.0, u_dn)
        u_up = jnp.where(last, 0.0, u_up)
        cw = cw_ref[...]
        return cw[0:1] * u_dn + cw[1:2] * um + cw[2:3] * u_up + cb_ref[...]

    act = conv(uv_ref, cwv_ref, cbv_ref) * _silu(conv(ug_ref, cwg_ref, cbg_ref))
    o_ref[...] += _dot(act.astype(BF16), dn_ref[...])

    @pl.when(f == pl.num_programs(1) - 1)
    def _():
        xn = x_ref[...] + gate * o_ref[...]
        if final:
            xn = xn * lax.rsqrt(jnp.mean(xn * xn, axis=-1, keepdims=True) + EPS) * fw_ref[...]
        o_ref[...] = xn


def _ffn(x, mod, nw, up, conv_w, conv_b, down, final_w, tm, seq, final):
    n_tok = x.shape[0]
    fc = FFN_DIM // 2
    nf = FFN_DIM // fc
    mi = _mod_index(mod.shape[0], tm, seq)
    n8 = n_tok // 8
    xb = pl.BlockSpec((tm, D_MODEL), lambda i, f: (i, 0))
    one = lambda i, f: (0, 0)
    return pl.pallas_call(
        functools.partial(_ffn_body, tm=tm, seq=seq, final=final),
        grid=(n_tok // tm, nf),
        in_specs=[xb,
                  pl.BlockSpec((8, D_MODEL), lambda i, f: (jnp.maximum(i * (tm // 8) - 1, 0), 0)),
                  pl.BlockSpec((8, D_MODEL), lambda i, f: (jnp.minimum((i + 1) * (tm // 8), n8 - 1), 0)),
                  pl.BlockSpec((None, 8, D_MODEL), lambda i, f: (mi(i), 0, 0)),
                  pl.BlockSpec((1, D_MODEL), one),
                  pl.BlockSpec((D_MODEL, fc), lambda i, f: (0, f)),
                  pl.BlockSpec((D_MODEL, fc), lambda i, f: (0, nf + f)),
                  pl.BlockSpec((3, fc), lambda i, f: (0, f)),
                  pl.BlockSpec((3, fc), lambda i, f: (0, nf + f)),
                  pl.BlockSpec((1, fc), lambda i, f: (0, f)),
                  pl.BlockSpec((1, fc), lambda i, f: (0, nf + f)),
                  pl.BlockSpec((fc, D_MODEL), lambda i, f: (f, 0)),
                  pl.BlockSpec((1, D_MODEL), one)],
        out_specs=xb,
        out_shape=jax.ShapeDtypeStruct((n_tok, D_MODEL), F32),
        scratch_shapes=[pltpu.VMEM((tm + 16, D_MODEL), BF16)],
        compiler_params=_params(("parallel", "arbitrary")),
        name="conv_ffn",
    )(x, x, x, mod, nw, up, up, conv_w, conv_w, conv_b, conv_b, down, final_w)


def _rope_tables(seq):
    rows = seq // GRID_W
    rr, cc = jnp.meshgrid(jnp.arange(rows), jnp.arange(GRID_W), indexing='ij')
    nf = RET_DK // 4
    inv = ROPE_BASE ** (-jnp.arange(nf, dtype=F32) / nf)
    ang = jnp.concatenate([rr.reshape(-1, 1) * inv, cc.reshape(-1, 1) * inv], axis=-1)
    cos, sin = jnp.cos(ang), jnp.sin(ang)
    return (jnp.tile(jnp.concatenate([cos, cos], axis=-1), (1, 2)),
            jnp.tile(jnp.concatenate([-sin, sin], axis=-1), (1, 2)))


def _lanes16(a):
    return jnp.zeros((1, LANES), F32).at[0, :16].set(a.reshape(-1).astype(F32))


def _pad_rows(w, lo):
    return jnp.zeros((2, LANES, MIX_W), F32).at[:, lo:lo + w.shape[1]].set(w).astype(BF16)


def _layer(x, mod, lw, states, rope, n_batch, seq, want_state, final, final_w):
    tm_p = 1024
    tm = 512
    p16, p32 = _in_proj(x, mod, lw["norm1"], lw["w_main"], lw["w_small"], tm_p, 2048, seq)
    st_ret, st_ssd, st_rwkv, st_c, st_nm = states
    cos, sin = rope if rope is not None else (None, None)
    y_ret, n_ret = _retention(lw["ret_lg"], p16, cos, sin, st_ret, n_batch, seq, want_state)
    y_ssd, n_ssd = _ssd(p16, p32, lw["ssd_conv_w"], lw["ssd_conv_b"], lw["ssd_alog"], lw["ssd_dtb"], lw["ssd_dd"],
                        st_ssd, n_batch, seq, want_state)
    y_rw, n_rwkv = _rwkv_chunked(p16, p32, lw["rwkv"], st_rwkv, n_batch, seq, want_state)
    y_ml, n_c, n_nm = _mlstm(p16, p32, lw["ml_gbias"], lw["ml_norm"], st_c, st_nm, n_batch, seq, want_state)
    x = _merge(y_ret, y_ssd, y_rw, y_ml, p16, lw["w_branch"], lw["w_out"], x, mod, lw["ssd_norm"], tm, seq)
    x = _ffn(x, mod, lw["norm2"], lw["ffn_up"], lw["ffn_conv_w"], lw["ffn_conv_b"], lw["ffn_down"], final_w,
             tm, seq, final)
    return x, (n_ret, n_ssd, n_rwkv, n_c, n_nm)


def kernel(x_prompt, x_sample, state_ret, state_ssd, state_rwkv, state_mlstm_c, state_mlstm_n, state_mlstm_m, c, c_ctx, ada_w, ada_b, norm1, norm2, w_in, ret_log_rate, ssd_conv_w, ssd_conv_b, ssd_A_log, ssd_dt_bias, ssd_D, ssd_norm, rwkv_conv_w, rwkv_w0, rwkv_w_up, rwkv_a0, rwkv_a_up, rwkv_g_up, rwkv_k_k, rwkv_k_a, rwkv_r_k, rwkv_ln_w, rwkv_ln_b, ml_i_bias, ml_f_bias, ml_norm, w_branch, w_out, ffn_up, ffn_conv_w, ffn_conv_b, ffn_down, final_norm):
    nb_c, seq_c, _ = x_prompt.shape
    nb_l, seq_l, _ = x_sample.shape
    assert seq_c % 256 == 0 and seq_l % 256 == 0 and nb_l * RWKV_H * 2 % LANES == 0

    cvec = jnp.zeros((16, D_MODEL), F32).at[0].set(c_ctx).at[1:1 + nb_l].set(c)
    mod_all = _modulation(cvec, ada_w, ada_b).reshape(DEPTH, 16, 6, D_MODEL)
    mod_all = jnp.pad(mod_all, ((0, 0), (0, 0), (0, 2), (0, 0)))
    rope = _rope_tables(seq_l)
    main_perm, small_perm, conv_perm = _main_perm(), _small_perm(), _ssd_conv_perm()
    row = lambda a: a.reshape(1, -1).astype(F32)
    final_w = row(final_norm)

    xp = x_prompt.reshape(nb_c * seq_c, D_MODEL)
    xs = x_sample.reshape(nb_l * seq_l, D_MODEL)
    new_states = []
    for l in range(DEPTH):
        lw = dict(
            norm1=row(norm1[l]), norm2=row(norm2[l]),
            w_main=_take_cols(w_in[l], main_perm).astype(BF16),
            w_small=_take_cols(w_in[l], small_perm).astype(BF16),
            ret_lg=-jnp.exp(ret_log_rate[l].astype(F32)),
            ssd_conv_w=_take_cols(ssd_conv_w[l], conv_perm), ssd_conv_b=_take_cols(row(ssd_conv_b[l]), conv_perm),
            ssd_alog=_lanes16(ssd_A_log[l]), ssd_dtb=_lanes16(ssd_dt_bias[l]),
            ssd_dd=jnp.repeat(ssd_D[l].astype(F32), SSD_P, axis=1), ssd_norm=row(ssd_norm[l]),
            rwkv=dict(conv_w=rwkv_conv_w[l], w0=rwkv_w0[l], w_up=_pad_rows(rwkv_w_up[l], 0),
                      a0=rwkv_a0[l], a_up=_pad_rows(rwkv_a_up[l], RWKV_W_LORA),
                      k_k=row(rwkv_k_k[l]), k_a=row(rwkv_k_a[l]), r_k=row(rwkv_r_k[l]),
                      g_up=rwkv_g_up[l].astype(BF16), ln_w=row(rwkv_ln_w[l]), ln_b=row(rwkv_ln_b[l])),
            ml_gbias=jnp.zeros((1, LANES), F32).at[0, 16:32].set(
                jnp.stack([ml_i_bias[l], ml_f_bias[l]], axis=1).reshape(-1)),
            ml_norm=row(ml_norm[l]),
            w_branch=w_branch[l].astype(BF16), w_out=w_out[l].astype(BF16),
            ffn_up=ffn_up[l].astype(BF16), ffn_conv_w=ffn_conv_w[l], ffn_conv_b=row(ffn_conv_b[l]),
            ffn_down=ffn_down[l].astype(BF16),
        )
        final = l == DEPTH - 1
        mod_c = mod_all[l, 0:1]
        mod_l = mod_all[l, 1:1 + nb_l]
        xp, st = _layer(xp, mod_c, lw, (None,) * 5, None, nb_c, seq_c, True, final, final_w)
        new_states.append(st)
        lat_states = (state_ret[:, l], _ssd_pack_state(state_ssd[:, l]), _rwkv_pack_state(state_rwkv[:, l]),
                      state_mlstm_c[:, l],
                      _ml_pack_nm(state_mlstm_n[:, l], state_mlstm_m[:, l]))
        xs, _ = _layer(xs, mod_l, lw, lat_states, rope, nb_l, seq_l, False, final, final_w)

    new_ret = jnp.stack([s[0] for s in new_states], axis=1)
    new_ssd = jnp.stack([_ssd_unpack_state(s[1]) for s in new_states], axis=1)
    new_rwkv = jnp.stack([_rwkv_unpack_state(s[2]) for s in new_states], axis=1)
    new_c = jnp.stack([s[3] for s in new_states], axis=1)
    nm = [_ml_unpack_nm(s[4]) for s in new_states]
    new_n = jnp.stack([a for a, _ in nm], axis=1)
    new_m = jnp.stack([b for _, b in nm], axis=1)
    return (xp.reshape(nb_c, seq_c, D_MODEL), xs.reshape(nb_l, seq_l, D_MODEL),
            new_ret, new_ssd, new_rwkv, new_c, new_n, new_m)
```

```python
import functools
import math

import jax
import jax.numpy as jnp
import numpy as np
from jax import lax
from jax.experimental import pallas as pl
from jax.experimental.pallas import tpu as pltpu

F32 = jnp.float32
BF16 = jnp.bfloat16

D_MODEL = 1024
DEPTH = 2
GRID_W = 64
CHUNK = 64
N_BRANCH = 4
MIX_W = 512
RET_H, RET_DK, RET_DV = 4, 64, 128
SSD_H, SSD_P, SSD_N, SSD_G = 8, 64, 64, 2
RWKV_H, RWKV_D = 8, 64
RWKV_W_LORA, RWKV_A_LORA, RWKV_G_LORA = 64, 64, 128
ML_H, ML_DK, ML_DV = 4, 64, 128
FFN_DIM = 2816
ROPE_BASE = 10000.0
EPS = 1e-6
GN_EPS = 1e-5
RWKV_GN_EPS = 64e-5

LANES = 128
VMEM_LIMIT = 56 * 1024 * 1024
PREP_UNROLL = 2
CHUNKS_PER_STEP = 8
RWKV_CHUNKS_PER_STEP = 8

_O_RQ, _O_RK, _O_RV, _O_RG = 0, 256, 512, 1024
_O_SZ, _O_SX, _O_SB, _O_SC, _O_SDT = 1536, 2048, 2560, 2688, 2816
_O_WRKV, _O_WWD, _O_WAD, _O_WGD = 2832, 4368, 4432, 4496
_O_MQ, _O_MK, _O_MV, _O_MO, _O_MIF, _O_MG = 4624, 4880, 5136, 5648, 6160, 6176

M_GATE, M_RQ, M_RK, M_RV, M_RG = 0, 4096, 4352, 4608, 5120
M_SZ, M_RWKV = 5632, 6144
M_MQ, M_MK, M_MV, M_MO = 7680, 7936, 8192, 8704
M_SX, M_SB, M_SC = 9216, 9728, 9984
N_MAIN = 10240
S_WGD, S_WWD, S_WAD, S_DT, S_MIF = 0, 128, 192, 256, 272
N_SMALL = 384


def _main_perm():
    idx = np.full((N_MAIN,), -1, np.int64)

    def put(dst, src, n):
        idx[dst:dst + n] = np.arange(src, src + n)

    put(M_GATE, _O_MG, 4096)
    put(M_RQ, _O_RQ, 256); put(M_RK, _O_RK, 256); put(M_RV, _O_RV, 512); put(M_RG, _O_RG, 512)
    put(M_SZ, _O_SZ, 512); put(M_RWKV, _O_WRKV, 1536)
    put(M_MQ, _O_MQ, 256); put(M_MK, _O_MK, 256); put(M_MV, _O_MV, 512); put(M_MO, _O_MO, 512)
    put(M_SX, _O_SX, 512)
    for g in range(SSD_G):
        put(M_SB + g * LANES, _O_SB + g * SSD_N, SSD_N)
        put(M_SC + g * LANES, _O_SC + g * SSD_N, SSD_N)
    return idx


def _small_perm():
    idx = np.full((N_SMALL,), -1, np.int64)
    idx[S_WGD:S_WGD + 128] = np.arange(_O_WGD, _O_WGD + 128)
    idx[S_WWD:S_WWD + 64] = np.arange(_O_WWD, _O_WWD + 64)
    idx[S_WAD:S_WAD + 64] = np.arange(_O_WAD, _O_WAD + 64)
    idx[S_DT:S_DT + 16] = np.arange(_O_SDT, _O_SDT + 16)
    idx[S_MIF:S_MIF + 16] = np.arange(_O_MIF, _O_MIF + 16)
    return idx


def _ssd_conv_perm():
    idx = np.full((1024,), -1, np.int64)
    idx[0:512] = np.arange(0, 512)
    for g in range(SSD_G):
        idx[512 + g * LANES:512 + g * LANES + SSD_N] = np.arange(512 + g * SSD_N, 512 + (g + 1) * SSD_N)
        idx[768 + g * LANES:768 + g * LANES + SSD_N] = np.arange(640 + g * SSD_N, 640 + (g + 1) * SSD_N)
    return idx


def _take_cols(a, idx):
    parts, start = [], 0
    for pos in range(1, len(idx) + 1):
        if pos == len(idx) or idx[pos] != idx[pos - 1] + (1 if idx[pos - 1] >= 0 else 0):
            n = pos - start
            if idx[start] < 0:
                parts.append(jnp.zeros(a.shape[:-1] + (n,), a.dtype))
            else:
                parts.append(a[..., int(idx[start]):int(idx[start]) + n])
            start = pos
    return jnp.concatenate(parts, axis=-1)


def _params(sem, vmem=VMEM_LIMIT):
    return pltpu.CompilerParams(dimension_semantics=sem, vmem_limit_bytes=vmem)


def _tdot(a, b, **kw):
    return lax.dot_general(a, b, (((0,), (0,)), ((), ())), preferred_element_type=F32, **kw)


def _ntdot(a, b, **kw):
    return lax.dot_general(a, b, (((1,), (1,)), ((), ())), preferred_element_type=F32, **kw)


def _dot(a, b, **kw):
    return jnp.dot(a, b, preferred_element_type=F32, **kw)


def _silu(x):
    return x * jax.nn.sigmoid(x)


def _softplus(x):
    return jnp.maximum(x, 0.0) + jnp.log1p(jnp.exp(-jnp.abs(x)))


def _iota(shape, dim):
    return lax.broadcasted_iota(jnp.int32, shape, dim)


def _mod_body(c_ref, w_ref, b_ref, o_ref):
    c = c_ref[...]
    o_ref[...] = _dot(_silu(c).astype(BF16), w_ref[...].astype(BF16)) + b_ref[...]


def _modulation(cvec, ada_w, ada_b):
    L = ada_w.shape[0]
    tn = 512
    return pl.pallas_call(
        _mod_body,
        grid=(L, 6 * D_MODEL // tn),
        in_specs=[pl.BlockSpec((16, D_MODEL), lambda l, j: (0, 0)),
                  pl.BlockSpec((None, D_MODEL, tn), lambda l, j: (l, 0, j)),
                  pl.BlockSpec((None, 1, tn), lambda l, j: (l, 0, j))],
        out_specs=pl.BlockSpec((None, 16, tn), lambda l, j: (l, 0, j)),
        out_shape=jax.ShapeDtypeStruct((L, 16, 6 * D_MODEL), F32),
        compiler_params=_params(("parallel", "parallel")),
        name="modulation",
    )(cvec, ada_w, ada_b.reshape(L, 1, 6 * D_MODEL))


def _rms_mod(x, nw, shift, scale):
    ms = jnp.mean(x * x, axis=-1, keepdims=True)
    return (x * lax.rsqrt(ms + EPS) * nw) * (1.0 + scale) + shift


def _in_proj_body(x_ref, mod_ref, nw_ref, w_ref, ws_ref, o_ref, os_ref, h_ref):
    @pl.when(pl.program_id(1) == 0)
    def _():
        h = _rms_mod(x_ref[...], nw_ref[...], mod_ref[0:1, :], mod_ref[1:2, :])
        h_ref[...] = h.astype(BF16)
        os_ref[...] = _dot(h_ref[...], ws_ref[...])

    o_ref[...] = _dot(h_ref[...], w_ref[...]).astype(o_ref.dtype)


def _mod_index(n_mod, tm, seq):
    if n_mod == 1:
        return lambda i: 0
    return lambda i: (i * tm) // seq


def _in_proj(x, mod, nw, w, w_small, tm, tn, seq):
    n_tok, n = x.shape[0], w.shape[1]
    mi = _mod_index(mod.shape[0], tm, seq)
    return pl.pallas_call(
        _in_proj_body,
        grid=(n_tok // tm, n // tn),
        in_specs=[pl.BlockSpec((tm, D_MODEL), lambda i, j: (i, 0)),
                  pl.BlockSpec((None, 8, D_MODEL), lambda i, j: (mi(i), 0, 0)),
                  pl.BlockSpec((1, D_MODEL), lambda i, j: (0, 0)),
                  pl.BlockSpec((D_MODEL, tn), lambda i, j: (0, j)),
                  pl.BlockSpec((D_MODEL, N_SMALL), lambda i, j: (0, 0))],
        out_specs=[pl.BlockSpec((tm, tn), lambda i, j: (i, j)),
                   pl.BlockSpec((tm, N_SMALL), lambda i, j: (i, 0))],
        out_shape=[jax.ShapeDtypeStruct((n_tok, n), BF16), jax.ShapeDtypeStruct((n_tok, N_SMALL), F32)],
        scratch_shapes=[pltpu.VMEM((tm, D_MODEL), BF16)],
        compiler_params=_params(("parallel", "arbitrary")),
        name="in_proj",
    )(x, mod, nw, w, w_small)


def _pair_masks():
    lane = _iota((1, LANES), 1)
    return lane < 64, lane


def _head_norm(y, eps):
    mu = jnp.mean(y, axis=-1, keepdims=True)
    d = y - mu
    var = jnp.mean(d * d, axis=-1, keepdims=True)
    return d * lax.rsqrt(var + eps)


def _blockdiag_rows(v, width):
    col = _iota((1, 2 * width), 1)
    left = col < width
    return jnp.concatenate([jnp.where(left, v, 0), jnp.where(left, 0, v)], axis=0)


def _tri_masks():
    ii = _iota((CHUNK, LANES), 0)
    jj = _iota((CHUNK, LANES), 1) % CHUNK
    return jj <= ii, jj >= ii, ii - jj


def _cumsum_mats():
    r = _iota((2 * CHUNK, CHUNK), 0)
    t = _iota((2 * CHUNK, CHUNK), 1)
    pre = (r < CHUNK) & (t <= r)
    suf = (r >= CHUNK) & (t >= r - CHUNK)
    return jnp.where(pre | suf, 1.0, 0.0).astype(F32)


def _split2(x):
    hi = x.astype(BF16)
    return hi, (x - hi.astype(F32)).astype(BF16)


def _sel_dot(sel, x):
    s = sel.astype(BF16)
    hi, lo = _split2(x)
    return _dot(s, hi) + _dot(s, lo)


def _dot_sel(x, sel):
    s = sel.astype(BF16)
    hi, lo = _split2(x)
    return _dot(hi, s) + _dot(lo, s)


def _sel_ntdot(sel, x):
    s = sel.astype(BF16)
    hi, lo = _split2(x)
    return _ntdot(s, hi) + _ntdot(s, lo)


def _seg_ones_pair():
    r = _iota((LANES, LANES), 0) // 64
    c = _iota((LANES, LANES), 1) // 64
    return jnp.where(r == c, 1.0, 0.0).astype(BF16)


def _seg_sum(x, ones):
    hi, lo = _split2(x)
    return _dot(hi, ones) + _dot(lo, ones)


def _pair_select(lane0s):
    k = len(lane0s)
    l = _iota((LANES, k * LANES), 0)
    c = _iota((LANES, k * LANES), 1)
    tgt = jnp.where((c % LANES) >= 64, 1, 0)
    for q in range(k):
        tgt = tgt + jnp.where(c // LANES == q, lane0s[q], 0)
    return jnp.where(l == tgt, 1.0, 0.0).astype(F32)


def _pair_pick(lane0s):
    r = _iota((8, LANES), 0)
    base = jnp.full((8, LANES), -8, jnp.int32)
    for q in range(len(lane0s)):
        base = jnp.where(r == q, lane0s[q], base)
    off = _iota((8, LANES), 1) - base
    return jnp.where((off == 0) | (off == 1), 1.0, 0.0).astype(F32)


def _even_odd_stack(x):
    even = (_iota((1, LANES), 1) % 2) == 0
    return jnp.concatenate([jnp.where(even, x, 0.0), jnp.where(even, 0.0, x)], axis=0)


def _ret_body(*refs, rope, has_s0, want_state, seq):
    it = iter(refs)
    lg_ref, q_ref, k_ref, v_ref, g_ref = (next(it) for _ in range(5))
    cos_ref = sin_ref = s0_ref = sn_ref = None
    if rope:
        cos_ref, sin_ref = next(it), next(it)
    if has_s0:
        s0_ref = next(it)
    y_ref = next(it)
    if want_state:
        sn_ref = next(it)
    qs, ks, sb_all = next(it), next(it), next(it)

    C = CHUNK
    n = seq // C
    hp = pl.program_id(1)
    m0, lane = _pair_masks()
    lgf0, lgf1 = lg_ref[0, 2 * hp], lg_ref[0, 2 * hp + 1]
    lgb0, lgb1 = lg_ref[1, 2 * hp], lg_ref[1, 2 * hp + 1]
    lgf = jnp.where(m0, lgf0, lgf1)
    lgb = jnp.where(m0, lgb0, lgb1)
    pos = _iota((C, 1), 0).astype(F32)
    e_qf = jnp.exp(lgf * (pos + 1.0))
    e_qb = jnp.exp(lgb * (C - pos))
    e_kf = jnp.exp(lgf * (C - 1.0 - pos))
    e_kb = jnp.exp(lgb * pos)
    row0 = _iota((LANES, 1), 0) < 64
    dec_f = jnp.exp(jnp.where(row0, lgf0, lgf1) * C)
    dec_b = jnp.exp(jnp.where(row0, lgb0, lgb1) * C)
    bd = (_iota((LANES, 2 * LANES), 0) < 64) == (_iota((LANES, 2 * LANES), 1) < LANES)
    _, _, diff = _tri_masks()
    dif = diff.astype(F32)
    dcomb = jnp.where(diff > 0, jnp.exp(lgf * jnp.maximum(dif, 0.0)),
                      jnp.where(diff < 0, jnp.exp(lgb * jnp.maximum(-dif, 0.0)), 2.0))

    rb = min(seq, 512)
    m32 = (lane % 64) < 32

    def swap(x):
        return jnp.where(m32, pltpu.roll(x, 96, 1), pltpu.roll(x, 32, 1))

    def prep(i, carry):
        r0 = pl.multiple_of(i * rb, rb)
        q = q_ref[pl.ds(r0, rb), :].astype(F32)
        k = k_ref[pl.ds(r0, rb), :].astype(F32)
        if rope:
            cs = cos_ref[pl.ds(r0, rb), :]
            sn = sin_ref[pl.ds(r0, rb), :]
            q = q * cs + swap(q) * sn
            k = k * cs + swap(k) * sn
        qs[pl.ds(r0, rb), :] = (q * RET_DK ** -0.5).astype(BF16)
        ks[pl.ds(r0, rb), :] = k.astype(BF16)
        return carry

    lax.fori_loop(0, seq // rb, prep, 0, unroll=min(PREP_UNROLL, seq // rb))

    def bd_state(d):
        if not has_s0:
            return jnp.zeros((LANES, 2 * LANES), F32)
        z = jnp.zeros((RET_DK, RET_DV), F32)
        return jnp.concatenate([jnp.concatenate([s0_ref[d, 0], z], axis=1),
                                jnp.concatenate([z, s0_ref[d, 1]], axis=1)], axis=0)

    nch = min(CHUNKS_PER_STEP, n)

    def bsweep(i, sb):
        kvs = []
        for q in range(nch):
            c = n - 1 - (i * nch + q)
            rows = pl.ds(pl.multiple_of(c * C, C), C)
            kb = (ks[rows, :].astype(F32) * e_kb).astype(BF16)
            kvs.append((c, _tdot(kb, v_ref[rows, :])))
        for c, kv in kvs:
            sb_all[c] = sb.astype(BF16)
            sb = dec_b * sb + jnp.where(bd, kv, 0.0)
        return sb

    sb_fin = lax.fori_loop(0, n // nch, bsweep, bd_state(1))

    def fsweep(i, sf):
        jobs = []
        for q in range(nch):
            c = i * nch + q
            rows = pl.ds(pl.multiple_of(c * C, C), C)
            qc, kc, vc = qs[rows, :], ks[rows, :], v_ref[rows, :]
            k2 = jnp.concatenate([jnp.where(m0, kc, 0), jnp.where(m0, 0, kc)], axis=0)
            kf = (kc.astype(F32) * e_kf).astype(BF16)
            jobs.append(dict(c=c, rows=rows, qc=qc, vc=vc, s2=_ntdot(qc, k2), kv=_tdot(kf, vc)))
        for j in jobs:
            rows, qf = j["rows"], j["qc"].astype(F32)
            lhs = jnp.concatenate([(j["s2"] * dcomb).astype(BF16), (qf * e_qf).astype(BF16),
                                   (qf * e_qb).astype(BF16)], axis=1)
            rhs = jnp.concatenate([_blockdiag_rows(j["vc"], LANES), sf.astype(BF16), sb_all[j["c"]]], axis=0)
            y = _dot(lhs, rhs)
            y = jnp.concatenate([_head_norm(y[:, :LANES], GN_EPS), _head_norm(y[:, LANES:], GN_EPS)], axis=1)
            y_ref[rows, :] = (_silu(g_ref[rows, :].astype(F32)) * y).astype(y_ref.dtype)
            sf = dec_f * sf + jnp.where(bd, j["kv"], 0.0)
        return sf

    sf_fin = lax.fori_loop(0, n // nch, fsweep, bd_state(0))

    if want_state:
        for d, s in ((0, sf_fin), (1, sb_fin)):
            sn_ref[d, 0] = s[:RET_DK, :RET_DV]
            sn_ref[d, 1] = s[RET_DK:, RET_DV:]


def _retention(lg, p16, cos, sin, s0, n_batch, seq, want_state):
    rope = cos is not None
    has_s0 = s0 is not None
    n_tok = n_batch * seq
    in_specs = [pl.BlockSpec(memory_space=pltpu.SMEM),
                pl.BlockSpec((seq, LANES), lambda b, p: (b, M_RQ // LANES + p)),
                pl.BlockSpec((seq, LANES), lambda b, p: (b, M_RK // LANES + p)),
                pl.BlockSpec((seq, 2 * LANES), lambda b, p: (b, M_RV // 256 + p)),
                pl.BlockSpec((seq, 2 * LANES), lambda b, p: (b, M_RG // 256 + p))]
    args = [lg, p16, p16, p16, p16]
    if rope:
        in_specs += [pl.BlockSpec((seq, LANES), lambda b, p: (0, 0))] * 2
        args += [cos, sin]
    st_spec = pl.BlockSpec((None, 2, 2, RET_DK, RET_DV), lambda b, p: (b, 0, p, 0, 0))
    if has_s0:
        in_specs.append(st_spec)
        args.append(s0)
    out_specs = [pl.BlockSpec((seq, 2 * LANES), lambda b, p: (b, p))]
    out_shape = [jax.ShapeDtypeStruct((n_tok, MIX_W), BF16)]
    if want_state:
        out_specs.append(st_spec)
        out_shape.append(jax.ShapeDtypeStruct((n_batch, 2, RET_H, RET_DK, RET_DV), F32))
    res = pl.pallas_call(
        functools.partial(_ret_body, rope=rope, has_s0=has_s0, want_state=want_state, seq=seq),
        grid=(n_batch, RET_H // 2),
        in_specs=in_specs, out_specs=out_specs, out_shape=out_shape,
        scratch_shapes=[pltpu.VMEM((seq, LANES), BF16), pltpu.VMEM((seq, LANES), BF16),
                        pltpu.VMEM((seq // CHUNK, LANES, 2 * LANES), BF16)],
        compiler_params=_params(("parallel", "parallel")),
        name="retention",
    )(*args)
    return res if want_state else (res[0], None)


def _ml_chunk_gate(bcol, icol, kc, n_x, m_x, reverse):
    C = CHUNK
    b_end = bcol[0:1] if reverse else bcol[C - 1:C]
    lwe = b_end - bcol + icol
    m_new = jnp.maximum(b_end + m_x, jnp.max(lwe, axis=0, keepdims=True))
    scale = jnp.exp(b_end + m_x - m_new)
    kw = kc.astype(F32) * jnp.exp(lwe - m_new)
    n_new = scale * n_x + jnp.sum(kw, axis=0, keepdims=True)
    return kw.astype(BF16), scale, n_new, m_new


def _chunk_cummax(x, reverse):
    rb = x.shape[0]
    row = _iota((rb, 1), 0) % CHUNK
    s = 1
    while s < CHUNK:
        if reverse:
            shifted, ok = pltpu.roll(x, rb - s, 0), row < CHUNK - s
        else:
            shifted, ok = pltpu.roll(x, s, 0), row >= s
        x = jnp.maximum(x, jnp.where(ok, shifted, -jnp.inf))
        s *= 2
    return x


def _ml_dir_att(s2, qf, bcol, brow, irow, mx, n_x, m_x, mask, ones):
    logw = jnp.where(mask, bcol - brow + irow, -jnp.inf)
    inter = bcol + m_x
    mt = jnp.maximum(inter, mx)
    att = s2 * jnp.exp(logw - mt)
    return dict(att=att, a_in=jnp.exp(inter - mt), mt=mt, rs=_seg_sum(att, ones), qs=_seg_sum(qf * n_x, ones))


def _ml_dir_lhs(t, qf):
    den = t["rs"] + t["a_in"] * t["qs"]
    sc = 1.0 / jnp.maximum(jnp.abs(den), jnp.exp(-t["mt"]))
    return (t["att"] * sc).astype(BF16), (qf * (t["a_in"] * sc)).astype(BF16)


def _ml_body(*refs, has_s0, want_state, seq):
    it = iter(refs)
    q_ref, k_ref, v_ref, o_ref, gt_ref, gb_ref, nw_ref = (next(it) for _ in range(7))
    s0_ref = nm0_ref = sn_ref = nmn_ref = None
    if has_s0:
        s0_ref, nm0_ref = next(it), next(it)
    y_ref = next(it)
    if want_state:
        sn_ref, nmn_ref = next(it), next(it)
    bcf_s, bcb_s, icf_s, icb_s, mxf_s, mxb_s, rows_s, cb_all, nmb_all = (next(it) for _ in range(9))
    ones = _seg_ones_pair()

    C = CHUNK
    n = seq // C
    nch = min(CHUNKS_PER_STEP, n)
    hp = pl.program_id(1)
    m0, _ = _pair_masks()
    row0 = _iota((LANES, 1), 0) < 64
    bd = (_iota((LANES, 2 * LANES), 0) < 64) == (_iota((LANES, 2 * LANES), 1) < LANES)
    tril, triu, _ = _tri_masks()
    cum = _cumsum_mats()
    gbias = gb_ref[...]
    lanes = [20 + 2 * hp, 28 + 2 * hp, 16 + 2 * hp, 24 + 2 * hp]
    sel = _pair_select(lanes)
    pick = _pair_pick(lanes)
    rb = min(seq, 256)

    def prep(i, carry):
        r0 = pl.multiple_of(i * rb, rb)
        rows = pl.ds(r0, rb)
        g = gt_ref[rows, :] + gbias
        lf = -_softplus(-g)
        cs2 = [_sel_dot(cum, lf[q * C:(q + 1) * C]) for q in range(rb // C)]
        pre = jnp.concatenate([c2[:C] for c2 in cs2], axis=0)
        suf = jnp.concatenate([c2[C:] for c2 in cs2], axis=0)
        e = _dot_sel(jnp.concatenate([pre, suf, g], axis=0), sel)
        bcf, bcb = e[0:rb, 0:LANES], e[rb:2 * rb, LANES:2 * LANES]
        icf, icb = e[2 * rb:, 2 * LANES:3 * LANES], e[2 * rb:, 3 * LANES:]
        bcf_s[rows, :] = bcf
        bcb_s[rows, :] = bcb
        icf_s[rows, :] = icf
        icb_s[rows, :] = icb
        mxf_s[rows, :] = bcf + _chunk_cummax(icf - bcf, False)
        mxb_s[rows, :] = bcb + _chunk_cummax(icb - bcb, True)
        for q in range(rb // C):
            st = jnp.concatenate([_even_odd_stack(cs2[q][:C]), _even_odd_stack(cs2[q][C:]),
                                  _even_odd_stack(g[q * C:(q + 1) * C])], axis=0)
            o = _sel_ntdot(pick, st)
            rows_s[i * (rb // C) + q] = jnp.concatenate(
                [o[0:1, 0:LANES], o[1:2, LANES:2 * LANES], o[2:3, 2 * LANES:], o[3:4, 2 * LANES:],
                 jnp.zeros((4, LANES), F32)], axis=0)
        return carry

    lax.fori_loop(0, seq // rb, prep, 0, unroll=min(PREP_UNROLL, seq // rb))

    def row_scale(scale):
        return jnp.where(row0, scale[:, 0:1], scale[:, 64:65])

    def init(d):
        if not has_s0:
            return (jnp.zeros((LANES, 2 * LANES), F32), jnp.zeros((1, LANES), F32), jnp.zeros((1, LANES), F32))
        z = jnp.zeros((ML_DK, ML_DV), F32)
        cst = jnp.concatenate([jnp.concatenate([s0_ref[d, 0], z], axis=1),
                               jnp.concatenate([z, s0_ref[d, 1]], axis=1)], axis=0)
        return cst, nm0_ref[d:d + 1, :], nm0_ref[2 + d:3 + d, :]

    def bsweep(i, carry):
        cst, n_x, m_x = carry
        terms = []
        for q in range(nch):
            c = n - 1 - (i * nch + q)
            rows = pl.ds(pl.multiple_of(c * C, C), C)
            kw, scale, n_new, m_new = _ml_chunk_gate(bcb_s[rows, :], icb_s[rows, :], k_ref[rows, :], n_x, m_x, True)
            terms.append((c, n_x, m_x, scale, _tdot(kw, v_ref[rows, :])))
            n_x, m_x = n_new, m_new
        for c, n_in, m_in, scale, kv in terms:
            cb_all[c] = cst.astype(BF16)
            nmb_all[c, 0:1, :] = n_in
            nmb_all[c, 1:2, :] = m_in
            cst = row_scale(scale) * cst + jnp.where(bd, kv, 0.0)
        return cst, n_x, m_x

    cb_fin = lax.fori_loop(0, n // nch, bsweep, init(1))

    def fsweep(i, carry):
        cst, n_x, m_x = carry
        jobs = []
        for q in range(nch):
            c = i * nch + q
            rows = pl.ds(pl.multiple_of(c * C, C), C)
            kc = k_ref[rows, :]
            vc = v_ref[rows, :]
            qf = q_ref[rows, :].astype(F32) * ML_DK ** -0.5
            k2 = jnp.concatenate([jnp.where(m0, kc, 0), jnp.where(m0, 0, kc)], axis=0)
            bcf = bcf_s[rows, :]
            kw, scale, n_new, m_new = _ml_chunk_gate(bcf, icf_s[rows, :], kc, n_x, m_x, False)
            jobs.append(dict(c=c, rows=rows, qf=qf, vc=vc, bcf=bcf, n_in=n_x, m_in=m_x, scale=scale,
                             s2=_ntdot(qf.astype(BF16), k2), kv=_tdot(kw, vc)))
            n_x, m_x = n_new, m_new
        for j in jobs:
            c, rows, s2, qf = j["c"], j["rows"], j["s2"], j["qf"]
            rw = rows_s[c]
            j["tf"] = _ml_dir_att(s2, qf, j["bcf"], rw[0:1], rw[2:3], mxf_s[rows, :], j["n_in"], j["m_in"], tril, ones)
            j["tb"] = _ml_dir_att(s2, qf, bcb_s[rows, :], rw[1:2], rw[3:4], mxb_s[rows, :], nmb_all[c, 0:1, :],
                                  nmb_all[c, 1:2, :], triu, ones)
        for j in jobs:
            c, rows, qf = j["c"], j["rows"], j["qf"]
            af, qaf = _ml_dir_lhs(j["tf"], qf)
            ab, qab = _ml_dir_lhs(j["tb"], qf)
            vbd = _blockdiag_rows(j["vc"], LANES)
            lhs = jnp.concatenate([af, qaf, ab, qab], axis=1)
            rhs = jnp.concatenate([vbd, cst.astype(BF16), vbd, cb_all[c]], axis=0)
            y = _dot(lhs, rhs)
            y = jnp.concatenate([_head_norm(y[:, :LANES], GN_EPS), _head_norm(y[:, LANES:], GN_EPS)], axis=1)
            y = jax.nn.sigmoid(o_ref[rows, :].astype(F32)) * (y * nw_ref[...])
            y_ref[rows, :] = y.astype(y_ref.dtype)
            cst = row_scale(j["scale"]) * cst + jnp.where(bd, j["kv"], 0.0)
        return cst, n_x, m_x

    cf_fin = lax.fori_loop(0, n // nch, fsweep, init(0))

    if want_state:
        nmn_ref[...] = jnp.zeros((8, LANES), F32)
        for d, (cst, n_x, m_x) in ((0, cf_fin), (1, cb_fin)):
            sn_ref[d, 0] = cst[:ML_DK, :ML_DV]
            sn_ref[d, 1] = cst[ML_DK:, ML_DV:]
            nmn_ref[d:d + 1, :] = n_x
            nmn_ref[2 + d:3 + d, :] = m_x


def _mlstm(p16, p32, gbias, nw, s0, nm0, n_batch, seq, want_state):
    has_s0 = s0 is not None
    n_tok = n_batch * seq
    in_specs = [pl.BlockSpec((seq, LANES), lambda b, p: (b, M_MQ // LANES + p)),
                pl.BlockSpec((seq, LANES), lambda b, p: (b, M_MK // LANES + p)),
                pl.BlockSpec((seq, 2 * LANES), lambda b, p: (b, M_MV // 256 + p)),
                pl.BlockSpec((seq, 2 * LANES), lambda b, p: (b, M_MO // 256 + p)),
                pl.BlockSpec((seq, LANES), lambda b, p: (b, S_DT // LANES)),
                pl.BlockSpec((1, LANES), lambda b, p: (0, 0)),
                pl.BlockSpec((1, 2 * LANES), lambda b, p: (0, p))]
    args = [p16, p16, p16, p16, p32, gbias, nw]
    st_spec = pl.BlockSpec((None, 2, 2, ML_DK, ML_DV), lambda b, p: (b, 0, p, 0, 0))
    nm_spec = pl.BlockSpec((None, None, 8, LANES), lambda b, p: (b, p, 0, 0))
    if has_s0:
        in_specs += [st_spec, nm_spec]
        args += [s0, nm0]
    out_specs = [pl.BlockSpec((seq, 2 * LANES), lambda b, p: (b, p))]
    out_shape = [jax.ShapeDtypeStruct((n_tok, MIX_W), BF16)]
    if want_state:
        out_specs += [st_spec, nm_spec]
        out_shape += [jax.ShapeDtypeStruct((n_batch, 2, ML_H, ML_DK, ML_DV), F32),
                      jax.ShapeDtypeStruct((n_batch, ML_H // 2, 8, LANES), F32)]
    res = pl.pallas_call(
        functools.partial(_ml_body, has_s0=has_s0, want_state=want_state, seq=seq),
        grid=(n_batch, ML_H // 2),
        in_specs=in_specs, out_specs=out_specs, out_shape=out_shape,
        scratch_shapes=[pltpu.VMEM((seq, LANES), F32)] * 6
        + [pltpu.VMEM((seq // CHUNK, 8, LANES), F32), pltpu.VMEM((seq // CHUNK, LANES, 2 * LANES), BF16),
           pltpu.VMEM((seq // CHUNK, 8, LANES), F32)],
        compiler_params=_params(("parallel", "parallel")),
        name="mlstm",
    )(*args)
    return res if want_state else (res[0], None, None)


def _ml_pack_nm(st_n, st_m):
    B = st_n.shape[0]
    n = st_n.reshape(B, 2, ML_H // 2, 2 * ML_DK).transpose(0, 2, 1, 3)
    m = jnp.repeat(st_m.reshape(B, 2, ML_H // 2, 2), ML_DK, axis=-1).transpose(0, 2, 1, 3)
    return jnp.concatenate([n, m, jnp.zeros((B, ML_H // 2, 4, LANES), F32)], axis=2)


def _ml_unpack_nm(nm):
    B = nm.shape[0]
    n = nm[:, :, 0:2, :].transpose(0, 2, 1, 3).reshape(B, 2, ML_H, ML_DK)
    m = nm[:, :, 2:4, :].transpose(0, 2, 1, 3).reshape(B, 2, ML_H, ML_DK)[..., 0]
    return n, m


def _conv3_rows(ref, r0, rb, seq, w, bias):
    x = ref[pl.ds(r0, rb), :].astype(F32)
    prev = ref[pl.ds(pl.multiple_of(jnp.maximum(r0 - 16, 0), 16), 16), :].astype(F32)[15:16]
    nxt = ref[pl.ds(pl.multiple_of(jnp.minimum(r0 + rb, seq - 16), 16), 16), :].astype(F32)[0:1]
    prev = jnp.where(r0 == 0, 0.0, prev)
    nxt = jnp.where(r0 + rb == seq, 0.0, nxt)
    row = _iota((rb, 1), 0)
    x_dn = jnp.where(row == 0, prev, pltpu.roll(x, 1, 0))
    x_up = jnp.where(row == rb - 1, nxt, pltpu.roll(x, rb - 1, 0))
    y = w[0:1] * x_dn + w[1:2] * x + w[2:3] * x_up
    return y if bias is None else y + bias


def _ssd_body(*refs, has_s0, want_state, seq):
    it = iter(refs)
    (x_ref, b_ref, c_ref, z_ref, dt_ref, wx_ref, wb_ref, wc_ref, bx_ref, bb_ref, bc_ref,
     alog_ref, dtb_ref, dd_ref) = (next(it) for _ in range(14))
    s0_ref = sn_ref = None
    if has_s0:
        s0_ref = next(it)
    y_ref = next(it)
    if want_state:
        sn_ref = next(it)
    xs, bs, cs_, ccf_s, ccb_s, dcf_s, dcb_s, rows_s, sb_all = (next(it) for _ in range(9))

    C = CHUNK
    n = seq // C
    nch = min(CHUNKS_PER_STEP, n)
    p = pl.program_id(1)
    tril, triu, _ = _tri_masks()
    a_lane = -jnp.exp(alog_ref[...])
    dsum = dd_ref[0:1, :] + dd_ref[1:2, :]
    rb = min(seq, 256)
    cum = _cumsum_mats()
    sel = _pair_select([2 * p, 8 + 2 * p])
    pick = _pair_pick([2 * p, 8 + 2 * p])

    def prep(i, carry):
        r0 = pl.multiple_of(i * rb, rb)
        rows = pl.ds(r0, rb)
        xs[rows, :] = _silu(_conv3_rows(x_ref, r0, rb, seq, wx_ref[...], bx_ref[...]))
        bs[rows, :] = _silu(_conv3_rows(b_ref, r0, rb, seq, wb_ref[...], bb_ref[...])).astype(BF16)
        cs_[rows, :] = _silu(_conv3_rows(c_ref, r0, rb, seq, wc_ref[...], bc_ref[...])).astype(BF16)
        dt = _softplus(dt_ref[rows, :] + dtb_ref[...])
        lw = dt * a_lane
        cs2 = [_sel_dot(cum, lw[q * C:(q + 1) * C]) for q in range(rb // C)]
        pre = jnp.concatenate([c2[:C] for c2 in cs2], axis=0)
        suf = jnp.concatenate([c2[C:] for c2 in cs2], axis=0)
        e = _dot_sel(jnp.concatenate([pre, suf, dt], axis=0), sel)
        ccf_s[rows, :] = e[0:rb, :LANES]
        ccb_s[rows, :] = e[rb:2 * rb, LANES:]
        dcf_s[rows, :] = e[2 * rb:, :LANES]
        dcb_s[rows, :] = e[2 * rb:, LANES:]
        for q in range(rb // C):
            st = jnp.concatenate([_even_odd_stack(cs2[q][:C]), _even_odd_stack(cs2[q][C:]),
                                  _even_odd_stack(dt[q * C:(q + 1) * C])], axis=0)
            o = _sel_ntdot(pick, st)
            rows_s[i * (rb // C) + q] = jnp.concatenate(
                [o[0:1, 0:LANES], o[1:2, LANES:2 * LANES], o[0:1, 2 * LANES:], o[1:2, 2 * LANES:],
                 jnp.zeros((4, LANES), F32)], axis=0)
        return carry

    lax.fori_loop(0, seq // rb, prep, 0, unroll=min(PREP_UNROLL, seq // rb))

    def state0(d):
        if not has_s0:
            return jnp.zeros((LANES, LANES), F32)
        return jnp.concatenate([s0_ref[d], jnp.zeros((SSD_N, LANES), F32)], axis=0)

    def state_update_terms(rows, ccol_s, dcol_s, rev):
        ccol = ccol_s[rows, :]
        last = ccol[0:1] if rev else ccol[C - 1:C]
        vdt = (xs[rows, :] * dcol_s[rows, :] * jnp.exp(last - ccol)).astype(BF16)
        return jnp.exp(last), _tdot(bs[rows, :], vdt)

    def bsweep(i, sb):
        terms = []
        for q in range(nch):
            c = n - 1 - (i * nch + q)
            rows = pl.ds(pl.multiple_of(c * C, C), C)
            terms.append((c,) + state_update_terms(rows, ccb_s, dcb_s, True))
        for c, dec, kv in terms:
            sb_all[c] = sb[:SSD_N].astype(BF16)
            sb = dec * sb + kv
        return sb

    sb_fin = lax.fori_loop(0, n // nch, bsweep, state0(1))

    def fsweep(i, sf):
        jobs = []
        for q in range(nch):
            c = i * nch + q
            rows = pl.ds(pl.multiple_of(c * C, C), C)
            j = dict(c=c, rows=rows, xc=xs[rows, :], bc=bs[rows, :], cc=cs_[rows, :],
                     ccf=ccf_s[rows, :], ccb=ccb_s[rows, :])
            rw = rows_s[c]
            j["m"] = (jnp.where(tril, jnp.exp(jnp.minimum(j["ccf"] - rw[0:1], 0.0)) * rw[2:3], 0.0)
                      + jnp.where(triu, jnp.exp(jnp.minimum(j["ccb"] - rw[1:2], 0.0)) * rw[3:4], 0.0))
            j["s2"] = _ntdot(j["cc"], jnp.concatenate([j["bc"], j["bc"]], axis=0))
            jobs.append(j)
        for j in jobs:
            att = (j["s2"] * j["m"]).astype(BF16)
            sbc = jnp.concatenate([sb_all[j["c"]], jnp.zeros((SSD_N, LANES), BF16)], axis=0)
            j["yi"] = _dot(att, _blockdiag_rows(j["xc"], 64).astype(BF16))
            j["ysb"] = _dot(j["cc"], sbc)
            j["dec"], j["kv"] = state_update_terms(j["rows"], ccf_s, dcf_s, False)
        for j in jobs:
            y = (j["yi"] + _dot(j["cc"], sf.astype(BF16)) * jnp.exp(j["ccf"]) + j["ysb"] * jnp.exp(j["ccb"])
                 + dsum * j["xc"])
            y_ref[j["rows"], :] = (y * _silu(z_ref[j["rows"], :].astype(F32))).astype(y_ref.dtype)
            sf = j["dec"] * sf + j["kv"]
        return sf

    sf_fin = lax.fori_loop(0, n // nch, fsweep, state0(0))
    if want_state:
        sn_ref[0] = sf_fin[:SSD_N]
        sn_ref[1] = sb_fin[:SSD_N]


def _ssd(p16, p32, conv_w, conv_b, alog, dtb, dd, s0, n_batch, seq, want_state):
    has_s0 = s0 is not None
    n_tok = n_batch * seq
    col = lambda off: (lambda b, p: (b, off // LANES + p))
    grp = lambda off: (lambda b, p: (b, off // LANES + p // 2))
    wcol = lambda off: (lambda b, p: (0, off + p))
    wgrp = lambda off: (lambda b, p: (0, off + p // 2))
    in_specs = [pl.BlockSpec((seq, LANES), col(M_SX)), pl.BlockSpec((seq, LANES), grp(M_SB)),
                pl.BlockSpec((seq, LANES), grp(M_SC)), pl.BlockSpec((seq, LANES), col(M_SZ)),
                pl.BlockSpec((seq, LANES), lambda b, p: (b, S_DT // LANES)),
                pl.BlockSpec((3, LANES), wcol(0)), pl.BlockSpec((3, LANES), wgrp(4)), pl.BlockSpec((3, LANES), wgrp(6)),
                pl.BlockSpec((1, LANES), wcol(0)), pl.BlockSpec((1, LANES), wgrp(4)), pl.BlockSpec((1, LANES), wgrp(6)),
                pl.BlockSpec((1, LANES), lambda b, p: (0, 0)), pl.BlockSpec((1, LANES), lambda b, p: (0, 0)),
                pl.BlockSpec((2, LANES), wcol(0))]
    args = [p16, p16, p16, p16, p32, conv_w, conv_w, conv_w, conv_b, conv_b, conv_b, alog, dtb, dd]
    st_spec = pl.BlockSpec((None, 2, None, SSD_N, LANES), lambda b, p: (b, 0, p, 0, 0))
    if has_s0:
        in_specs.append(st_spec)
        args.append(s0)
    out_specs = [pl.BlockSpec((seq, LANES), lambda b, p: (b, p))]
    out_shape = [jax.ShapeDtypeStruct((n_tok, MIX_W), BF16)]
    if want_state:
        out_specs.append(st_spec)
        out_shape.append(jax.ShapeDtypeStruct((n_batch, 2, SSD_H // 2, SSD_N, LANES), F32))
    res = pl.pallas_call(
        functools.partial(_ssd_body, has_s0=has_s0, want_state=want_state, seq=seq),
        grid=(n_batch, SSD_H // 2),
        in_specs=in_specs, out_specs=out_specs, out_shape=out_shape,
        scratch_shapes=[pltpu.VMEM((seq, LANES), F32), pltpu.VMEM((seq, LANES), BF16), pltpu.VMEM((seq, LANES), BF16)]
        + [pltpu.VMEM((seq, LANES), F32)] * 4
        + [pltpu.VMEM((seq // CHUNK, 8, LANES), F32), pltpu.VMEM((seq // CHUNK, SSD_N, LANES), BF16)],
        compiler_params=_params(("parallel", "parallel")),
        name="ssd",
    )(*args)
    return res if want_state else (res[0], None)


def _ssd_pack_state(st):
    B = st.shape[0]
    return st.reshape(B, 2, SSD_H // 2, 2, SSD_N, SSD_P).transpose(0, 1, 2, 4, 3, 5).reshape(B, 2, SSD_H // 2, SSD_N, 2 * SSD_P)


def _ssd_unpack_state(st):
    B = st.shape[0]
    return st.reshape(B, 2, SSD_H // 2, SSD_N, 2, SSD_P).transpose(0, 1, 2, 4, 3, 5).reshape(B, 2, SSD_H, SSD_N, SSD_P)


def _b16(x):
    return x.astype(BF16)


def _rwkv_chunk_stage_a(job, cum, refs):
    d, r0, rev = job["d"], job["r0"], job["rev"]
    rs, vs, as_, lws, kds, bs = refs
    C = CHUNK
    lw = lws[d, pl.ds(r0, C), :]
    g = _sel_dot(cum[C:] if rev else cum[:C], lw)
    g_end = g[0:1] if rev else g[C - 1:C]
    r, a = rs[pl.ds(r0, C), :], as_[pl.ds(r0, C), :]
    kd, b = kds[d, pl.ds(r0, C), :], bs[d, pl.ds(r0, C), :]
    e_g, e_ng, e_end = jnp.exp(g), jnp.exp(-g), jnp.exp(g_end - g)
    job.update(v=vs[pl.ds(r0, C), :], at=a * jnp.exp(g - lw), rt=r * e_g, bt=b * e_ng, kt=kd * e_ng,
               bh=b * e_end, kh=kd * e_end, dec=jnp.exp(g_end))


def _rwkv_chunk_stage_b(job, m0, strict_f, incl_f, strict_b, incl_b):
    C = CHUNK
    strict, incl = (strict_b, incl_b) if job["rev"] else (strict_f, incl_f)
    lhs = _b16(jnp.concatenate([job["at"], job["rt"]], axis=0))
    bt, kt = job["bt"], job["kt"]
    rhs = _b16(jnp.concatenate([jnp.where(m0, bt, 0.0), jnp.where(m0, 0.0, bt),
                                jnp.where(m0, kt, 0.0), jnp.where(m0, 0.0, kt)], axis=0))
    a4 = _ntdot(lhs, rhs)
    job.update(lab=jnp.where(strict, a4[:C, :LANES], 0.0), lak=jnp.where(strict, a4[:C, LANES:], 0.0),
               mrb=jnp.where(incl, a4[C:, :LANES], 0.0), mrk=jnp.where(incl, a4[C:, LANES:], 0.0))


RWKV_INV_BASE = 8


def _bd16(x):
    return _b16(_blockdiag_rows(x, 64))


def _rwkv_chunk_inverse(jobs):
    i = _iota((CHUNK, LANES), 0)
    c = _iota((CHUNK, LANES), 1) % CHUNK
    diag = (i // RWKV_INV_BASE) == (c // RWKV_INV_BASE)
    eye = jnp.where(i == c, 1.0, 0.0)
    for j in jobs:
        j["pw"] = jnp.where(diag, j["lab"], 0.0)
        j["tm"] = eye + j["pw"]
    s = 2
    while s < RWKV_INV_BASE:
        for j in jobs:
            j["pw"] = _dot(_b16(j["pw"]), _bd16(j["pw"]))
        for j in jobs:
            j["tm"] = j["tm"] + _dot(_b16(j["tm"]), _bd16(j["pw"]))
        s *= 2
    s = RWKV_INV_BASE
    while s < CHUNK:
        off = ((i // (2 * s)) == (c // (2 * s))) & ((i // s) != (c // s))
        for j in jobs:
            j["tl"] = _dot(_b16(j["tm"]), _bd16(jnp.where(off, j["lab"], 0.0)))
        for j in jobs:
            j["tm"] = j["tm"] + _dot(_b16(j["tl"]), _bd16(j["tm"]))
        s *= 2


def _rwkv_chunk_stage_c(job):
    v_rows = _blockdiag_rows(job["v"], 64)
    wv = _dot(_b16(job["lak"]), _b16(v_rows))
    job.update(v_rows=v_rows, x_rows=jnp.concatenate([_bd16(job["at"]), _bd16(wv)], axis=1))


def _rwkv_chunk_stage_d(job):
    tx = _dot(_b16(job["tm"]), job["x_rows"])
    job.update(ah=tx[:, :LANES], uv=tx[:, LANES:])


def _rwkv_chunk_stage_e(job, m0, eye):
    C = CHUNK
    ah, uv = job["ah"], job["uv"]
    z = jnp.zeros((2 * C, LANES), BF16)
    rhs_m = jnp.concatenate([jnp.concatenate([_bd16(ah), _bd16(uv)], axis=1),
                             jnp.concatenate([z, _b16(job["v_rows"])], axis=1)], axis=0)
    my = _dot(_b16(jnp.concatenate([job["mrb"], job["mrk"]], axis=1)), rhs_m)
    zc = jnp.zeros((C, LANES), F32)
    rhs_g = jnp.concatenate([jnp.concatenate([ah, uv], axis=1),
                             jnp.concatenate([zc, job["v"]], axis=1)], axis=0)
    gh = _tdot(_b16(jnp.concatenate([job["bh"], job["kh"]], axis=0)), _b16(rhs_g))
    job.update(rh=job["rt"] + my[:, :LANES], yv=my[:, LANES:],
               gm=jnp.where(m0, gh[:C, :LANES], gh[C:, :LANES]) + jnp.where(eye, job["dec"], 0.0),
               hm=jnp.where(m0, gh[:C, LANES:], gh[C:, LANES:]))


def _rwkv_chunk_stage_f(job, p):
    C = CHUNK
    out = _dot(_b16(jnp.concatenate([job["rh"], job["gm"]], axis=0)), _bd16(p))
    return out[:C] + job["yv"], out[C:] + job["hm"]


def _rwkv_body(*refs, has_s0, want_state, seq):
    it = iter(refs)
    (r_ref, k_ref, v_ref, lora_ref, g_ref, cwr_ref, cwk_ref, cwv_ref, w0_ref, wup_ref, a0_ref, aup_ref,
     kkw_ref, kaw_ref, rkw_ref, gup_ref, lnw_ref, lnb_ref) = (next(it) for _ in range(18))
    s0_ref = sn_ref = None
    if has_s0:
        s0_ref = next(it)
    y_ref = next(it)
    if want_state:
        sn_ref = next(it)
    rs, vs, as_, lws, kds, bs, yf, yb = (next(it) for _ in range(8))

    C = CHUNK
    n = seq // C
    m0, _ = _pair_masks()
    ones = _seg_ones_pair()
    rb = min(seq, 256)

    def prep(i, carry):
        r0 = pl.multiple_of(i * rb, rb)
        r_ = _conv3_rows(r_ref, r0, rb, seq, cwr_ref[...], None)
        k_ = _conv3_rows(k_ref, r0, rb, seq, cwk_ref[...], None)
        rs[pl.ds(r0, rb), :] = r_
        vs[pl.ds(r0, rb), :] = _conv3_rows(v_ref, r0, rb, seq, cwv_ref[...], None)
        kk = k_ * kkw_ref[...]
        kk = kk / jnp.maximum(jnp.sqrt(_seg_sum(kk * kk, ones)), 1e-12)
        as_[pl.ds(r0, rb), :] = -kk
        lora = lora_ref[pl.ds(r0, rb), :]
        w_low, a_low = _b16(jnp.tanh(lora)), _b16(lora)
        for d in range(2):
            z = w0_ref[d:d + 1, :] + _dot(w_low, wup_ref[d])
            lws[d, pl.ds(r0, rb), :] = -math.exp(-0.5) * jax.nn.sigmoid(z)
            iclr = jax.nn.sigmoid(a0_ref[d:d + 1, :] + _dot(a_low, aup_ref[d]))
            kds[d, pl.ds(r0, rb), :] = k_ * (1.0 + (iclr - 1.0) * kaw_ref[...])
            bs[d, pl.ds(r0, rb), :] = kk * iclr
        return carry

    lax.fori_loop(0, seq // rb, prep, 0, unroll=min(PREP_UNROLL, seq // rb))

    cum = _cumsum_mats()
    ii = _iota((C, LANES), 0)
    jj = _iota((C, LANES), 1) % C
    strict_f, incl_f, strict_b, incl_b = jj < ii, jj <= ii, jj > ii, jj >= ii
    eye_cat = ii == jj
    scr = (rs, vs, as_, lws, kds, bs)

    nch = min(RWKV_CHUNKS_PER_STEP, n)

    def sweep(i, carry):
        pf, pb = carry
        jobs = []
        for q in range(nch):
            jobs.append(dict(d=0, rev=False, r0=pl.multiple_of((i * nch + q) * C, C)))
            jobs.append(dict(d=1, rev=True, r0=pl.multiple_of((n - 1 - i * nch - q) * C, C)))
        for j in jobs:
            _rwkv_chunk_stage_a(j, cum, scr)
        for j in jobs:
            _rwkv_chunk_stage_b(j, m0, strict_f, incl_f, strict_b, incl_b)
        _rwkv_chunk_inverse(jobs)
        for j in jobs:
            _rwkv_chunk_stage_c(j)
        for j in jobs:
            _rwkv_chunk_stage_d(j)
        for j in jobs:
            _rwkv_chunk_stage_e(j, m0, eye_cat)
        for q in range(nch):
            y_f, pf = _rwkv_chunk_stage_f(jobs[2 * q], pf)
            y_b, pb = _rwkv_chunk_stage_f(jobs[2 * q + 1], pb)
            yf[pl.ds(jobs[2 * q]["r0"], C), :] = y_f
            yb[pl.ds(jobs[2 * q + 1]["r0"], C), :] = y_b
        return pf, pb

    if has_s0:
        p0 = tuple(jnp.where(m0, s0_ref[d, 0:RWKV_D, :], s0_ref[d, RWKV_D:, :]) for d in range(2))
    else:
        p0 = (jnp.zeros((RWKV_D, LANES), F32),) * 2
    pf, pb = lax.fori_loop(0, n // nch, sweep, p0)
    if want_state:
        sn_ref[0] = _blockdiag_rows(pf, 64)
        sn_ref[1] = _blockdiag_rows(pb, 64)

    def post(i, carry):
        r0 = pl.multiple_of(i * rb, rb)
        rows = pl.ds(r0, rb)
        y = yf[rows, :] + yb[rows, :]
        mu = _seg_sum(y, ones) * (1.0 / RWKV_D)
        dv = y - mu
        var = _seg_sum(dv * dv, ones) * (1.0 / RWKV_D)
        bonus = _seg_sum(rs[rows, :] * (kds[0, rows, :] + kds[1, rows, :]) * rkw_ref[...], ones) * vs[rows, :]
        out = dv * lax.rsqrt(var + RWKV_GN_EPS) * lnw_ref[...] + lnb_ref[...] + bonus
        g = _dot(_b16(jax.nn.sigmoid(g_ref[rows, :])), gup_ref[...])
        y_ref[rows, :] = (out * g).astype(y_ref.dtype)
        return carry

    lax.fori_loop(0, seq // rb, post, 0, unroll=min(PREP_UNROLL, seq // rb))


def _rwkv_chunked(p16, p32, lw, s0, n_batch, seq, want_state):
    has_s0 = s0 is not None
    n_tok = n_batch * seq
    base = M_RWKV // LANES
    tok = lambda off: pl.BlockSpec((seq, LANES), lambda b, p: (b, off + p))
    wcol = lambda rows: pl.BlockSpec((rows, LANES), lambda b, p: (0, p))
    in_specs = [tok(base), tok(base + 4), tok(base + 8),
                pl.BlockSpec((seq, LANES), lambda b, p: (b, S_WWD // LANES)),
                pl.BlockSpec((seq, LANES), lambda b, p: (b, S_WGD // LANES)),
                pl.BlockSpec((3, LANES), lambda b, p: (0, p)),
                pl.BlockSpec((3, LANES), lambda b, p: (0, 4 + p)),
                pl.BlockSpec((3, LANES), lambda b, p: (0, 8 + p)),
                wcol(2), pl.BlockSpec((2, LANES, LANES), lambda b, p: (0, 0, p)),
                wcol(2), pl.BlockSpec((2, LANES, LANES), lambda b, p: (0, 0, p)),
                wcol(1), wcol(1), wcol(1),
                pl.BlockSpec((LANES, LANES), lambda b, p: (0, p)), wcol(1), wcol(1)]
    args = [p16, p16, p16, p32, p32, lw["conv_w"], lw["conv_w"], lw["conv_w"], lw["w0"], lw["w_up"], lw["a0"],
            lw["a_up"], lw["k_k"], lw["k_a"], lw["r_k"], lw["g_up"], lw["ln_w"], lw["ln_b"]]
    st_spec = pl.BlockSpec((None, 2, None, LANES, LANES), lambda b, p: (b, 0, p, 0, 0))
    if has_s0:
        in_specs.append(st_spec)
        args.append(s0)
    out_specs = [pl.BlockSpec((seq, LANES), lambda b, p: (b, p))]
    out_shape = [jax.ShapeDtypeStruct((n_tok, MIX_W), BF16)]
    if want_state:
        out_specs.append(st_spec)
        out_shape.append(jax.ShapeDtypeStruct((n_batch, 2, RWKV_H // 2, LANES, LANES), F32))
    tokf = pltpu.VMEM((seq, LANES), F32)
    tok2 = pltpu.VMEM((2, seq, LANES), F32)
    res = pl.pallas_call(
        functools.partial(_rwkv_body, has_s0=has_s0, want_state=want_state, seq=seq),
        grid=(n_batch, RWKV_H // 2),
        in_specs=in_specs, out_specs=out_specs, out_shape=out_shape,
        scratch_shapes=[tokf, tokf, tokf, tok2, tok2, tok2, tokf, tokf],
        compiler_params=_params(("parallel", "parallel")),
        name="rwkv",
    )(*args)
    return res if want_state else (res[0], None)


def _rwkv_pack_state(st):
    B = st.shape[0]
    p = jnp.swapaxes(st, -1, -2).reshape(B, 2, RWKV_H // 2, 2, RWKV_D, RWKV_D)
    z = jnp.zeros_like(p[:, :, :, 0])
    return jnp.concatenate([jnp.concatenate([p[:, :, :, 0], z], axis=-1),
                            jnp.concatenate([z, p[:, :, :, 1]], axis=-1)], axis=-2)


def _rwkv_unpack_state(pbd):
    h0 = pbd[:, :, :, :RWKV_D, :RWKV_D]
    h1 = pbd[:, :, :, RWKV_D:, RWKV_D:]
    p = jnp.stack([h0, h1], axis=3)
    B = p.shape[0]
    return jnp.swapaxes(p, -1, -2).reshape(B, 2, RWKV_H, RWKV_D, RWKV_D)


def _merge_body(yr_ref, ys_ref, yw_ref, ym_ref, g_ref, wb_ref, wo_ref, x_ref, mod_ref, sn_ref, o_ref):
    ys = ys_ref[...].astype(F32)
    ys = ys * lax.rsqrt(jnp.mean(ys * ys, axis=-1, keepdims=True) + EPS) * sn_ref[...]
    branches = (yr_ref[...], ys.astype(BF16), yw_ref[...], ym_ref[...])
    merged = None
    for i, br in enumerate(branches):
        gate = jax.nn.sigmoid(g_ref[:, i * D_MODEL:(i + 1) * D_MODEL].astype(F32))
        term = gate * _dot(br, wb_ref[i])
        merged = term if merged is None else merged + term
    out = _dot(merged.astype(BF16), wo_ref[...])
    o_ref[...] = x_ref[...] + mod_ref[2:3, :] * out


def _merge(y_ret, y_ssd, y_rw, y_ml, p16, w_branch, w_out, x, mod, ssd_norm, tm, seq):
    n_tok = x.shape[0]
    mi = _mod_index(mod.shape[0], tm, seq)
    yb = pl.BlockSpec((tm, MIX_W), lambda i: (i, 0))
    xb = pl.BlockSpec((tm, D_MODEL), lambda i: (i, 0))
    return pl.pallas_call(
        _merge_body,
        grid=(n_tok // tm,),
        in_specs=[yb, yb, yb, yb,
                  pl.BlockSpec((tm, N_BRANCH * D_MODEL), lambda i: (i, M_GATE)),
                  pl.BlockSpec((N_BRANCH, MIX_W, D_MODEL), lambda i: (0, 0, 0)),
                  pl.BlockSpec((D_MODEL, D_MODEL), lambda i: (0, 0)),
                  xb,
                  pl.BlockSpec((None, 8, D_MODEL), lambda i: (mi(i), 0, 0)),
                  pl.BlockSpec((1, MIX_W), lambda i: (0, 0))],
        out_specs=xb,
        out_shape=jax.ShapeDtypeStruct((n_tok, D_MODEL), F32),
        compiler_params=_params(("parallel",)),
        name="merge",
    )(y_ret, y_ssd, y_rw, y_ml, p16, w_branch, w_out, x, mod, ssd_norm)


def _ffn_body(x_ref, xp_ref, xn_ref, mod_ref, nw_ref, uv_ref, ug_ref, cwv_ref, cwg_ref, cbv_ref, cbg_ref,
              dn_ref, fw_ref, o_ref, h_ref, *, tm, seq, final):
    i = pl.program_id(0)
    f = pl.program_id(1)
    shift, scale, gate = mod_ref[3:4, :], mod_ref[4:5, :], mod_ref[5:6, :]

    @pl.when(f == 0)
    def _():
        nw = nw_ref[...]
        h_ref[0:tm, :] = _rms_mod(x_ref[...], nw, shift, scale).astype(BF16)
        h_ref[tm:tm + 8, :] = _rms_mod(xp_ref[...], nw, shift, scale).astype(BF16)
        h_ref[tm + 8:tm + 16, :] = _rms_mod(xn_ref[...], nw, shift, scale).astype(BF16)
        o_ref[...] = jnp.zeros(o_ref.shape, F32)

    row = _iota((tm, 1), 0)
    pos = (i * tm + row) % seq
    first, last = pos == 0, pos == seq - 1
    h = h_ref[...]

    def conv(u_ref, cw_ref, cb_ref):
        u = _dot(h, u_ref[...])
        um = u[0:tm]
        u_dn = jnp.where(row == 0, u[tm + 7:tm + 8], pltpu.roll(um, 1, 0))
        u_up = jnp.where(row == tm - 1, u[tm + 8:tm + 9], pltpu.roll(um, tm - 1, 0))
        u_dn = jnp.where(first, 0.0, u_dn)
        u_up = jnp.where(last, 0.0, u_up)
        cw = cw_ref[...]
        return cw[0:1] * u_dn + cw[1:2] * um + cw[2:3] * u_up + cb_ref[...]

    act = conv(uv_ref, cwv_ref, cbv_ref) * _silu(conv(ug_ref, cwg_ref, cbg_ref))
    o_ref[...] += _dot(act.astype(BF16), dn_ref[...])

    @pl.when(f == pl.num_programs(1) - 1)
    def _():
        xn = x_ref[...] + gate * o_ref[...]
        if final:
            xn = xn * lax.rsqrt(jnp.mean(xn * xn, axis=-1, keepdims=True) + EPS) * fw_ref[...]
        o_ref[...] = xn


def _ffn(x, mod, nw, up, conv_w, conv_b, down, final_w, tm, seq, final):
    n_tok = x.shape[0]
    fc = FFN_DIM // 2
    nf = FFN_DIM // fc
    mi = _mod_index(mod.shape[0], tm, seq)
    n8 = n_tok // 8
    xb = pl.BlockSpec((tm, D_MODEL), lambda i, f: (i, 0))
    one = lambda i, f: (0, 0)
    return pl.pallas_call(
        functools.partial(_ffn_body, tm=tm, seq=seq, final=final),
        grid=(n_tok // tm, nf),
        in_specs=[xb,
                  pl.BlockSpec((8, D_MODEL), lambda i, f: (jnp.maximum(i * (tm // 8) - 1, 0), 0)),
                  pl.BlockSpec((8, D_MODEL), lambda i, f: (jnp.minimum((i + 1) * (tm // 8), n8 - 1), 0)),
                  pl.BlockSpec((None, 8, D_MODEL), lambda i, f: (mi(i), 0, 0)),
                  pl.BlockSpec((1, D_MODEL), one),
                  pl.BlockSpec((D_MODEL, fc), lambda i, f: (0, f)),
                  pl.BlockSpec((D_MODEL, fc), lambda i, f: (0, nf + f)),
                  pl.BlockSpec((3, fc), lambda i, f: (0, f)),
                  pl.BlockSpec((3, fc), lambda i, f: (0, nf + f)),
                  pl.BlockSpec((1, fc), lambda i, f: (0, f)),
                  pl.BlockSpec((1, fc), lambda i, f: (0, nf + f)),
                  pl.BlockSpec((fc, D_MODEL), lambda i, f: (f, 0)),
                  pl.BlockSpec((1, D_MODEL), one)],
        out_specs=xb,
        out_shape=jax.ShapeDtypeStruct((n_tok, D_MODEL), F32),
        scratch_shapes=[pltpu.VMEM((tm + 16, D_MODEL), BF16)],
        compiler_params=_params(("parallel", "arbitrary")),
        name="conv_ffn",
    )(x, x, x, mod, nw, up, up, conv_w, conv_w, conv_b, conv_b, down, final_w)


def _rope_tables(seq):
    rows = seq // GRID_W
    rr, cc = jnp.meshgrid(jnp.arange(rows), jnp.arange(GRID_W), indexing='ij')
    nf = RET_DK // 4
    inv = ROPE_BASE ** (-jnp.arange(nf, dtype=F32) / nf)
    ang = jnp.concatenate([rr.reshape(-1, 1) * inv, cc.reshape(-1, 1) * inv], axis=-1)
    cos, sin = jnp.cos(ang), jnp.sin(ang)
    return (jnp.tile(jnp.concatenate([cos, cos], axis=-1), (1, 2)),
            jnp.tile(jnp.concatenate([-sin, sin], axis=-1), (1, 2)))


def _lanes16(a):
    return jnp.zeros((1, LANES), F32).at[0, :16].set(a.reshape(-1).astype(F32))


def _pad_rows(w, lo):
    return jnp.zeros((2, LANES, MIX_W), F32).at[:, lo:lo + w.shape[1]].set(w).astype(BF16)


def _layer(x, mod, lw, states, rope, n_batch, seq, want_state, final, final_w):
    tm_p = 1024
    tm = 512
    p16, p32 = _in_proj(x, mod, lw["norm1"], lw["w_main"], lw["w_small"], tm_p, 2048, seq)
    st_ret, st_ssd, st_rwkv, st_c, st_nm = states
    cos, sin = rope if rope is not None else (None, None)
    y_ret, n_ret = _retention(lw["ret_lg"], p16, cos, sin, st_ret, n_batch, seq, want_state)
    y_ssd, n_ssd = _ssd(p16, p32, lw["ssd_conv_w"], lw["ssd_conv_b"], lw["ssd_alog"], lw["ssd_dtb"], lw["ssd_dd"],
                        st_ssd, n_batch, seq, want_state)
    y_rw, n_rwkv = _rwkv_chunked(p16, p32, lw["rwkv"], st_rwkv, n_batch, seq, want_state)
    y_ml, n_c, n_nm = _mlstm(p16, p32, lw["ml_gbias"], lw["ml_norm"], st_c, st_nm, n_batch, seq, want_state)
    x = _merge(y_ret, y_ssd, y_rw, y_ml, p16, lw["w_branch"], lw["w_out"], x, mod, lw["ssd_norm"], tm, seq)
    x = _ffn(x, mod, lw["norm2"], lw["ffn_up"], lw["ffn_conv_w"], lw["ffn_conv_b"], lw["ffn_down"], final_w,
             tm, seq, final)
    return x, (n_ret, n_ssd, n_rwkv, n_c, n_nm)


def kernel(x_prompt, x_sample, state_ret, state_ssd, state_rwkv, state_mlstm_c, state_mlstm_n, state_mlstm_m, c, c_ctx, ada_w, ada_b, norm1, norm2, w_in, ret_log_rate, ssd_conv_w, ssd_conv_b, ssd_A_log, ssd_dt_bias, ssd_D, ssd_norm, rwkv_conv_w, rwkv_w0, rwkv_w_up, rwkv_a0, rwkv_a_up, rwkv_g_up, rwkv_k_k, rwkv_k_a, rwkv_r_k, rwkv_ln_w, rwkv_ln_b, ml_i_bias, ml_f_bias, ml_norm, w_branch, w_out, ffn_up, ffn_conv_w, ffn_conv_b, ffn_down, final_norm):
    nb_c, seq_c, _ = x_prompt.shape
    nb_l, seq_l, _ = x_sample.shape
    assert seq_c % 256 == 0 and seq_l % 256 == 0 and nb_l * RWKV_H * 2 % LANES == 0

    cvec = jnp.zeros((16, D_MODEL), F32).at[0].set(c_ctx).at[1:1 + nb_l].set(c)
    mod_all = _modulation(cvec, ada_w, ada_b).reshape(DEPTH, 16, 6, D_MODEL)
    mod_all = jnp.pad(mod_all, ((0, 0), (0, 0), (0, 2), (0, 0)))
    rope = _rope_tables(seq_l)
    main_perm, small_perm, conv_perm = _main_perm(), _small_perm(), _ssd_conv_perm()
    row = lambda a: a.reshape(1, -1).astype(F32)
    final_w = row(final_norm)

    xp = x_prompt.reshape(nb_c * seq_c, D_MODEL)
    xs = x_sample.reshape(nb_l * seq_l, D_MODEL)
    new_states = []
    for l in range(DEPTH):
        lw = dict(
            norm1=row(norm1[l]), norm2=row(norm2[l]),
            w_main=_take_cols(w_in[l], main_perm).astype(BF16),
            w_small=_take_cols(w_in[l], small_perm).astype(BF16),
            ret_lg=-jnp.exp(ret_log_rate[l].astype(F32)),
            ssd_conv_w=_take_cols(ssd_conv_w[l], conv_perm), ssd_conv_b=_take_cols(row(ssd_conv_b[l]), conv_perm),
            ssd_alog=_lanes16(ssd_A_log[l]), ssd_dtb=_lanes16(ssd_dt_bias[l]),
            ssd_dd=jnp.repeat(ssd_D[l].astype(F32), SSD_P, axis=1), ssd_norm=row(ssd_norm[l]),
            rwkv=dict(conv_w=rwkv_conv_w[l], w0=rwkv_w0[l], w_up=_pad_rows(rwkv_w_up[l], 0),
                      a0=rwkv_a0[l], a_up=_pad_rows(rwkv_a_up[l], RWKV_W_LORA),
                      k_k=row(rwkv_k_k[l]), k_a=row(rwkv_k_a[l]), r_k=row(rwkv_r_k[l]),
                      g_up=rwkv_g_up[l].astype(BF16), ln_w=row(rwkv_ln_w[l]), ln_b=row(rwkv_ln_b[l])),
            ml_gbias=jnp.zeros((1, LANES), F32).at[0, 16:32].set(
                jnp.stack([ml_i_bias[l], ml_f_bias[l]], axis=1).reshape(-1)),
            ml_norm=row(ml_norm[l]),
            w_branch=w_branch[l].astype(BF16), w_out=w_out[l].astype(BF16),
            ffn_up=ffn_up[l].astype(BF16), ffn_conv_w=ffn_conv_w[l], ffn_conv_b=row(ffn_conv_b[l]),
            ffn_down=ffn_down[l].astype(BF16),
        )
        final = l == DEPTH - 1
        mod_c = mod_all[l, 0:1]
        mod_l = mod_all[l, 1:1 + nb_l]
        xp, st = _layer(xp, mod_c, lw, (None,) * 5, None, nb_c, seq_c, True, final, final_w)
        new_states.append(st)
        lat_states = (state_ret[:, l], _ssd_pack_state(state_ssd[:, l]), _rwkv_pack_state(state_rwkv[:, l]),
                      state_mlstm_c[:, l],
                      _ml_pack_nm(state_mlstm_n[:, l], state_mlstm_m[:, l]))
        xs, _ = _layer(xs, mod_l, lw, lat_states, rope, nb_l, seq_l, False, final, final_w)

    new_ret = jnp.stack([s[0] for s in new_states], axis=1)
    new_ssd = jnp.stack([_ssd_unpack_state(s[1]) for s in new_states], axis=1)
    new_rwkv = jnp.stack([_rwkv_unpack_state(s[2]) for s in new_states], axis=1)
    new_c = jnp.stack([s[3] for s in new_states], axis=1)
    nm = [_ml_unpack_nm(s[4]) for s in new_states]
    new_n = jnp.stack([a for a, _ in nm], axis=1)
    new_m = jnp.stack([b for _, b in nm], axis=1)
    return (xp.reshape(nb_c, seq_c, D_MODEL), xs.reshape(nb_l, seq_l, D_MODEL),
            new_ret, new_ssd, new_rwkv, new_c, new_n, new_m)
```
